```python
import math
import jax, jax.numpy as jnp
from jax import lax
import numpy as np

D_MODEL = 2048
BATCH = 8
SEQ = 8192
DEPTH = 4

CHUNK = 64
N_MEM = 256
N_BRANCH = 3
D_CONV = D_MODEL // 2
CONV_WIDTH = 31
ATTN_HEAD_DIM = 64
D_ATTN = D_MODEL // 2
N_ATTN_HEADS = D_ATTN // ATTN_HEAD_DIM
LEFT_CHUNKS = 8
BAND = LEFT_CHUNKS + 1
MAX_REL = 256
N_MEM_HEADS = 4
D_MEM = D_MODEL // 2
MEM_HEAD_DIM = D_MEM // N_MEM_HEADS
D_FF = ((8 * D_MODEL // 3 + 255) // 256) * 256
FFN_CONV_WIDTH = 3
D_IN = 2 * D_CONV + 3 * D_ATTN + D_MEM + N_BRANCH * D_MODEL
SPLITS = (2 * D_CONV, 2 * D_CONV + 3 * D_ATTN, 2 * D_CONV + 3 * D_ATTN + D_MEM)
EPS = 1e-6
NEG_INF = -1e30

kernel_name = 'hybrid_gated_conv_chunkattn_mem_encoder'


def rmsnorm(x, g):
    xf = x.astype(jnp.float32)
    y = xf * lax.rsqrt(jnp.mean(xf * xf, axis=-1, keepdims=True) + EPS)
    return (y * g.astype(jnp.float32)).astype(x.dtype)


def layernorm(x, g, b):
    xf = x.astype(jnp.float32)
    mu = jnp.mean(xf, axis=-1, keepdims=True)
    var = jnp.mean(jnp.square(xf - mu), axis=-1, keepdims=True)
    y = (xf - mu) * lax.rsqrt(var + EPS)
    return (y * g.astype(jnp.float32) + b.astype(jnp.float32)).astype(x.dtype)


def causal_dwconv(x, w, b):
    width = w.shape[0]
    ch = x.shape[-1]
    y = lax.conv_general_dilated(
        x, w[:, None, :].astype(x.dtype), window_strides=(1,),
        padding=[(width - 1, 0)], dimension_numbers=('NWC', 'WIO', 'NWC'),
        feature_group_count=ch)
    return y + b.astype(x.dtype)


def conv_module(u, conv_w, conv_b, ln_g, ln_b):
    a, gt = jnp.split(u, 2, axis=-1)
    h = a * jax.nn.sigmoid(gt)
    h = causal_dwconv(h, conv_w, conv_b)
    return jax.nn.silu(layernorm(h, ln_g, ln_b))


def chunk_attention(q, k, v, rel_bias):
    b, s, h, hd = q.shape
    nc = s // CHUNK
    qc = q.reshape(b, nc, CHUNK, h, hd)
    pad = ((0, 0), (LEFT_CHUNKS, 0), (0, 0), (0, 0), (0, 0))
    kp = jnp.pad(k.reshape(b, nc, CHUNK, h, hd), pad)
    vp = jnp.pad(v.reshape(b, nc, CHUNK, h, hd), pad)
    band_idx = jnp.arange(nc)[:, None] + jnp.arange(BAND)[None, :]
    kb = kp[:, band_idx].reshape(b, nc, BAND * CHUNK, h, hd)
    vb = vp[:, band_idx].reshape(b, nc, BAND * CHUNK, h, hd)
    scores = jnp.einsum('bcqhd,bckhd->bhcqk', qc, kb).astype(jnp.float32) * (hd ** -0.5)
    qi = jnp.arange(CHUNK)[:, None]
    km = jnp.arange(BAND * CHUNK)[None, :]
    dist = LEFT_CHUNKS * CHUNK + qi - km
    rel_idx = jnp.clip(dist, -MAX_REL, MAX_REL) + MAX_REL
    bias = rel_bias.astype(jnp.float32)[:, rel_idx]
    valid = jnp.repeat(band_idx >= LEFT_CHUNKS, CHUNK, axis=1)
    scores = scores + bias[None, :, None, :, :]
    scores = jnp.where(valid[None, None, :, None, :], scores, NEG_INF)
    p = jax.nn.softmax(scores, axis=-1).astype(v.dtype)
    out = jnp.einsum('bhcqk,bckhd->bcqhd', p, vb)
    return out.reshape(b, s, h * hd)


def memory_attention(q, km, vm):
    b, s, h, hd = q.shape
    scores = jnp.einsum('bshd,bmhd->bhsm', q, km).astype(jnp.float32) * (hd ** -0.5)
    p = jax.nn.softmax(scores, axis=-1).astype(vm.dtype)
    out = jnp.einsum('bhsm,bmhd->bshd', p, vm)
    return out.reshape(b, s, h * hd)


def _fwd_setup_inputs(seed: int = 0) -> dict:
    key = jax.random.key(seed)
    ks = jax.random.split(key, 24)

    def nrm(k, shape, scale):
        return jax.random.normal(k, shape, jnp.float32) * scale

    def gain(k, shape):
        return 1.0 + 0.05 * jax.random.normal(k, shape, jnp.float32)

    L = DEPTH
    return {
        'x': nrm(ks[0], (BATCH, SEQ, D_MODEL), 1.0),
        'mem': nrm(ks[1], (BATCH, N_MEM, D_MODEL), 1.0),
        'mix_norm_g': gain(ks[2], (L, D_MODEL)),
        'mem_norm_g': gain(ks[3], (L, D_MODEL)),
        'w_in': nrm(ks[4], (L, D_MODEL, D_IN), D_MODEL ** -0.5),
        'gate_b': nrm(ks[5], (L, N_BRANCH * D_MODEL), 0.1),
        'conv_w': nrm(ks[6], (L, CONV_WIDTH, D_CONV), CONV_WIDTH ** -0.5),
        'conv_b': nrm(ks[7], (L, D_CONV), 0.02),
        'conv_ln_g': gain(ks[8], (L, D_CONV)),
        'conv_ln_b': nrm(ks[9], (L, D_CONV), 0.02),
        'w_conv_out': nrm(ks[10], (L, D_CONV, D_MODEL), D_CONV ** -0.5),
        'rel_bias': nrm(ks[11], (L, N_ATTN_HEADS, 2 * MAX_REL + 1), 0.1),
        'w_attn_out': nrm(ks[12], (L, D_ATTN, D_MODEL), D_ATTN ** -0.5),
        'w_mem_kv': nrm(ks[13], (L, D_MODEL, 2 * D_MEM), D_MODEL ** -0.5),
        'w_mem_out': nrm(ks[14], (L, D_MEM, D_MODEL), D_MEM ** -0.5),
        'w_o': nrm(ks[15], (L, D_MODEL, D_MODEL), D_MODEL ** -0.5),
        'ffn_norm_g': gain(ks[16], (L, D_MODEL)),
        'w_up': nrm(ks[17], (L, D_MODEL, 2 * D_FF), D_MODEL ** -0.5),
        'ffn_conv_w': nrm(ks[18], (L, FFN_CONV_WIDTH, 2 * D_FF), FFN_CONV_WIDTH ** -0.5),
        'ffn_conv_b': nrm(ks[19], (L, 2 * D_FF), 0.02),
        'w_down': nrm(ks[20], (L, D_FF, D_MODEL), D_FF ** -0.5),
        'final_norm_g': gain(ks[21], (D_MODEL,)),
    }


def _fwd_reference(x, mem, mix_norm_g, mem_norm_g, w_in, gate_b, conv_w, conv_b, conv_ln_g, conv_ln_b,
              w_conv_out, rel_bias, w_attn_out, w_mem_kv, w_mem_out, w_o, ffn_norm_g, w_up,
              ffn_conv_w, ffn_conv_b, w_down, final_norm_g):
    b, s, d = x.shape
    h = x
    for l in range(DEPTH):
        xn = rmsnorm(h, mix_norm_g[l])
        proj = xn @ w_in[l]
        u_conv, qkv, q_mem, gates = jnp.split(proj, SPLITS, axis=-1)
        y_conv = conv_module(u_conv, conv_w[l], conv_b[l], conv_ln_g[l], conv_ln_b[l]) @ w_conv_out[l]
        q, k, v = jnp.split(qkv.reshape(b, s, 3, N_ATTN_HEADS, ATTN_HEAD_DIM), 3, axis=2)
        y_attn = chunk_attention(q[:, :, 0], k[:, :, 0], v[:, :, 0], rel_bias[l]) @ w_attn_out[l]
        mn = rmsnorm(mem, mem_norm_g[l])
        kv = (mn @ w_mem_kv[l]).reshape(b, mem.shape[1], 2, N_MEM_HEADS, MEM_HEAD_DIM)
        qm = q_mem.reshape(b, s, N_MEM_HEADS, MEM_HEAD_DIM)
        y_mem = memory_attention(qm, kv[:, :, 0], kv[:, :, 1]) @ w_mem_out[l]
        g = jax.nn.sigmoid(gates + gate_b[l]).reshape(b, s, N_BRANCH, d)
        merged = g[:, :, 0] * y_conv + g[:, :, 1] * y_attn + g[:, :, 2] * y_mem
        h = h + merged @ w_o[l]
        hn = rmsnorm(h, ffn_norm_g[l])
        up = causal_dwconv(hn @ w_up[l], ffn_conv_w[l], ffn_conv_b[l])
        val, gt = jnp.split(up, 2, axis=-1)
        h = h + (jax.nn.silu(gt) * val) @ w_down[l]
    return rmsnorm(h, final_norm_g)


import jax as _jax
import jax.numpy as _jnp

TWIN_FORMAT = 'train_step'
FWD_PARAMS = ['x', 'mem', 'mix_norm_g', 'mem_norm_g', 'w_in', 'gate_b', 'conv_w', 'conv_b', 'conv_ln_g', 'conv_ln_b', 'w_conv_out', 'rel_bias', 'w_attn_out', 'w_mem_kv', 'w_mem_out', 'w_o', 'ffn_norm_g', 'w_up', 'ffn_conv_w', 'ffn_conv_b', 'w_down', 'final_norm_g']
TWIN_WEIGHTS = ['mix_norm_g', 'mem_norm_g', 'w_in', 'gate_b', 'conv_w', 'conv_b', 'conv_ln_g', 'conv_ln_b', 'w_conv_out', 'rel_bias', 'w_attn_out', 'w_mem_kv', 'w_mem_out', 'w_o', 'ffn_norm_g', 'w_up', 'ffn_conv_w', 'ffn_conv_b', 'w_down', 'final_norm_g']
TWIN_DIFF_INPUT = 'x'
TWIN_INPUTS = ['x', 'mem', 'mix_norm_g', 'mem_norm_g', 'w_in', 'gate_b', 'conv_w', 'conv_b', 'conv_ln_g', 'conv_ln_b', 'w_conv_out', 'rel_bias', 'w_attn_out', 'w_mem_kv', 'w_mem_out', 'w_o', 'ffn_norm_g', 'w_up', 'ffn_conv_w', 'ffn_conv_b', 'w_down', 'final_norm_g', 'loss_target', 'm_mix_norm_g', 'm_mem_norm_g', 'm_w_in', 'm_gate_b', 'm_conv_w', 'm_conv_b', 'm_conv_ln_g', 'm_conv_ln_b', 'm_w_conv_out', 'm_rel_bias', 'm_w_attn_out', 'm_w_mem_kv', 'm_w_mem_out', 'm_w_o', 'm_ffn_norm_g', 'm_w_up', 'm_ffn_conv_w', 'm_ffn_conv_b', 'm_w_down', 'm_final_norm_g', 'v_mix_norm_g', 'v_mem_norm_g', 'v_w_in', 'v_gate_b', 'v_conv_w', 'v_conv_b', 'v_conv_ln_g', 'v_conv_ln_b', 'v_w_conv_out', 'v_rel_bias', 'v_w_attn_out', 'v_w_mem_kv', 'v_w_mem_out', 'v_w_o', 'v_ffn_norm_g', 'v_w_up', 'v_ffn_conv_w', 'v_ffn_conv_b', 'v_w_down', 'v_final_norm_g']
TWIN_OUTPUTS = ['loss', 'grad_x', 'grad_mix_norm_g', 'grad_mem_norm_g', 'grad_w_in', 'grad_gate_b', 'grad_conv_w', 'grad_conv_b', 'grad_conv_ln_g', 'grad_conv_ln_b', 'grad_w_conv_out', 'grad_rel_bias', 'grad_w_attn_out', 'grad_w_mem_kv', 'grad_w_mem_out', 'grad_w_o', 'grad_ffn_norm_g', 'grad_w_up', 'grad_ffn_conv_w', 'grad_ffn_conv_b', 'grad_w_down', 'grad_final_norm_g', 'delta_mix_norm_g', 'delta_mem_norm_g', 'delta_w_in', 'delta_gate_b', 'delta_conv_w', 'delta_conv_b', 'delta_conv_ln_g', 'delta_conv_ln_b', 'delta_w_conv_out', 'delta_rel_bias', 'delta_w_attn_out', 'delta_w_mem_kv', 'delta_w_mem_out', 'delta_w_o', 'delta_ffn_norm_g', 'delta_w_up', 'delta_ffn_conv_w', 'delta_ffn_conv_b', 'delta_w_down', 'delta_final_norm_g', 'new_m_mix_norm_g', 'new_m_mem_norm_g', 'new_m_w_in', 'new_m_gate_b', 'new_m_conv_w', 'new_m_conv_b', 'new_m_conv_ln_g', 'new_m_conv_ln_b', 'new_m_w_conv_out', 'new_m_rel_bias', 'new_m_w_attn_out', 'new_m_w_mem_kv', 'new_m_w_mem_out', 'new_m_w_o', 'new_m_ffn_norm_g', 'new_m_w_up', 'new_m_ffn_conv_w', 'new_m_ffn_conv_b', 'new_m_w_down', 'new_m_final_norm_g', 'new_v_mix_norm_g', 'new_v_mem_norm_g', 'new_v_w_in', 'new_v_gate_b', 'new_v_conv_w', 'new_v_conv_b', 'new_v_conv_ln_g', 'new_v_conv_ln_b', 'new_v_w_conv_out', 'new_v_rel_bias', 'new_v_w_attn_out', 'new_v_w_mem_kv', 'new_v_w_mem_out', 'new_v_w_o', 'new_v_ffn_norm_g', 'new_v_w_up', 'new_v_ffn_conv_w', 'new_v_ffn_conv_b', 'new_v_w_down', 'new_v_final_norm_g']
TWIN_LEAF_KINDS = {'loss': 'loss', 'grad_x': 'grad_x', 'grad_mix_norm_g': 'grad_w', 'grad_mem_norm_g': 'grad_w', 'grad_w_in': 'grad_w', 'grad_gate_b': 'grad_w', 'grad_conv_w': 'grad_w', 'grad_conv_b': 'grad_w', 'grad_conv_ln_g': 'grad_w', 'grad_conv_ln_b': 'grad_w', 'grad_w_conv_out': 'grad_w', 'grad_rel_bias': 'grad_w', 'grad_w_attn_out': 'grad_w', 'grad_w_mem_kv': 'grad_w', 'grad_w_mem_out': 'grad_w', 'grad_w_o': 'grad_w', 'grad_ffn_norm_g': 'grad_w', 'grad_w_up': 'grad_w', 'grad_ffn_conv_w': 'grad_w', 'grad_ffn_conv_b': 'grad_w', 'grad_w_down': 'grad_w', 'grad_final_norm_g': 'grad_w', 'delta_mix_norm_g': 'delta_w', 'delta_mem_norm_g': 'delta_w', 'delta_w_in': 'delta_w', 'delta_gate_b': 'delta_w', 'delta_conv_w': 'delta_w', 'delta_conv_b': 'delta_w', 'delta_conv_ln_g': 'delta_w', 'delta_conv_ln_b': 'delta_w', 'delta_w_conv_out': 'delta_w', 'delta_rel_bias': 'delta_w', 'delta_w_attn_out': 'delta_w', 'delta_w_mem_kv': 'delta_w', 'delta_w_mem_out': 'delta_w', 'delta_w_o': 'delta_w', 'delta_ffn_norm_g': 'delta_w', 'delta_w_up': 'delta_w', 'delta_ffn_conv_w': 'delta_w', 'delta_ffn_conv_b': 'delta_w', 'delta_w_down': 'delta_w', 'delta_final_norm_g': 'delta_w', 'new_m_mix_norm_g': 'new_m', 'new_m_mem_norm_g': 'new_m', 'new_m_w_in': 'new_m', 'new_m_gate_b': 'new_m', 'new_m_conv_w': 'new_m', 'new_m_conv_b': 'new_m', 'new_m_conv_ln_g': 'new_m', 'new_m_conv_ln_b': 'new_m', 'new_m_w_conv_out': 'new_m', 'new_m_rel_bias': 'new_m', 'new_m_w_attn_out': 'new_m', 'new_m_w_mem_kv': 'new_m', 'new_m_w_mem_out': 'new_m', 'new_m_w_o': 'new_m', 'new_m_ffn_norm_g': 'new_m', 'new_m_w_up': 'new_m', 'new_m_ffn_conv_w': 'new_m', 'new_m_ffn_conv_b': 'new_m', 'new_m_w_down': 'new_m', 'new_m_final_norm_g': 'new_m', 'new_v_mix_norm_g': 'new_v', 'new_v_mem_norm_g': 'new_v', 'new_v_w_in': 'new_v', 'new_v_gate_b': 'new_v', 'new_v_conv_w': 'new_v', 'new_v_conv_b': 'new_v', 'new_v_conv_ln_g': 'new_v', 'new_v_conv_ln_b': 'new_v', 'new_v_w_conv_out': 'new_v', 'new_v_rel_bias': 'new_v', 'new_v_w_attn_out': 'new_v', 'new_v_w_mem_kv': 'new_v', 'new_v_w_mem_out': 'new_v', 'new_v_w_o': 'new_v', 'new_v_ffn_norm_g': 'new_v', 'new_v_w_up': 'new_v', 'new_v_ffn_conv_w': 'new_v', 'new_v_ffn_conv_b': 'new_v', 'new_v_w_down': 'new_v', 'new_v_final_norm_g': 'new_v'}


def _forward(args):
    return _fwd_reference(*[args[k] for k in FWD_PARAMS])


def _output_shape():
    def fwd():
        inp = _fwd_setup_inputs(0)
        return _fwd_reference(*[inp[k] for k in FWD_PARAMS])
    out = _jax.eval_shape(fwd)
    return out.shape, out.dtype

N_MICROBATCH = 1
ADAM_LR = 0.001
ADAM_B1 = 0.9
ADAM_B2 = 0.999
ADAM_EPS = 1e-08
ADAM_WD = 0.01
ADAM_STEP = 10
PER_EXAMPLE_BATCH_AXIS = {'x': 0, 'mem': 0, 'loss_target': 0}
SHARED_INPUTS = []
_WEIGHT_DTYPES = {'mix_norm_g': _jnp.float32, 'mem_norm_g': _jnp.float32, 'w_in': _jnp.float32, 'gate_b': _jnp.float32, 'conv_w': _jnp.float32, 'conv_b': _jnp.float32, 'conv_ln_g': _jnp.float32, 'conv_ln_b': _jnp.float32, 'w_conv_out': _jnp.float32, 'rel_bias': _jnp.float32, 'w_attn_out': _jnp.float32, 'w_mem_kv': _jnp.float32, 'w_mem_out': _jnp.float32, 'w_o': _jnp.float32, 'ffn_norm_g': _jnp.float32, 'w_up': _jnp.float32, 'ffn_conv_w': _jnp.float32, 'ffn_conv_b': _jnp.float32, 'w_down': _jnp.float32, 'final_norm_g': _jnp.float32}
MOMENT_SCALE = {'mix_norm_g': 5.206491e-02, 'mem_norm_g': 1.047475e-02, 'w_in': 2.156048e-02, 'gate_b': 1.102722e-02, 'conv_w': 6.082552e-02, 'conv_b': 1.504758e-01, 'conv_ln_g': 8.732314e-02, 'conv_ln_b': 9.347952e-02, 'w_conv_out': 4.646712e-02, 'rel_bias': 4.714078e-03, 'w_attn_out': 1.264461e-02, 'w_mem_kv': 1.035624e-02, 'w_mem_out': 7.607914e-03, 'w_o': 4.789330e-02, 'ffn_norm_g': 9.154076e-02, 'w_up': 3.841640e-02, 'ffn_conv_w': 3.883635e-02, 'ffn_conv_b': 4.372215e-02, 'w_down': 6.305464e-02, 'final_norm_g': 3.205645e+01}


def _to_microbatches(a, axis):
    t = _jnp.moveaxis(a, axis, 0)
    t = t.reshape((N_MICROBATCH, t.shape[0] // N_MICROBATCH) + t.shape[1:])
    return _jnp.moveaxis(t, 1, axis + 1)


def setup_inputs(seed: int = 0) -> dict:
    inp = _fwd_setup_inputs(seed)
    key = _jax.random.fold_in(_jax.random.key(seed), 7919)
    shape, _ = _output_shape()
    out = dict(inp)
    out["loss_target"] = _jax.random.normal(_jax.random.fold_in(key, 0), shape, _jnp.float32)
    for i, name in enumerate(TWIN_WEIGHTS):
        w = inp[name].astype(_jnp.float32)
        if MOMENT_SCALE is None:
            s = _jnp.sqrt(_jnp.mean(_jnp.square(w)) + 1e-30)
        else:
            s = MOMENT_SCALE[name]
        km, kv = _jax.random.split(_jax.random.fold_in(key, i + 1))
        out[name] = w
        out["m_" + name] = s * _jax.random.normal(km, w.shape, _jnp.float32)
        out["v_" + name] = (s * s) * _jax.random.uniform(kv, w.shape, _jnp.float32, 0.5, 1.5)
    if N_MICROBATCH > 1:
        for name, axis in PER_EXAMPLE_BATCH_AXIS.items():
            out[name] = _to_microbatches(out[name], axis)
    return {'x': out['x'], 'mem': out['mem'], 'mix_norm_g': out['mix_norm_g'], 'mem_norm_g': out['mem_norm_g'], 'w_in': out['w_in'], 'gate_b': out['gate_b'], 'conv_w': out['conv_w'], 'conv_b': out['conv_b'], 'conv_ln_g': out['conv_ln_g'], 'conv_ln_b': out['conv_ln_b'], 'w_conv_out': out['w_conv_out'], 'rel_bias': out['rel_bias'], 'w_attn_out': out['w_attn_out'], 'w_mem_kv': out['w_mem_kv'], 'w_mem_out': out['w_mem_out'], 'w_o': out['w_o'], 'ffn_norm_g': out['ffn_norm_g'], 'w_up': out['w_up'], 'ffn_conv_w': out['ffn_conv_w'], 'ffn_conv_b': out['ffn_conv_b'], 'w_down': out['w_down'], 'final_norm_g': out['final_norm_g'], 'loss_target': out['loss_target'], 'm_mix_norm_g': out['m_mix_norm_g'], 'm_mem_norm_g': out['m_mem_norm_g'], 'm_w_in': out['m_w_in'], 'm_gate_b': out['m_gate_b'], 'm_conv_w': out['m_conv_w'], 'm_conv_b': out['m_conv_b'], 'm_conv_ln_g': out['m_conv_ln_g'], 'm_conv_ln_b': out['m_conv_ln_b'], 'm_w_conv_out': out['m_w_conv_out'], 'm_rel_bias': out['m_rel_bias'], 'm_w_attn_out': out['m_w_attn_out'], 'm_w_mem_kv': out['m_w_mem_kv'], 'm_w_mem_out': out['m_w_mem_out'], 'm_w_o': out['m_w_o'], 'm_ffn_norm_g': out['m_ffn_norm_g'], 'm_w_up': out['m_w_up'], 'm_ffn_conv_w': out['m_ffn_conv_w'], 'm_ffn_conv_b': out['m_ffn_conv_b'], 'm_w_down': out['m_w_down'], 'm_final_norm_g': out['m_final_norm_g'], 'v_mix_norm_g': out['v_mix_norm_g'], 'v_mem_norm_g': out['v_mem_norm_g'], 'v_w_in': out['v_w_in'], 'v_gate_b': out['v_gate_b'], 'v_conv_w': out['v_conv_w'], 'v_conv_b': out['v_conv_b'], 'v_conv_ln_g': out['v_conv_ln_g'], 'v_conv_ln_b': out['v_conv_ln_b'], 'v_w_conv_out': out['v_w_conv_out'], 'v_rel_bias': out['v_rel_bias'], 'v_w_attn_out': out['v_w_attn_out'], 'v_w_mem_kv': out['v_w_mem_kv'], 'v_w_mem_out': out['v_w_mem_out'], 'v_w_o': out['v_w_o'], 'v_ffn_norm_g': out['v_ffn_norm_g'], 'v_w_up': out['v_w_up'], 'v_ffn_conv_w': out['v_ffn_conv_w'], 'v_ffn_conv_b': out['v_ffn_conv_b'], 'v_w_down': out['v_w_down'], 'v_final_norm_g': out['v_final_norm_g']}


def _loss(weights, diff, rest, loss_target):
    with _jax.named_scope("forward"):
        args = {**rest, TWIN_DIFF_INPUT: diff, **{k: w.astype(_WEIGHT_DTYPES[k]) for k, w in weights.items()}}
        y = _forward(args)
    with _jax.named_scope("loss_head"):
        err = _jnp.square(y.astype(_jnp.float32) - loss_target)
        return 0.5 * _jnp.sum(_jnp.mean(err, axis=-1)) if err.ndim else 0.5 * err


def _adamw(w, g, m, v):
    m = ADAM_B1 * m + (1.0 - ADAM_B1) * g
    v = ADAM_B2 * v + (1.0 - ADAM_B2) * _jnp.square(g)
    m_hat = m / (1.0 - ADAM_B1 ** ADAM_STEP)
    v_hat = v / (1.0 - ADAM_B2 ** ADAM_STEP)
    delta = -ADAM_LR * (m_hat / (_jnp.sqrt(v_hat) + ADAM_EPS) + ADAM_WD * w)
    return delta, m, v


def reference(x, mem, mix_norm_g, mem_norm_g, w_in, gate_b, conv_w, conv_b, conv_ln_g, conv_ln_b, w_conv_out, rel_bias, w_attn_out, w_mem_kv, w_mem_out, w_o, ffn_norm_g, w_up, ffn_conv_w, ffn_conv_b, w_down, final_norm_g, loss_target, m_mix_norm_g, m_mem_norm_g, m_w_in, m_gate_b, m_conv_w, m_conv_b, m_conv_ln_g, m_conv_ln_b, m_w_conv_out, m_rel_bias, m_w_attn_out, m_w_mem_kv, m_w_mem_out, m_w_o, m_ffn_norm_g, m_w_up, m_ffn_conv_w, m_ffn_conv_b, m_w_down, m_final_norm_g, v_mix_norm_g, v_mem_norm_g, v_w_in, v_gate_b, v_conv_w, v_conv_b, v_conv_ln_g, v_conv_ln_b, v_w_conv_out, v_rel_bias, v_w_attn_out, v_w_mem_kv, v_w_mem_out, v_w_o, v_ffn_norm_g, v_w_up, v_ffn_conv_w, v_ffn_conv_b, v_w_down, v_final_norm_g):
    given = dict(x=x, mem=mem, mix_norm_g=mix_norm_g, mem_norm_g=mem_norm_g, w_in=w_in, gate_b=gate_b, conv_w=conv_w, conv_b=conv_b, conv_ln_g=conv_ln_g, conv_ln_b=conv_ln_b, w_conv_out=w_conv_out, rel_bias=rel_bias, w_attn_out=w_attn_out, w_mem_kv=w_mem_kv, w_mem_out=w_mem_out, w_o=w_o, ffn_norm_g=ffn_norm_g, w_up=w_up, ffn_conv_w=ffn_conv_w, ffn_conv_b=ffn_conv_b, w_down=w_down, final_norm_g=final_norm_g, loss_target=loss_target, m_mix_norm_g=m_mix_norm_g, m_mem_norm_g=m_mem_norm_g, m_w_in=m_w_in, m_gate_b=m_gate_b, m_conv_w=m_conv_w, m_conv_b=m_conv_b, m_conv_ln_g=m_conv_ln_g, m_conv_ln_b=m_conv_ln_b, m_w_conv_out=m_w_conv_out, m_rel_bias=m_rel_bias, m_w_attn_out=m_w_attn_out, m_w_mem_kv=m_w_mem_kv, m_w_mem_out=m_w_mem_out, m_w_o=m_w_o, m_ffn_norm_g=m_ffn_norm_g, m_w_up=m_w_up, m_ffn_conv_w=m_ffn_conv_w, m_ffn_conv_b=m_ffn_conv_b, m_w_down=m_w_down, m_final_norm_g=m_final_norm_g, v_mix_norm_g=v_mix_norm_g, v_mem_norm_g=v_mem_norm_g, v_w_in=v_w_in, v_gate_b=v_gate_b, v_conv_w=v_conv_w, v_conv_b=v_conv_b, v_conv_ln_g=v_conv_ln_g, v_conv_ln_b=v_conv_ln_b, v_w_conv_out=v_w_conv_out, v_rel_bias=v_rel_bias, v_w_attn_out=v_w_attn_out, v_w_mem_kv=v_w_mem_kv, v_w_mem_out=v_w_mem_out, v_w_o=v_w_o, v_ffn_norm_g=v_ffn_norm_g, v_w_up=v_w_up, v_ffn_conv_w=v_ffn_conv_w, v_ffn_conv_b=v_ffn_conv_b, v_w_down=v_w_down, v_final_norm_g=v_final_norm_g)
    weights = {n: given[n] for n in TWIN_WEIGHTS}
    shared = {n: given[n] for n in SHARED_INPUTS}
    per_example = {n: given[n] for n in ['x', 'mem']}
    grad_fn = _jax.value_and_grad(_loss, argnums=(0, 1))

    def one_microbatch(ex, loss_target):
        ex = dict(ex)
        diff = ex.pop(TWIN_DIFF_INPUT)
        return grad_fn(weights, diff, {**shared, **ex}, loss_target)

    if N_MICROBATCH == 1:
        loss, (grad_w, grad_x) = one_microbatch(per_example, given["loss_target"])
    else:
        def body(carry, xs):
            loss_sum, grad_sum = carry
            l_k, (gw_k, gx_k) = one_microbatch(xs[0], xs[1])
            with _jax.named_scope("update"):
                return (loss_sum + l_k, _jax.tree.map(_jnp.add, grad_sum, gw_k)), gx_k

        init = (_jnp.zeros((), _jnp.float32), _jax.tree.map(_jnp.zeros_like, weights))
        (loss, grad_w), grad_x = _jax.lax.scan(body, init, (per_example, given["loss_target"]))
    with _jax.named_scope("update"):
        delta_w, new_m, new_v = {}, {}, {}
        for n in TWIN_WEIGHTS:
            delta_w[n], new_m[n], new_v[n] = _adamw(weights[n], grad_w[n], given["m_" + n], given["v_" + n])
    return (loss, grad_x, *[grad_w[n] for n in TWIN_WEIGHTS], *[delta_w[n] for n in TWIN_WEIGHTS],
            *[new_m[n] for n in TWIN_WEIGHTS], *[new_v[n] for n in TWIN_WEIGHTS])
```

```python
import jax
import jax.numpy as jnp
from jax import lax
from jax.experimental import pallas as pl
from jax.experimental.pallas import tpu as pltpu

F32 = jnp.float32
BF16 = jnp.bfloat16
MESH = pl.DeviceIdType.MESH

CHUNK = 64
LEFT_CHUNKS = 8
BAND = (LEFT_CHUNKS + 1) * CHUNK
BANDP = BAND + CHUNK
MAX_REL = 256
REL_PAD = 640
ATTN_HEAD_DIM = 64
N_MEM_HEADS = 4
QBLK = LEFT_CHUNKS * CHUNK
EPS = 1e-6
NEG_INF = -1e30
LANES = 128
VMEM_LIMIT = 56 * 1024 * 1024

ADAM_LR = 0.001
ADAM_B1 = 0.9
ADAM_B2 = 0.999
ADAM_EPS = 1e-08
ADAM_WD = 0.01
ADAM_STEP = 10


def _pick(dim, target, unit=LANES):
    if dim <= target:
        return dim
    d = (target // unit) * unit
    while d >= unit:
        if dim % d == 0:
            return d
        d -= unit
    raise ValueError(f"no tile for {dim} under {target}")


def _sigmoid(x):
    return 1.0 / (1.0 + jnp.exp(-x))


def _pcall(body, *, name, grid, sc, ins, outs, scratch=(), sem=None):
    arrays = [a for a, _ in ins]
    in_specs = [s for _, s in ins]
    n_in = len(arrays)
    out_shape, out_specs, aliases = [], [], {}
    for k, (o, spec) in enumerate(outs):
        if isinstance(o, jax.ShapeDtypeStruct):
            out_shape.append(o)
        else:
            aliases[1 + len(arrays)] = k
            arrays.append(o)
            in_specs.append(pl.BlockSpec(memory_space=pl.ANY))
            out_shape.append(jax.ShapeDtypeStruct(o.shape, o.dtype))
        out_specs.append(spec)
    n_alias = len(arrays) - n_in

    def wrapped(sc_ref, *refs):
        body(sc_ref, *refs[:n_in], *refs[n_in + n_alias:])

    res = pl.pallas_call(
        wrapped,
        name=name,
        grid_spec=pltpu.PrefetchScalarGridSpec(
            num_scalar_prefetch=1, grid=grid, in_specs=in_specs, out_specs=out_specs, scratch_shapes=list(scratch)),
        out_shape=out_shape,
        input_output_aliases=aliases,
        compiler_params=pltpu.CompilerParams(
            dimension_semantics=sem or ("arbitrary",) * len(grid), vmem_limit_bytes=VMEM_LIMIT),
    )(sc, *arrays)
    return res


def _bs(arr, blk, rc, lead=None):
    nlead = len(arr.shape) - 2

    def imap(*ids):
        sc = ids[-1]
        r, c = rc(*ids)
        return (*(lead(sc) if nlead else ()), r, c)

    return pl.BlockSpec((None,) * nlead + tuple(blk), imap)


def _sds(shape, dtype):
    return jax.ShapeDtypeStruct(tuple(shape), dtype)


L0 = lambda sc: (sc[0],)
L1 = lambda sc: (sc[1],)


def _mm_core(name, grid, sc, a_items, b_item, add_item, out_item, dims, ranges, out_dtype, acc_shape):
    nk = grid[2]
    na = len(a_items)
    has_add = add_item is not None

    def body(sc_ref, *refs):
        a_refs = refs[:na]
        b_ref = refs[na]
        pos = na + 1
        add_ref = refs[pos] if has_add else None
        pos += int(has_add)
        o_ref = refs[pos]
        acc = refs[pos + 1] if nk > 1 else None
        k = pl.program_id(2)

        def contrib(ar):
            return lax.dot_general(ar[...].astype(BF16), b_ref[...].astype(BF16), dims, preferred_element_type=F32)

        def fin(val):
            if has_add:
                val = val + add_ref[...]
            o_ref[...] = val.astype(out_dtype)

        if nk == 1:
            fin(contrib(a_refs[0]))
            return
        for p, (k0, k1) in enumerate(ranges):
            @pl.when((k >= k0) & (k < k1))
            def _(p=p):
                part = contrib(a_refs[p])

                @pl.when(k == 0)
                def _():
                    acc[...] = part

                @pl.when((k > 0) & (k < nk - 1))
                def _():
                    acc[...] += part

                @pl.when(k == nk - 1)
                def _():
                    fin(acc[...] + part)

    ins = list(a_items) + [b_item] + ([add_item] if has_add else [])
    scratch = [pltpu.VMEM(acc_shape, F32)] if nk > 1 else []
    return _pcall(body, name=name, grid=grid, sc=sc, ins=ins, outs=[out_item], scratch=scratch,
                  sem=("parallel", "parallel", "arbitrary"))[0]


def mm_nn(name, sc, a, a_lead, w, M, K, N, out, out_lead=None, out_joff=0, add=None, out_dtype=F32,
          tm=1024, tn=1408, tk=2048):
    tm, tn, tk = _pick(M, tm, 8), _pick(N, tn), _pick(K, tk)
    nk = K // tk
    a_item = (a, _bs(a, (tm, tk), lambda i, j, k, sc: (i, k), a_lead))
    b_item = (w, _bs(w, (tk, tn), lambda i, j, k, sc: (k, j), L0))
    add_item = None if add is None else (add, _bs(add, (tm, tn), lambda i, j, k, sc: (i, j)))
    out_spec = _bs(out, (tm, tn), lambda i, j, k, sc: (i, j + out_joff // tn), out_lead)
    return _mm_core(name, (M // tm, N // tn, nk), sc, [a_item], b_item, add_item, (out, out_spec),
                    (((1,), (0,)), ((), ())), [(0, nk)], out_dtype, (tm, tn))


def mm_nt(name, sc, pieces, w, M, N, out, w_koff=0, out_dtype=F32, tm=1024, tn=1408, tk=2048):
    tm, tn = _pick(M, tm, 8), _pick(N, tn)
    widths = [a.shape[-1] for a, _ in pieces]
    tk = _pick(widths[0], tk)
    offs = [w_koff]
    for wd in widths:
        offs.append(offs[-1] + wd)
    while any(x % tk for x in offs):
        tk = _pick(widths[0], tk - LANES)
    ranges, a_items, k0 = [], [], 0
    for (a, lead), wd in zip(pieces, widths):
        n = wd // tk
        ranges.append((k0, k0 + n))
        a_items.append((a, _bs(a, (tm, tk), lambda i, j, k, sc, k0=k0, n=n: (i, jnp.clip(k - k0, 0, n - 1)), lead)))
        k0 += n
    nk = k0
    b_item = (w, _bs(w, (tn, tk), lambda i, j, k, sc: (j, k + w_koff // tk), L0))
    out_spec = _bs(out, (tm, tn), lambda i, j, k, sc: (i, j))
    return _mm_core(name, (M // tm, N // tn, nk), sc, a_items, b_item, None, (out, out_spec),
                    (((1,), (1,)), ((), ())), ranges, out_dtype, (tm, tn))


def mm_tn(name, sc, a, a_lead, b, b_lead, S, K, N, out, out_lead, out_joff=0, tm=1408, tn=1408, tk=2048):
    tm, tn, tk = _pick(K, tm), _pick(N, tn), _pick(S, tk, 8)
    while out_joff % tn:
        tn = _pick(N, tn - LANES)
    nk = S // tk
    a_item = (a, _bs(a, (tk, tm), lambda i, j, k, sc: (k, i), a_lead))
    b_item = (b, _bs(b, (tk, tn), lambda i, j, k, sc: (k, j), b_lead))
    out_spec = _bs(out, (tm, tn), lambda i, j, k, sc: (i, j + out_joff // tn), out_lead)
    return _mm_core(name, (K // tm, N // tn, nk), sc, [a_item], b_item, None, (out, out_spec),
                    (((0,), (0,)), ((), ())), [(0, nk)], F32, (tm, tn))


def rms_fwd(name, sc, h, g3, xn_buf, hs_buf):
    S, D = h.shape
    tr = _pick(S, 256, 8)

    def body(sc_ref, h_ref, g_ref, xn_ref, hs_ref):
        x = h_ref[...]
        r = lax.rsqrt(jnp.mean(x * x, axis=-1, keepdims=True) + EPS)
        xn_ref[...] = (x * r * g_ref[...]).astype(BF16)
        hs_ref[...] = x

    return _pcall(body, name=name, grid=(S // tr,), sc=sc,
                  ins=[(h, _bs(h, (tr, D), lambda i, sc: (i, 0))), (g3, _bs(g3, (1, D), lambda i, sc: (0, 0), L0))],
                  outs=[(xn_buf, _bs(xn_buf, (tr, D), lambda i, sc: (i, 0), L1)),
                        (hs_buf, _bs(hs_buf, (tr, D), lambda i, sc: (i, 0), L1))])


def rms_bwd(name, sc, hs_buf, g3, dy, dres, dg_buf):
    S, D = dy.shape
    tr = _pick(S, 256, 8)

    def body(sc_ref, x_ref, g_ref, dy_ref, dres_ref, dx_ref, dxb_ref, dg_ref):
        i = pl.program_id(0)
        x = x_ref[...]
        r = lax.rsqrt(jnp.mean(x * x, axis=-1, keepdims=True) + EPS)
        xh = x * r
        dyv = dy_ref[...]
        dxh = dyv * g_ref[...]
        dx = r * (dxh - xh * jnp.mean(dxh * xh, axis=-1, keepdims=True)) + dres_ref[...]
        dx_ref[...] = dx
        dxb_ref[...] = dx.astype(BF16)
        part = jnp.sum(dyv * xh, axis=0, keepdims=True)

        @pl.when(i == 0)
        def _():
            dg_ref[...] = part

        @pl.when(i > 0)
        def _():
            dg_ref[...] += part

    return _pcall(body, name=name, grid=(S // tr,), sc=sc,
                  ins=[(hs_buf, _bs(hs_buf, (tr, D), lambda i, sc: (i, 0), L1)),
                       (g3, _bs(g3, (1, D), lambda i, sc: (0, 0), L0)),
                       (dy, _bs(dy, (tr, D), lambda i, sc: (i, 0))),
                       (dres, _bs(dres, (tr, D), lambda i, sc: (i, 0)))],
                  outs=[(_sds((S, D), F32), pl.BlockSpec((tr, D), lambda i, sc: (i, 0))),
                        (_sds((S, D), BF16), pl.BlockSpec((tr, D), lambda i, sc: (i, 0))),
                        (dg_buf, _bs(dg_buf, (1, D), lambda i, sc: (0, 0), L0))])


CONV_HALO = 32
CONV_RB = 32


def conv_fwd(name, sc, proj, cw, cb, lg, lb, hc_buf, cact_buf, C, W):
    S = proj.shape[1]
    T = _pick(S, 256, CONV_HALO)
    nh = T // CONV_HALO
    off = CONV_HALO - (W - 1)

    def body(sc_ref, ua_ref, ug_ref, pa_ref, pg_ref, w_ref, b_ref, lg_ref, lb_ref, hc_ref, ca_ref, buf):
        i = pl.program_id(0)
        halo = pa_ref[...] * _sigmoid(pg_ref[...])
        buf[0:CONV_HALO, :] = jnp.where(i > 0, halo, 0.0)
        buf[CONV_HALO:CONV_HALO + T, :] = ua_ref[...] * _sigmoid(ug_ref[...])
        for r in range(0, T, CONV_RB):
            acc = jnp.zeros((CONV_RB, C), F32) + b_ref[...]
            for k in range(W):
                acc = acc + w_ref[k:k + 1, :] * buf[r + off + k:r + off + k + CONV_RB, :]
            hc_ref[r:r + CONV_RB, :] = acc
        hc = hc_ref[...]
        mu = jnp.mean(hc, axis=-1, keepdims=True)
        d = hc - mu
        rstd = lax.rsqrt(jnp.mean(d * d, axis=-1, keepdims=True) + EPS)
        y = d * rstd * lg_ref[...] + lb_ref[...]
        ca_ref[...] = (y * _sigmoid(y)).astype(BF16)

    vec = lambda a: (a, _bs(a, (1, C), lambda i, sc: (0, 0), L0))
    return _pcall(body, name=name, grid=(S // T,), sc=sc,
                  ins=[(proj, _bs(proj, (T, C), lambda i, sc: (i, 0), L0)),
                       (proj, _bs(proj, (T, C), lambda i, sc: (i, 1), L0)),
                       (proj, _bs(proj, (CONV_HALO, C), lambda i, sc: (jnp.maximum(i * nh - 1, 0), 0), L0)),
                       (proj, _bs(proj, (CONV_HALO, C), lambda i, sc: (jnp.maximum(i * nh - 1, 0), 1), L0)),
                       (cw, _bs(cw, (CONV_HALO, C), lambda i, sc: (0, 0), L0)), vec(cb), vec(lg), vec(lb)],
                  outs=[(hc_buf, _bs(hc_buf, (T, C), lambda i, sc: (i, 0), L0)),
                        (cact_buf, _bs(cact_buf, (T, C), lambda i, sc: (i, 0), L0))],
                  scratch=[pltpu.VMEM((T + CONV_HALO, C), F32)])


def conv_bwd1(name, sc, hc_buf, dcact, lg, lb, dlg_buf, dlb_buf, dcb_buf):
    S, C = dcact.shape
    tr = _pick(S, 256, 8)

    def body(sc_ref, hc_ref, dc_ref, lg_ref, lb_ref, dhc_ref, dlg_ref, dlb_ref, dcb_ref):
        i = pl.program_id(0)
        hc = hc_ref[...]
        mu = jnp.mean(hc, axis=-1, keepdims=True)
        d = hc - mu
        rstd = lax.rsqrt(jnp.mean(d * d, axis=-1, keepdims=True) + EPS)
        yh = d * rstd
        y = yh * lg_ref[...] + lb_ref[...]
        sg = _sigmoid(y)
        dy = dc_ref[...] * (sg * (1.0 + y * (1.0 - sg)))
        dyh = dy * lg_ref[...]
        dhc = rstd * (dyh - jnp.mean(dyh, axis=-1, keepdims=True) - yh * jnp.mean(dyh * yh, axis=-1, keepdims=True))
        dhc_ref[...] = dhc
        parts = (jnp.sum(dy * yh, axis=0, keepdims=True), jnp.sum(dy, axis=0, keepdims=True),
                 jnp.sum(dhc, axis=0, keepdims=True))

        @pl.when(i == 0)
        def _():
            dlg_ref[...], dlb_ref[...], dcb_ref[...] = parts

        @pl.when(i > 0)
        def _():
            dlg_ref[...] += parts[0]
            dlb_ref[...] += parts[1]
            dcb_ref[...] += parts[2]

    vec = lambda a: (a, _bs(a, (1, C), lambda i, sc: (0, 0), L0))
    return _pcall(body, name=name, grid=(S // tr,), sc=sc,
                  ins=[(hc_buf, _bs(hc_buf, (tr, C), lambda i, sc: (i, 0), L0)),
                       (dcact, _bs(dcact, (tr, C), lambda i, sc: (i, 0))), vec(lg), vec(lb)],
                  outs=[(_sds((S, C), F32), pl.BlockSpec((tr, C), lambda i, sc: (i, 0))),
                        vec(dlg_buf), vec(dlb_buf), vec(dcb_buf)])


def conv_bwd2(name, sc, proj, dhc, cw, dcw_buf, C, W):
    S = proj.shape[1]
    T = _pick(S, 256, CONV_HALO)
    nh = T // CONV_HALO
    nt = S // T
    off = CONV_HALO - (W - 1)

    def body(sc_ref, ua_ref, ug_ref, pa_ref, pg_ref, d_ref, dn_ref, w_ref, o_ref, dw_ref, buf, dbuf, dhg):
        i = pl.program_id(0)
        halo = pa_ref[...] * _sigmoid(pg_ref[...])
        buf[0:CONV_HALO, :] = jnp.where(i > 0, halo, 0.0)
        sg = _sigmoid(ug_ref[...])
        ua = ua_ref[...]
        buf[CONV_HALO:CONV_HALO + T, :] = ua * sg
        dbuf[0:T, :] = d_ref[...]
        dbuf[T:T + CONV_HALO, :] = jnp.where(i < nt - 1, dn_ref[...], 0.0)

        @pl.when(i == 0)
        def _():
            dw_ref[...] = jnp.zeros_like(dw_ref)

        for r in range(0, T, CONV_RB):
            acc = jnp.zeros((CONV_RB, C), F32)
            for k in range(W):
                acc = acc + w_ref[k:k + 1, :] * dbuf[r + (W - 1) - k:r + (W - 1) - k + CONV_RB, :]
            dhg[r:r + CONV_RB, :] = acc
        for k in range(W):
            dw_ref[k:k + 1, :] += jnp.sum(dbuf[0:T, :] * buf[off + k:off + k + T, :], axis=0, keepdims=True)
        dh = dhg[...]
        o_ref[:, 0:C] = (dh * sg).astype(BF16)
        o_ref[:, C:2 * C] = (dh * ua * sg * (1.0 - sg)).astype(BF16)

    return _pcall(body, name=name, grid=(nt,), sc=sc,
                  ins=[(proj, _bs(proj, (T, C), lambda i, sc: (i, 0), L0)),
                       (proj, _bs(proj, (T, C), lambda i, sc: (i, 1), L0)),
                       (proj, _bs(proj, (CONV_HALO, C), lambda i, sc: (jnp.maximum(i * nh - 1, 0), 0), L0)),
                       (proj, _bs(proj, (CONV_HALO, C), lambda i, sc: (jnp.maximum(i * nh - 1, 0), 1), L0)),
                       (dhc, _bs(dhc, (T, C), lambda i, sc: (i, 0))),
                       (dhc, _bs(dhc, (CONV_HALO, C), lambda i, sc: (jnp.minimum((i + 1) * nh, S // CONV_HALO - 1), 0))),
                       (cw, _bs(cw, (CONV_HALO, C), lambda i, sc: (0, 0), L0))],
                  outs=[(_sds((S, 2 * C), BF16), pl.BlockSpec((T, 2 * C), lambda i, sc: (i, 0))),
                        (dcw_buf, _bs(dcw_buf, (CONV_HALO, C), lambda i, sc: (0, 0), L0))],
                  scratch=[pltpu.VMEM((T + CONV_HALO, C), F32), pltpu.VMEM((T + CONV_HALO, C), F32),
                           pltpu.VMEM((T, C), F32)])


def gate_fwd(name, sc, proj, gb3, y_buf, merged_buf, D):
    S = proj.shape[1]
    tr, tc = _pick(S, 256, 8), _pick(D, 1024)
    nc = D // tc

    def body(sc_ref, g0, g1, g2, b0, b1, b2, y0, y1, y2, o_ref):
        acc = None
        for g, b, y in ((g0, b0, y0), (g1, b1, y1), (g2, b2, y2)):
            t = _sigmoid(g[...] + b[...]) * y[...]
            acc = t if acc is None else acc + t
        o_ref[...] = acc.astype(BF16)

    ins = [(proj, _bs(proj, (tr, tc), lambda i, j, sc, b=b: (i, (3 + b) * nc + j), L0)) for b in range(3)]
    ins += [(gb3, _bs(gb3, (1, tc), lambda i, j, sc, b=b: (0, b * nc + j), L0)) for b in range(3)]
    ins += [(y_buf, _bs(y_buf, (tr, tc), lambda i, j, sc: (i, j), lambda sc, b=b: (sc[0], b))) for b in range(3)]
    return _pcall(body, name=name, grid=(S // tr, nc), sc=sc, ins=ins,
                  outs=[(merged_buf, _bs(merged_buf, (tr, tc), lambda i, j, sc: (i, j), L0))])


def gate_bwd(name, sc, proj, gb3, y_buf, dmerged, dgb_buf, D):
    S = proj.shape[1]
    tr, tc = _pick(S, 256, 8), _pick(D, 1024)
    nc = D // tc

    def body(sc_ref, g_ref, b_ref, y_ref, dm_ref, dy_ref, dg_ref, dgb_ref):
        i = pl.program_id(2)
        g = _sigmoid(g_ref[...] + b_ref[...])
        dm = dm_ref[...]
        dy_ref[...] = (dm * g).astype(BF16)
        dgt = dm * y_ref[...] * g * (1.0 - g)
        dg_ref[...] = dgt.astype(BF16)
        part = jnp.sum(dgt, axis=0, keepdims=True)

        @pl.when(i == 0)
        def _():
            dgb_ref[...] = part

        @pl.when(i > 0)
        def _():
            dgb_ref[...] += part

    dy_sds = _sds((3, S, D), BF16)
    return _pcall(body, name=name, grid=(3, nc, S // tr), sc=sc,
                  ins=[(proj, _bs(proj, (tr, tc), lambda b, j, i, sc: (i, (3 + b) * nc + j), L0)),
                       (gb3, _bs(gb3, (1, tc), lambda b, j, i, sc: (0, b * nc + j), L0)),
                       (y_buf, pl.BlockSpec((None, None, tr, tc), lambda b, j, i, sc: (sc[0], b, i, j))),
                       (dmerged, _bs(dmerged, (tr, tc), lambda b, j, i, sc: (i, j)))],
                  outs=[(dy_sds, pl.BlockSpec((None, tr, tc), lambda b, j, i, sc: (b, i, j))),
                        (_sds((S, 3 * D), BF16), pl.BlockSpec((tr, tc), lambda b, j, i, sc: (i, b * nc + j))),
                        (dgb_buf, _bs(dgb_buf, (1, tc), lambda b, j, i, sc: (0, b * nc + j), L0))])


FFN_HALO = 8


def ffn_fwd(name, sc, up, fw, fb3, act_buf, Fh, W):
    S = up.shape[1]
    T, tc = _pick(S, 256, 8), _pick(Fh, 512)
    nf = Fh // tc
    nh = T // FFN_HALO
    off = FFN_HALO - (W - 1)

    def body(sc_ref, v_ref, g_ref, pv_ref, pg_ref, wv_ref, wg_ref, bv_ref, bg_ref, o_ref, bufv, bufg):
        i = pl.program_id(1)
        res = []
        for m_ref, p_ref, w_ref, b_ref, buf in ((v_ref, pv_ref, wv_ref, bv_ref, bufv), (g_ref, pg_ref, wg_ref, bg_ref, bufg)):
            buf[0:FFN_HALO, :] = jnp.where(i > 0, p_ref[...], 0.0)
            buf[FFN_HALO:FFN_HALO + T, :] = m_ref[...]
            acc = jnp.zeros((T, tc), F32) + b_ref[...]
            for k in range(W):
                acc = acc + w_ref[k:k + 1, :] * buf[off + k:off + k + T, :]
            res.append(acc)
        val, gt = res
        o_ref[...] = (gt * _sigmoid(gt) * val).astype(BF16)

    prev = lambda i: jnp.maximum(i * nh - 1, 0)
    return _pcall(body, name=name, grid=(nf, S // T), sc=sc,
                  ins=[(up, _bs(up, (T, tc), lambda j, i, sc: (i, j), L0)),
                       (up, _bs(up, (T, tc), lambda j, i, sc: (i, j + nf), L0)),
                       (up, _bs(up, (FFN_HALO, tc), lambda j, i, sc: (prev(i), j), L0)),
                       (up, _bs(up, (FFN_HALO, tc), lambda j, i, sc: (prev(i), j + nf), L0)),
                       (fw, _bs(fw, (FFN_HALO, tc), lambda j, i, sc: (0, j), L0)),
                       (fw, _bs(fw, (FFN_HALO, tc), lambda j, i, sc: (0, j + nf), L0)),
                       (fb3, _bs(fb3, (1, tc), lambda j, i, sc: (0, j), L0)),
                       (fb3, _bs(fb3, (1, tc), lambda j, i, sc: (0, j + nf), L0))],
                  outs=[(act_buf, _bs(act_buf, (T, tc), lambda j, i, sc: (i, j), L0))],
                  scratch=[pltpu.VMEM((T + FFN_HALO, tc), F32), pltpu.VMEM((T + FFN_HALO, tc), F32)])


def ffn_bwd(name, sc, up, dact, fw, fb3, dfw_v_buf, dfw_g_buf, dfb_v_buf, dfb_g_buf, Fh, W):
    S = up.shape[1]
    T, tc = _pick(S, 256, 8), _pick(Fh, 512)
    nf = Fh // tc
    nh = T // FFN_HALO
    nt = S // T
    off = FFN_HALO - (W - 1)
    TE = T + FFN_HALO

    def body(sc_ref, v_ref, g_ref, pv_ref, pg_ref, nv_ref, ng_ref, d_ref, dn_ref, wv_ref, wg_ref, bv_ref, bg_ref,
             ov_ref, og_ref, dwv_ref, dwg_ref, dbv_ref, dbg_ref, bufv, bufg, dv, dg):
        i = pl.program_id(1)
        ups = []
        for m_ref, p_ref, n_ref, w_ref, b_ref, buf in ((v_ref, pv_ref, nv_ref, wv_ref, bv_ref, bufv),
                                                      (g_ref, pg_ref, ng_ref, wg_ref, bg_ref, bufg)):
            buf[0:FFN_HALO, :] = jnp.where(i > 0, p_ref[...], 0.0)
            buf[FFN_HALO:FFN_HALO + T, :] = m_ref[...]
            buf[FFN_HALO + T:FFN_HALO + TE, :] = n_ref[...]
            acc = jnp.zeros((TE, tc), F32) + b_ref[...]
            for k in range(W):
                acc = acc + w_ref[k:k + 1, :] * buf[off + k:off + k + TE, :]
            ups.append(acc)
        val, gt = ups
        dn = jnp.where(i < nt - 1, dn_ref[...], 0.0)
        d = jnp.concatenate([d_ref[...], dn], axis=0)
        sg = _sigmoid(gt)
        dv[0:TE, :] = d * gt * sg
        dg[0:TE, :] = d * val * (sg * (1.0 + gt * (1.0 - sg)))

        @pl.when(i == 0)
        def _():
            for r in (dwv_ref, dwg_ref, dbv_ref, dbg_ref):
                r[...] = jnp.zeros_like(r)

        for dsrc, w_ref, buf, o_ref, dw_ref, db_ref in ((dv, wv_ref, bufv, ov_ref, dwv_ref, dbv_ref),
                                                      (dg, wg_ref, bufg, og_ref, dwg_ref, dbg_ref)):
            acc = jnp.zeros((T, tc), F32)
            for k in range(W):
                acc = acc + w_ref[k:k + 1, :] * dsrc[(W - 1) - k:(W - 1) - k + T, :]
            o_ref[...] = acc.astype(BF16)
            dm = dsrc[0:T, :]
            for k in range(W):
                dw_ref[k:k + 1, :] += jnp.sum(dm * buf[off + k:off + k + T, :], axis=0, keepdims=True)
            db_ref[...] += jnp.sum(dm, axis=0, keepdims=True)

    prev = lambda i: jnp.maximum(i * nh - 1, 0)
    nxt = lambda i: jnp.minimum((i + 1) * nh, S // FFN_HALO - 1)
    o_sds = _sds((S, Fh), BF16)
    return _pcall(body, name=name, grid=(nf, nt), sc=sc,
                  ins=[(up, _bs(up, (T, tc), lambda j, i, sc: (i, j), L0)),
                       (up, _bs(up, (T, tc), lambda j, i, sc: (i, j + nf), L0)),
                       (up, _bs(up, (FFN_HALO, tc), lambda j, i, sc: (prev(i), j), L0)),
                       (up, _bs(up, (FFN_HALO, tc), lambda j, i, sc: (prev(i), j + nf), L0)),
                       (up, _bs(up, (FFN_HALO, tc), lambda j, i, sc: (nxt(i), j), L0)),
                       (up, _bs(up, (FFN_HALO, tc), lambda j, i, sc: (nxt(i), j + nf), L0)),
                       (dact, _bs(dact, (T, tc), lambda j, i, sc: (i, j))),
                       (dact, _bs(dact, (FFN_HALO, tc), lambda j, i, sc: (nxt(i), j))),
                       (fw, _bs(fw, (FFN_HALO, tc), lambda j, i, sc: (0, j), L0)),
                       (fw, _bs(fw, (FFN_HALO, tc), lambda j, i, sc: (0, j + nf), L0)),
                       (fb3, _bs(fb3, (1, tc), lambda j, i, sc: (0, j), L0)),
                       (fb3, _bs(fb3, (1, tc), lambda j, i, sc: (0, j + nf), L0))],
                  outs=[(o_sds, pl.BlockSpec((T, tc), lambda j, i, sc: (i, j))),
                        (o_sds, pl.BlockSpec((T, tc), lambda j, i, sc: (i, j))),
                        (dfw_v_buf, _bs(dfw_v_buf, (FFN_HALO, tc), lambda j, i, sc: (0, j), L0)),
                        (dfw_g_buf, _bs(dfw_g_buf, (FFN_HALO, tc), lambda j, i, sc: (0, j), L0)),
                        (dfb_v_buf, _bs(dfb_v_buf, (1, tc), lambda j, i, sc: (0, j), L0)),
                        (dfb_g_buf, _bs(dfb_g_buf, (1, tc), lambda j, i, sc: (0, j), L0))],
                  scratch=[pltpu.VMEM((T + 2 * FFN_HALO, tc), F32), pltpu.VMEM((T + 2 * FFN_HALO, tc), F32),
                           pltpu.VMEM((T + 2 * FFN_HALO, tc), F32), pltpu.VMEM((T + 2 * FFN_HALO, tc), F32)],
                  sem=("parallel", "arbitrary"))


def final_loss(name, sc, h, g2, target):
    S, D = h.shape
    tr = _pick(S, 256, 8)

    def body(sc_ref, h_ref, g_ref, t_ref, loss_ref, dx_ref, dxb_ref, dg_ref):
        i = pl.program_id(0)
        x = h_ref[...]
        r = lax.rsqrt(jnp.mean(x * x, axis=-1, keepdims=True) + EPS)
        xh = x * r
        err = xh * g_ref[...] - t_ref[...]
        part_loss = 0.5 * jnp.sum(jnp.mean(err * err, axis=-1, keepdims=True), axis=0, keepdims=True)
        dy = err * (1.0 / D)
        dxh = dy * g_ref[...]
        dx = r * (dxh - xh * jnp.mean(dxh * xh, axis=-1, keepdims=True))
        dx_ref[...] = dx
        dxb_ref[...] = dx.astype(BF16)
        part_g = jnp.sum(dy * xh, axis=0, keepdims=True)

        @pl.when(i == 0)
        def _():
            loss_ref[...] = jnp.zeros_like(loss_ref) + part_loss
            dg_ref[...] = part_g

        @pl.when(i > 0)
        def _():
            loss_ref[...] += part_loss
            dg_ref[...] += part_g

    row = lambda a: (a, pl.BlockSpec((tr, D), lambda i, sc: (i, 0)))
    return _pcall(body, name=name, grid=(S // tr,), sc=sc,
                  ins=[row(h), (g2, pl.BlockSpec((1, D), lambda i, sc: (0, 0))), row(target)],
                  outs=[(_sds((8, LANES), F32), pl.BlockSpec((8, LANES), lambda i, sc: (0, 0))),
                        row(_sds((S, D), F32)), row(_sds((S, D), BF16)),
                        (_sds((1, D), F32), pl.BlockSpec((1, D), lambda i, sc: (0, 0)))])


def _split3(a):
    hi = a.astype(BF16)
    r1 = a - hi.astype(F32)
    mid = r1.astype(BF16)
    lo = (r1 - mid.astype(F32)).astype(BF16)
    return hi, mid, lo


def _rel_onehot(qi, rows_are_keys):
    shape = (BANDP, REL_PAD) if rows_are_keys else (REL_PAD, BANDP)
    km = lax.broadcasted_iota(jnp.int32, shape, 0 if rows_are_keys else 1)
    idx = lax.broadcasted_iota(jnp.int32, shape, 1 if rows_are_keys else 0)
    rel = jnp.clip(LEFT_CHUNKS * CHUNK + qi - km, -MAX_REL, MAX_REL) + MAX_REL
    return jnp.where(rel == idx, 1.0, 0.0).astype(BF16)


def bias_expand(name, sc, rel3, H):
    def body(sc_ref, rel_ref, o_ref):
        qi = pl.program_id(0)
        oh = _rel_onehot(qi, False)
        acc = jnp.zeros((H, BANDP), F32)
        for part in _split3(rel_ref[...]):
            acc = acc + jnp.dot(part, oh, preferred_element_type=F32)
        o_ref[...] = acc

    return _pcall(body, name=name, grid=(CHUNK,), sc=sc,
                  ins=[(rel3, _bs(rel3, (H, REL_PAD), lambda q, sc: (0, 0), L0))],
                  outs=[(_sds((CHUNK, H, BANDP), F32), pl.BlockSpec((None, H, BANDP), lambda q, sc: (q, 0, 0)))])[0]


def bias_reduce(name, sc, dbias_q, drel_buf, H):
    def body(sc_ref, d_ref, o_ref):
        qi = pl.program_id(0)
        oh = _rel_onehot(qi, True)
        acc = jnp.zeros((H, REL_PAD), F32)
        for part in _split3(d_ref[...]):
            acc = acc + jnp.dot(part, oh, preferred_element_type=F32)

        @pl.when(qi == 0)
        def _():
            o_ref[...] = acc

        @pl.when(qi > 0)
        def _():
            o_ref[...] += acc

    return _pcall(body, name=name, grid=(CHUNK,), sc=sc,
                  ins=[(dbias_q, pl.BlockSpec((None, H, BANDP), lambda q, sc: (q, 0, 0)))],
                  outs=[(drel_buf, _bs(drel_buf, (H, REL_PAD), lambda q, sc: (0, 0), L0))])[0]


def _attn_probs(q2, k2, bias_h, sub, c, first, scale):
    lane = lax.broadcasted_iota(jnp.int32, q2.shape, 1)
    qm = jnp.where((lane // ATTN_HEAD_DIM) == sub, q2, jnp.zeros_like(q2))
    s = lax.dot_general(qm, k2, (((1,), (1,)), ((), ())), preferred_element_type=F32) * scale + bias_h
    km = lax.broadcasted_iota(jnp.int32, s.shape, 1)
    valid = (km < BAND) & (jnp.logical_not(first) | (km + c * CHUNK >= QBLK))
    s = jnp.where(valid, s, NEG_INF)
    p = jnp.exp(s - jnp.max(s, axis=-1, keepdims=True))
    return qm, p / jnp.sum(p, axis=-1, keepdims=True)


def _fill_window(win, prev_ref, cur_ref, A):
    win[0:QBLK, :] = prev_ref[...].astype(BF16)
    win[QBLK:2 * QBLK, :] = cur_ref[...].astype(BF16)
    win[2 * QBLK:2 * QBLK + CHUNK, :] = jnp.zeros((CHUNK, A), BF16)


def attn_fwd(name, sc, proj, bias_buf, ao_buf, A, qcol):
    S = proj.shape[1]
    nb = S // QBLK
    H = A // ATTN_HEAD_DIM
    npair = A // LANES
    scale = ATTN_HEAD_DIM ** -0.5

    def body(sc_ref, q_ref, kp_ref, kc_ref, vp_ref, vc_ref, b_ref, o_ref, kw, vw):
        first = pl.program_id(0) == 0
        _fill_window(kw, kp_ref, kc_ref, A)
        _fill_window(vw, vp_ref, vc_ref, A)

        def chunk(c, carry):
            r0 = pl.multiple_of(c * CHUNK, CHUNK)
            for hp in range(npair):
                cols = slice(hp * LANES, (hp + 1) * LANES)
                q2 = q_ref[pl.ds(r0, CHUNK), cols].astype(BF16)
                k2 = kw[pl.ds(r0, BANDP), cols]
                v2 = vw[pl.ds(r0, BANDP), cols]
                lane = lax.broadcasted_iota(jnp.int32, (CHUNK, LANES), 1)
                out2 = jnp.zeros((CHUNK, LANES), F32)
                for sub in range(LANES // ATTN_HEAD_DIM):
                    _, p = _attn_probs(q2, k2, b_ref[hp * 2 + sub], sub, c, first, scale)
                    o = jnp.dot(p.astype(BF16), v2, preferred_element_type=F32)
                    out2 = jnp.where((lane // ATTN_HEAD_DIM) == sub, o, out2)
                o_ref[pl.ds(r0, CHUNK), cols] = out2.astype(BF16)
            return carry

        lax.fori_loop(0, LEFT_CHUNKS, chunk, 0)

    prevb = lambda i: jnp.maximum(i - 1, 0)
    blk = lambda rowf, col: (proj, _bs(proj, (QBLK, A), lambda i, sc: (rowf(i), col), L0))
    same = lambda i: i
    return _pcall(body, name=name, grid=(nb,), sc=sc,
                  ins=[blk(same, qcol), blk(prevb, qcol + 1), blk(same, qcol + 1), blk(prevb, qcol + 2), blk(same, qcol + 2),
                       (bias_buf, pl.BlockSpec((None, H, CHUNK, BANDP), lambda i, sc: (sc[0], 0, 0, 0)))],
                  outs=[(ao_buf, _bs(ao_buf, (QBLK, A), lambda i, sc: (i, 0), L0))],
                  scratch=[pltpu.VMEM((2 * QBLK + CHUNK, A), BF16), pltpu.VMEM((2 * QBLK + CHUNK, A), BF16)])


def attn_bwd(name, sc, proj, bias_buf, dao, A, qcol):
    S = proj.shape[1]
    nb = S // QBLK
    H = A // ATTN_HEAD_DIM
    npair = A // LANES
    scale = ATTN_HEAD_DIM ** -0.5
    WIN = 2 * QBLK + CHUNK

    def body(sc_ref, q_ref, kp_ref, kc_ref, vp_ref, vc_ref, b_ref, do_ref, dq_ref, dk_ref, dv_ref, db_ref, kw, vw, dkw, dvw):
        i = pl.program_id(0)
        first = i == 0

        @pl.when(first)
        def _():
            dkw[0:QBLK, :] = jnp.zeros((QBLK, A), F32)
            dvw[0:QBLK, :] = jnp.zeros((QBLK, A), F32)
            db_ref[...] = jnp.zeros_like(db_ref)

        @pl.when(i > 0)
        def _():
            dkw[0:QBLK, :] = dkw[QBLK:2 * QBLK, :]
            dvw[0:QBLK, :] = dvw[QBLK:2 * QBLK, :]

        dkw[QBLK:WIN, :] = jnp.zeros((WIN - QBLK, A), F32)
        dvw[QBLK:WIN, :] = jnp.zeros((WIN - QBLK, A), F32)

        @pl.when(i < nb)
        def _():
            _fill_window(kw, kp_ref, kc_ref, A)
            _fill_window(vw, vp_ref, vc_ref, A)

            def chunk(c, carry):
                r0 = pl.multiple_of(c * CHUNK, CHUNK)
                for hp in range(npair):
                    cols = slice(hp * LANES, (hp + 1) * LANES)
                    q2 = q_ref[pl.ds(r0, CHUNK), cols].astype(BF16)
                    k2 = kw[pl.ds(r0, BANDP), cols]
                    v2 = vw[pl.ds(r0, BANDP), cols]
                    do2 = do_ref[pl.ds(r0, CHUNK), cols]
                    lane = lax.broadcasted_iota(jnp.int32, (CHUNK, LANES), 1)
                    dq2 = jnp.zeros((CHUNK, LANES), F32)
                    dk2 = jnp.zeros((BANDP, LANES), F32)
                    dv2 = jnp.zeros((BANDP, LANES), F32)
                    for sub in range(LANES // ATTN_HEAD_DIM):
                        h = hp * 2 + sub
                        qm, p = _attn_probs(q2, k2, b_ref[h], sub, c, first, scale)
                        dom = jnp.where((lane // ATTN_HEAD_DIM) == sub, do2, jnp.zeros_like(do2))
                        dp = lax.dot_general(dom, v2, (((1,), (1,)), ((), ())), preferred_element_type=F32)
                        ds = p * (dp - jnp.sum(dp * p, axis=-1, keepdims=True))
                        db_ref[h] += ds
                        dsb = ds.astype(BF16)
                        dq = jnp.dot(dsb, k2, preferred_element_type=F32) * scale
                        dq2 = jnp.where((lane // ATTN_HEAD_DIM) == sub, dq, dq2)
                        dk2 = dk2 + lax.dot_general(dsb, qm, (((0,), (0,)), ((), ())), preferred_element_type=F32) * scale
                        dv2 = dv2 + lax.dot_general(p.astype(BF16), dom, (((0,), (0,)), ((), ())), preferred_element_type=F32)
                    dq_ref[pl.ds(r0, CHUNK), cols] = dq2.astype(BF16)
                    dkw[pl.ds(r0, BANDP), cols] += dk2
                    dvw[pl.ds(r0, BANDP), cols] += dv2
                return carry

            lax.fori_loop(0, LEFT_CHUNKS, chunk, 0)

        dk_ref[...] = dkw[0:QBLK, :].astype(BF16)
        dv_ref[...] = dvw[0:QBLK, :].astype(BF16)

    cur = lambda i: jnp.minimum(i, nb - 1)
    prevb = lambda i: jnp.maximum(jnp.minimum(i, nb - 1) - 1, 0)
    done = lambda i: jnp.maximum(i - 1, 0)
    blk = lambda rowf, col: (proj, _bs(proj, (QBLK, A), lambda i, sc: (rowf(i), col), L0))
    o_sds = _sds((S, A), BF16)
    return _pcall(body, name=name, grid=(nb + 1,), sc=sc,
                  ins=[blk(cur, qcol), blk(prevb, qcol + 1), blk(cur, qcol + 1), blk(prevb, qcol + 2), blk(cur, qcol + 2),
                       (bias_buf, pl.BlockSpec((None, H, CHUNK, BANDP), lambda i, sc: (sc[0], 0, 0, 0))),
                       (dao, pl.BlockSpec((QBLK, A), lambda i, sc: (cur(i), 0)))],
                  outs=[(o_sds, pl.BlockSpec((QBLK, A), lambda i, sc: (cur(i), 0))),
                        (o_sds, pl.BlockSpec((QBLK, A), lambda i, sc: (done(i), 0))),
                        (o_sds, pl.BlockSpec((QBLK, A), lambda i, sc: (done(i), 0))),
                        (_sds((H, CHUNK, BANDP), F32), pl.BlockSpec((H, CHUNK, BANDP), lambda i, sc: (0, 0, 0)))],
                  scratch=[pltpu.VMEM((WIN, A), BF16), pltpu.VMEM((WIN, A), BF16),
                           pltpu.VMEM((WIN, A), F32), pltpu.VMEM((WIN, A), F32)])


def mem_fwd(name, sc, proj, kv_buf, mo_buf, Dm, qcol):
    S = proj.shape[1]
    NM = kv_buf.shape[1]
    tr = _pick(S, 512, 8)
    hd = Dm // N_MEM_HEADS
    scale = hd ** -0.5

    def body(sc_ref, q_ref, kv_ref, o_ref):
        for h in range(N_MEM_HEADS):
            cols = slice(h * hd, (h + 1) * hd)
            q = q_ref[:, cols].astype(BF16)
            k = kv_ref[:, cols].astype(BF16)
            v = kv_ref[:, Dm + h * hd:Dm + (h + 1) * hd].astype(BF16)
            s = lax.dot_general(q, k, (((1,), (1,)), ((), ())), preferred_element_type=F32) * scale
            p = jnp.exp(s - jnp.max(s, axis=-1, keepdims=True))
            p = p / jnp.sum(p, axis=-1, keepdims=True)
            o_ref[:, cols] = jnp.dot(p.astype(BF16), v, preferred_element_type=F32).astype(BF16)

    return _pcall(body, name=name, grid=(S // tr,), sc=sc,
                  ins=[(proj, _bs(proj, (tr, Dm), lambda i, sc: (i, qcol), L0)),
                       (kv_buf, _bs(kv_buf, (NM, 2 * Dm), lambda i, sc: (0, 0), L0))],
                  outs=[(mo_buf, _bs(mo_buf, (tr, Dm), lambda i, sc: (i, 0), L0))])


def mem_bwd(name, sc, proj, kv_buf, dmo, Dm, qcol):
    S = proj.shape[1]
    NM = kv_buf.shape[1]
    tr = _pick(S, 512, 8)
    hd = Dm // N_MEM_HEADS
    scale = hd ** -0.5

    def body(sc_ref, q_ref, kv_ref, do_ref, dq_ref, dkv_ref):
        i = pl.program_id(0)

        @pl.when(i == 0)
        def _():
            dkv_ref[...] = jnp.zeros_like(dkv_ref)

        for h in range(N_MEM_HEADS):
            cols = slice(h * hd, (h + 1) * hd)
            vcols = slice(Dm + h * hd, Dm + (h + 1) * hd)
            q = q_ref[:, cols].astype(BF16)
            k = kv_ref[:, cols].astype(BF16)
            v = kv_ref[:, vcols].astype(BF16)
            do = do_ref[:, cols]
            s = lax.dot_general(q, k, (((1,), (1,)), ((), ())), preferred_element_type=F32) * scale
            p = jnp.exp(s - jnp.max(s, axis=-1, keepdims=True))
            p = p / jnp.sum(p, axis=-1, keepdims=True)
            dp = lax.dot_general(do, v, (((1,), (1,)), ((), ())), preferred_element_type=F32)
            ds = p * (dp - jnp.sum(dp * p, axis=-1, keepdims=True))
            dsb = ds.astype(BF16)
            dq_ref[:, cols] = (jnp.dot(dsb, k, preferred_element_type=F32) * scale).astype(BF16)
            dkv_ref[:, cols] += lax.dot_general(dsb, q, (((0,), (0,)), ((), ())), preferred_element_type=F32) * scale
            dkv_ref[:, vcols] += lax.dot_general(p.astype(BF16), do, (((0,), (0,)), ((), ())), preferred_element_type=F32)

    return _pcall(body, name=name, grid=(S // tr,), sc=sc,
                  ins=[(proj, _bs(proj, (tr, Dm), lambda i, sc: (i, qcol), L0)),
                       (kv_buf, _bs(kv_buf, (NM, 2 * Dm), lambda i, sc: (0, 0), L0)),
                       (dmo, pl.BlockSpec((tr, Dm), lambda i, sc: (i, 0)))],
                  outs=[(_sds((S, Dm), BF16), pl.BlockSpec((tr, Dm), lambda i, sc: (i, 0))),
                        (_sds((NM, 2 * Dm), F32), pl.BlockSpec((NM, 2 * Dm), lambda i, sc: (0, 0)))])


def _rows2d(a):
    return a.reshape(-1, a.shape[-1])


def _ew(name, fn, ins, out_dtypes, tall=None):
    R, C = ins[0].shape
    tc = _pick(C, 2048)
    tr = _pick(R, max(8, (1 << 19) // tc), 8)
    n_in = len(ins) + (tall is not None)

    def body(sc_ref, *refs):
        outs = fn(*[r[...] for r in refs[:n_in]])
        for o_ref, o in zip(refs[n_in:], outs):
            o_ref[...] = o.astype(o_ref.dtype)

    plain = pl.BlockSpec((tr, tc), lambda i, j, sc: (i, j))
    items = [(a, plain) for a in ins]
    sc = jnp.zeros((1,), jnp.int32)
    if tall is not None:
        items.insert(0, (tall[0], pl.BlockSpec((tr, tc), lambda i, j, sc: (i + sc[0] * (R // tr), j))))
        sc = tall[1]
    return _pcall(body, name=name, grid=(R // tr, C // tc), sc=sc, ins=items,
                  outs=[(_sds((R, C), dt), plain) for dt in out_dtypes], sem=("parallel", "parallel"))


def _adamw_math(w, g, m, v):
    m = ADAM_B1 * m + (1.0 - ADAM_B1) * g
    v = ADAM_B2 * v + (1.0 - ADAM_B2) * (g * g)
    m_hat = m / (1.0 - ADAM_B1 ** ADAM_STEP)
    v_hat = v / (1.0 - ADAM_B2 ** ADAM_STEP)
    delta = -ADAM_LR * (m_hat / (jnp.sqrt(v_hat) + ADAM_EPS) + ADAM_WD * w)
    return delta, m, v


def adamw(name, w, g, m, v):
    shp = w.shape
    d, nm, nv = _ew(name, _adamw_math, [_rows2d(w), _rows2d(g), _rows2d(m), _rows2d(v)], (F32, F32, F32))
    return d.reshape(shp), nm.reshape(shp), nv.reshape(shp)


def _place():
    x, y, c = lax.axis_index("x"), lax.axis_index("y"), lax.axis_index("c")
    chips = [(1 - x, y), (x, 1 - y), (1 - x, 1 - y)]
    return x, y, c, chips


def _sub(ref, ax=None, s=None, n=None, half=None, lh=None):
    lay = slice(None) if half is None else pl.ds(half * lh, lh)
    if ax is None:
        return ref.at[lay]
    cut = pl.ds(pl.multiple_of(s * n, n), n)
    return ref.at[lay, cut, :] if ax == 1 else ref.at[lay, :, cut]


HBM_SPEC = pl.BlockSpec(memory_space=pl.ANY)


def gather_weights(shards, axes):
    n = len(shards)
    L = shards[0].shape[0]
    lh = L // 2

    def body(*refs):
        src, dst = refs[:n], refs[n:2 * n]
        ici_send, ici_recv, d2d_send, d2d_recv, loc = refs[2 * n:]
        x, y, c, chips = _place()
        sibling = (x, y, 1 - c)
        local, remote = [], []
        for k in range(n):
            ax = axes[k]
            ns = src[k].shape[ax]
            own = pltpu.make_async_copy(src[k], _sub(dst[k], ax, 2 * x + y, ns), loc.at[k])
            own.start()
            local.append(own)
            for j, (px, py) in enumerate(chips):
                cp = pltpu.make_async_remote_copy(
                    src_ref=_sub(src[k], half=c, lh=lh), dst_ref=_sub(dst[k], ax, 2 * x + y, ns, c, lh),
                    send_sem=ici_send.at[k, j], recv_sem=ici_recv.at[k, j], device_id=(px, py, c), device_id_type=MESH)
                cp.start()
                remote.append(cp)
        for k in range(n):
            ax = axes[k]
            ns = src[k].shape[ax]
            for j, (px, py) in enumerate(chips):
                theirs = _sub(dst[k], ax, 2 * px + py, ns, c, lh)
                pltpu.make_async_remote_copy(
                    src_ref=theirs, dst_ref=theirs, send_sem=ici_send.at[k, j], recv_sem=ici_recv.at[k, j],
                    device_id=(px, py, c), device_id_type=MESH).wait_recv()
                fwd = pltpu.make_async_remote_copy(
                    src_ref=theirs, dst_ref=theirs, send_sem=d2d_send.at[k, j], recv_sem=d2d_recv.at[k, j],
                    device_id=sibling, device_id_type=MESH)
                fwd.start()
                remote.append(fwd)
        for k in range(n):
            ax = axes[k]
            ns = src[k].shape[ax]
            for j, (px, py) in enumerate(chips):
                got = _sub(dst[k], ax, 2 * px + py, ns, 1 - c, lh)
                pltpu.make_async_remote_copy(
                    src_ref=got, dst_ref=got, send_sem=d2d_send.at[k, j], recv_sem=d2d_recv.at[k, j],
                    device_id=sibling, device_id_type=MESH).wait_recv()
        for cp in remote:
            cp.wait_send()
        for cp in local:
            cp.wait()

    out_shape = []
    for a, ax in zip(shards, axes):
        shp = list(a.shape)
        shp[ax] *= 4
        out_shape.append(_sds(shp, a.dtype))
    return pl.pallas_call(
        body, name="gather_weights", out_shape=out_shape,
        in_specs=[HBM_SPEC] * n, out_specs=[HBM_SPEC] * n,
        scratch_shapes=[pltpu.SemaphoreType.DMA((n, 3)), pltpu.SemaphoreType.DMA((n, 3)),
                        pltpu.SemaphoreType.DMA((n, 3)), pltpu.SemaphoreType.DMA((n, 3)), pltpu.SemaphoreType.DMA((n,))],
    )(*shards)


def swap_halves(name, grads):
    n = len(grads)
    lh = grads[0].shape[0] // 2

    def body(*refs):
        src, dst = refs[:n], refs[n:2 * n]
        send, recv = refs[2 * n:]
        x, y, c, _ = _place()
        cps = []
        for k in range(n):
            cp = pltpu.make_async_remote_copy(
                src_ref=_sub(src[k], half=1 - c, lh=lh), dst_ref=dst[k], send_sem=send.at[k], recv_sem=recv.at[k],
                device_id=(x, y, 1 - c), device_id_type=MESH)
            cp.start()
            cps.append(cp)
        for cp in cps:
            cp.wait()

    return pl.pallas_call(
        body, name=name, out_shape=[_sds((lh,) + g.shape[1:], g.dtype) for g in grads],
        in_specs=[HBM_SPEC] * n, out_specs=[HBM_SPEC] * n,
        scratch_shapes=[pltpu.SemaphoreType.DMA((n,)), pltpu.SemaphoreType.DMA((n,))],
    )(*grads)


def exchange_chip_sums(name, sums_f32, sums_bf16, axes):
    n = len(sums_f32)

    def body(*refs):
        sf, sb = refs[:n], refs[n:2 * n]
        own, got = refs[2 * n:3 * n], refs[3 * n:4 * n]
        send, recv, loc = refs[4 * n:]
        x, y, c, chips = _place()
        cps = []
        for k in range(n):
            ax = axes[k]
            ns = own[k].shape[ax]
            lc = pltpu.make_async_copy(_sub(sf[k], ax, 2 * x + y, ns), own[k], loc.at[k])
            lc.start()
            cps.append(lc)
            for j, (px, py) in enumerate(chips):
                cp = pltpu.make_async_remote_copy(
                    src_ref=_sub(sb[k], ax, 2 * px + py, ns), dst_ref=got[k].at[j],
                    send_sem=send.at[k, j], recv_sem=recv.at[k, j], device_id=(px, py, c), device_id_type=MESH)
                cp.start()
                cps.append(cp)
        for cp in cps:
            cp.wait()

    out_shape = []
    for a, ax in zip(sums_f32, axes):
        shp = list(a.shape)
        shp[ax] //= 4
        out_shape.append(_sds(shp, F32))
    for a, ax in zip(sums_f32, axes):
        shp = list(a.shape)
        shp[ax] //= 4
        out_shape.append(_sds([3] + shp, BF16))
    res = pl.pallas_call(
        body, name=name, out_shape=out_shape,
        in_specs=[HBM_SPEC] * (2 * n), out_specs=[HBM_SPEC] * (2 * n),
        scratch_shapes=[pltpu.SemaphoreType.DMA((n, 3)), pltpu.SemaphoreType.DMA((n, 3)), pltpu.SemaphoreType.DMA((n,))],
    )(*sums_f32, *sums_bf16)
    return res[:n], res[n:]


def join_halves(name, halves):
    n = len(halves)
    lh = halves[0].shape[0]

    def body(*refs):
        src, dst = refs[:n], refs[n:2 * n]
        send, recv, loc = refs[2 * n:]
        x, y, c, _ = _place()
        cps = []
        for k in range(n):
            lc = pltpu.make_async_copy(src[k], _sub(dst[k], half=c, lh=lh), loc.at[k])
            lc.start()
            cps.append(lc)
            cp = pltpu.make_async_remote_copy(
                src_ref=src[k], dst_ref=_sub(dst[k], half=c, lh=lh), send_sem=send.at[k], recv_sem=recv.at[k],
                device_id=(x, y, 1 - c), device_id_type=MESH)
            cp.start()
            cps.append(cp)
        for cp in cps:
            cp.wait()

    return pl.pallas_call(
        body, name=name, out_shape=[_sds((2 * lh,) + h.shape[1:], h.dtype) for h in halves],
        in_specs=[HBM_SPEC] * n, out_specs=[HBM_SPEC] * n,
        scratch_shapes=[pltpu.SemaphoreType.DMA((n,)), pltpu.SemaphoreType.DMA((n,)), pltpu.SemaphoreType.DMA((n,))],
    )(*halves)


def allreduce_small(name, packed):
    R = packed.shape[0]

    def body(x_ref, o_ref, buf, send, recv, loc):
        x, y, c, _ = _place()
        me = 4 * x + 2 * y + c
        cps = [pltpu.make_async_copy(x_ref, buf.at[me], loc)]
        cps[0].start()
        for d in range(1, 8):
            px, py, pc = x ^ (d >> 2), y ^ ((d >> 1) & 1), c ^ (d & 1)
            cp = pltpu.make_async_remote_copy(
                src_ref=x_ref, dst_ref=buf.at[me], send_sem=send.at[d - 1], recv_sem=recv.at[d - 1],
                device_id=(px, py, pc), device_id_type=MESH)
            cp.start()
            cps.append(cp)
        for cp in cps:
            cp.wait()
        acc = buf[0]
        for s in range(1, 8):
            acc = acc + buf[s]
        o_ref[...] = acc

    return pl.pallas_call(
        body, name=name, out_shape=_sds((R, LANES), F32),
        in_specs=[pl.BlockSpec(memory_space=pltpu.VMEM)], out_specs=pl.BlockSpec(memory_space=pltpu.VMEM),
        scratch_shapes=[pltpu.VMEM((8, R, LANES), F32), pltpu.SemaphoreType.DMA((7,)), pltpu.SemaphoreType.DMA((7,)),
                        pltpu.SemaphoreType.DMA],
        compiler_params=pltpu.CompilerParams(vmem_limit_bytes=VMEM_LIMIT),
    )(packed)


def _pack(arrs):
    flat = jnp.concatenate([a.reshape(-1) for a in arrs])
    pad = (-flat.shape[0]) % (8 * LANES)
    return jnp.pad(flat, (0, pad)).reshape(-1, LANES)


def _unpack(packed, like):
    flat = packed.reshape(-1)
    out, pos = [], 0
    for a in like:
        out.append(flat[pos:pos + a.size].reshape(a.shape))
        pos += a.size
    return out


def kernel(x, mem, mix_norm_g, mem_norm_g, w_in, gate_b, conv_w, conv_b, conv_ln_g, conv_ln_b, w_conv_out, rel_bias, w_attn_out, w_mem_kv, w_mem_out, w_o, ffn_norm_g, w_up, ffn_conv_w, ffn_conv_b, w_down, final_norm_g, loss_target, m_mix_norm_g, m_mem_norm_g, m_w_in, m_gate_b, m_conv_w, m_conv_b, m_conv_ln_g, m_conv_ln_b, m_w_conv_out, m_rel_bias, m_w_attn_out, m_w_mem_kv, m_w_mem_out, m_w_o, m_ffn_norm_g, m_w_up, m_ffn_conv_w, m_ffn_conv_b, m_w_down, m_final_norm_g, v_mix_norm_g, v_mem_norm_g, v_w_in, v_gate_b, v_conv_w, v_conv_b, v_conv_ln_g, v_conv_ln_b, v_w_conv_out, v_rel_bias, v_w_attn_out, v_w_mem_kv, v_w_mem_out, v_w_o, v_ffn_norm_g, v_w_up, v_ffn_conv_w, v_ffn_conv_b, v_w_down, v_final_norm_g):
    S, D = x.shape[1], x.shape[2]
    NM = mem.shape[1]
    L = w_in.shape[0]
    C = conv_b.shape[1]
    A = w_attn_out.shape[1]
    Dm = w_mem_out.shape[1]
    Fh = w_down.shape[1] * 4
    D_IN = w_in.shape[2] * 4
    CW = conv_w.shape[1]
    FW = ffn_conv_w.shape[1]
    H = A // ATTN_HEAD_DIM
    assert 2 * C == 2 * A == 2 * Dm == D and D_IN == 6 * D and S % QBLK == 0 and L % 2 == 0
    xi, yi, ci = lax.axis_index("x"), lax.axis_index("y"), lax.axis_index("c")
    chip = 2 * xi + yi
    c_sc = jnp.reshape(ci, (1,)).astype(jnp.int32)
    zero_sc = jnp.zeros((1,), jnp.int32)

    big = [w_in, w_conv_out, w_attn_out, w_mem_kv, w_mem_out, w_o, w_up, w_down]
    big_m = [m_w_in, m_w_conv_out, m_w_attn_out, m_w_mem_kv, m_w_mem_out, m_w_o, m_w_up, m_w_down]
    big_v = [v_w_in, v_w_conv_out, v_w_attn_out, v_w_mem_kv, v_w_mem_out, v_w_o, v_w_up, v_w_down]
    big_ax = [2, 2, 2, 1, 2, 1, 2, 1]
    shards = [_ew(f"cast_w{k}", lambda a: (a,), [_rows2d(w)], (BF16,))[0].reshape(w.shape) for k, w in enumerate(big)]
    full = gather_weights(shards + [conv_w, ffn_conv_w], big_ax + [2, 2])
    W_in, W_co, W_ao, W_kv, W_mo, W_o, W_up, W_dn, conv_w_f, ffn_conv_w_f = full

    r3 = lambda a: a.reshape(L, 1, a.shape[-1])
    mix_g3, mem_g3, ffn_g3, gb3 = r3(mix_norm_g), r3(mem_norm_g), r3(ffn_norm_g), r3(gate_b)
    cb3, lg3, lb3, fb3 = r3(conv_b), r3(conv_ln_g), r3(conv_ln_b), r3(ffn_conv_b)
    cw_p = jnp.pad(conv_w_f, ((0, 0), (0, CONV_HALO - CW), (0, 0)))
    fw_p = jnp.pad(ffn_conv_w_f, ((0, 0), (0, FFN_HALO - FW), (0, 0)))
    rel_p = jnp.pad(rel_bias, ((0, 0), (0, 0), (0, REL_PAD - rel_bias.shape[2])))

    x2, mem2, tgt2 = x[0], mem[0], loss_target[0]
    empty = lambda shape, dt: lax.empty(shape, dt)

    saved = dict(
        XN=empty((2 * L, S, D), BF16), HS=empty((2 * L, S, D), F32), PROJ=empty((L, S, D_IN), F32),
        HC=empty((L, S, C), F32), CACT=empty((L, S, C), BF16), AO=empty((L, S, A), BF16), MO=empty((L, S, Dm), BF16),
        Y=empty((L, 3, S, D), F32), MERGED=empty((L, S, D), BF16), UP=empty((L, S, 2 * Fh), F32),
        ACT=empty((L, S, Fh), BF16), MN=empty((L, NM, D), BF16), MEMS=empty((L, NM, D), F32),
        KV=empty((L, NM, 2 * Dm), F32), BIAS=empty((L, H, CHUNK, BANDP), F32))

    def fwd_layer(l, carry):
        h, sv = carry
        sv = dict(sv)
        sc = jnp.stack([l, 2 * l, 2 * l + 1]).astype(jnp.int32)
        sc_a = sc[jnp.array([0, 1])]
        sc_f = sc[jnp.array([0, 2])]
        sc_m = sc[jnp.array([0, 0])]
        sv["XN"], sv["HS"] = rms_fwd("rms_mix", sc_a, h, mix_g3, sv["XN"], sv["HS"])
        sv["PROJ"] = mm_nn("mm_in", sc_a, sv["XN"], L1, W_in, S, D, D_IN, sv["PROJ"], L0)
        sv["HC"], sv["CACT"] = conv_fwd("conv_fwd", sc, sv["PROJ"], cw_p, cb3, lg3, lb3, sv["HC"], sv["CACT"], C, CW)
        bias_q = bias_expand("bias_expand", sc, rel_p, H)
        sv["BIAS"] = lax.dynamic_update_slice(sv["BIAS"], jnp.transpose(bias_q, (1, 0, 2))[None], (l, 0, 0, 0))
        sv["AO"], = attn_fwd("attn_fwd", sc, sv["PROJ"], sv["BIAS"], sv["AO"], A, 2)
        sv["MN"], sv["MEMS"] = rms_fwd("rms_mem", sc_m, mem2, mem_g3, sv["MN"], sv["MEMS"])
        sv["KV"] = mm_nn("mm_kv", sc, sv["MN"], L0, W_kv, NM, D, 2 * Dm, sv["KV"], L0)
        sv["MO"], = mem_fwd("mem_fwd", sc, sv["PROJ"], sv["KV"], sv["MO"], Dm, 5)
        for b, (src, w, nm) in enumerate(((sv["CACT"], W_co, "mm_co"), (sv["AO"], W_ao, "mm_ao"), (sv["MO"], W_mo, "mm_mo"))):
            sv["Y"] = mm_nn(nm, sc, src, L0, w, S, C, D, sv["Y"], lambda s, b=b: (s[0], b))
        sv["MERGED"], = gate_fwd("gate_fwd", sc, sv["PROJ"], gb3, sv["Y"], sv["MERGED"], D)
        h2 = mm_nn("mm_o", sc, sv["MERGED"], L0, W_o, S, D, D, _sds((S, D), F32), add=h)
        sv["XN"], sv["HS"] = rms_fwd("rms_ffn", sc_f, h2, ffn_g3, sv["XN"], sv["HS"])
        sv["UP"] = mm_nn("mm_up", sc_f, sv["XN"], L1, W_up, S, D, 2 * Fh, sv["UP"], L0)
        sv["ACT"], = ffn_fwd("ffn_fwd", sc, sv["UP"], fw_p, fb3, sv["ACT"], Fh, FW)
        h3 = mm_nn("mm_down", sc, sv["ACT"], L0, W_dn, S, Fh, D, _sds((S, D), F32), add=h2)
        return h3, sv

    h_last, saved = lax.fori_loop(0, L, fwd_layer, (x2, saved))
    loss_t, dh, dhb, d_final_g = final_loss("final_loss", zero_sc, h_last, final_norm_g.reshape(1, D), tgt2)
    loss = lax.psum(loss_t[0, 0], ("x", "y", "c"))

    zeros = lambda shape: jnp.zeros(shape, F32)
    grads = dict(
        w_in=empty((L, D, D_IN), F32), w_conv_out=empty((L, C, D), F32), w_attn_out=empty((L, A, D), F32),
        w_mem_kv=empty((L, D, 2 * Dm), F32), w_mem_out=empty((L, Dm, D), F32), w_o=empty((L, D, D), F32),
        w_up=empty((L, D, 2 * Fh), F32), w_down=empty((L, Fh, D), F32),
        mix_g=zeros((L, 1, D)), mem_g=zeros((L, 1, D)), ffn_g=zeros((L, 1, D)), gate_b=zeros((L, 1, 3 * D)),
        conv_w=zeros((L, CONV_HALO, C)), conv_b=zeros((L, 1, C)), ln_g=zeros((L, 1, C)), ln_b=zeros((L, 1, C)),
        rel=zeros((L, H, REL_PAD)), fw_v=zeros((L, FFN_HALO, Fh)), fw_g=zeros((L, FFN_HALO, Fh)),
        fb_v=zeros((L, 1, Fh)), fb_g=zeros((L, 1, Fh)))
    zero_mem = jnp.zeros((NM, D), F32)

    def bwd_layer(it, carry):
        dh, dhb, g = carry
        g = dict(g)
        l = L - 1 - it
        sc = jnp.stack([l, 2 * l, 2 * l + 1]).astype(jnp.int32)
        sc_a = sc[jnp.array([0, 1])]
        sc_f = sc[jnp.array([0, 2])]
        sc_m = sc[jnp.array([0, 0])]
        sv = saved
        dact = mm_nt("mm_down_dx", sc, [(dhb, None)], W_dn, S, Fh, _sds((S, Fh), F32))
        g["w_down"] = mm_tn("mm_down_dw", sc, sv["ACT"], L0, dhb, None, S, Fh, D, g["w_down"], L0)
        dupv, dupg, g["fw_v"], g["fw_g"], g["fb_v"], g["fb_g"] = ffn_bwd(
            "ffn_bwd", sc, sv["UP"], dact, fw_p, fb3, g["fw_v"], g["fw_g"], g["fb_v"], g["fb_g"], Fh, FW)
        dhn = mm_nt("mm_up_dx", sc, [(dupv, None), (dupg, None)], W_up, S, D, _sds((S, D), F32))
        g["w_up"] = mm_tn("mm_up_dw_v", sc_f, sv["XN"], L1, dupv, None, S, D, Fh, g["w_up"], L0)
        g["w_up"] = mm_tn("mm_up_dw_g", sc_f, sv["XN"], L1, dupg, None, S, D, Fh, g["w_up"], L0, out_joff=Fh)
        dh2, dh2b, g["ffn_g"] = rms_bwd("rms_ffn_bwd", sc_f, sv["HS"], ffn_g3, dhn, dh, g["ffn_g"])
        dmerged = mm_nt("mm_o_dx", sc, [(dh2b, None)], W_o, S, D, _sds((S, D), F32))
        g["w_o"] = mm_tn("mm_o_dw", sc, sv["MERGED"], L0, dh2b, None, S, D, D, g["w_o"], L0)
        dy, dgates, g["gate_b"] = gate_bwd("gate_bwd", sc, sv["PROJ"], gb3, sv["Y"], dmerged, g["gate_b"], D)
        dcact = mm_nt("mm_co_dx", sc, [(dy, lambda s: (0,))], W_co, S, C, _sds((S, C), F32))
        dao = mm_nt("mm_ao_dx", sc, [(dy, lambda s: (1,))], W_ao, S, A, _sds((S, A), BF16), out_dtype=BF16)
        dmo = mm_nt("mm_mo_dx", sc, [(dy, lambda s: (2,))], W_mo, S, Dm, _sds((S, Dm), BF16), out_dtype=BF16)
        g["w_conv_out"] = mm_tn("mm_co_dw", sc, sv["CACT"], L0, dy, lambda s: (0,), S, C, D, g["w_conv_out"], L0)
        g["w_attn_out"] = mm_tn("mm_ao_dw", sc, sv["AO"], L0, dy, lambda s: (1,), S, A, D, g["w_attn_out"], L0)
        g["w_mem_out"] = mm_tn("mm_mo_dw", sc, sv["MO"], L0, dy, lambda s: (2,), S, Dm, D, g["w_mem_out"], L0)
        dhc, g["ln_g"], g["ln_b"], g["conv_b"] = conv_bwd1(
            "conv_bwd1", sc, sv["HC"], dcact, lg3, lb3, g["ln_g"], g["ln_b"], g["conv_b"])
        dconv, g["conv_w"] = conv_bwd2("conv_bwd2", sc, sv["PROJ"], dhc, cw_p, g["conv_w"], C, CW)
        dq, dk, dv, dbias = attn_bwd("attn_bwd", sc, sv["PROJ"], sv["BIAS"], dao, A, 2)
        g["rel"] = bias_reduce("bias_reduce", sc, jnp.transpose(dbias, (1, 0, 2)), g["rel"], H)
        dqm, dkv = mem_bwd("mem_bwd", sc, sv["PROJ"], sv["KV"], dmo, Dm, 5)
        g["w_mem_kv"] = mm_tn("mm_kv_dw", sc, sv["MN"], L0, dkv, None, NM, D, 2 * Dm, g["w_mem_kv"], L0)
        dmn = mm_nt("mm_kv_dx", sc, [(dkv, None)], W_kv, NM, D, _sds((NM, D), F32))
        _, _, g["mem_g"] = rms_bwd("rms_mem_bwd", sc_m, sv["MEMS"], mem_g3, dmn, zero_mem, g["mem_g"])
        pieces = [(dconv, None), (dq, None), (dk, None), (dv, None), (dqm, None), (dgates, None)]
        dxn = mm_nt("mm_in_dx", sc, pieces, W_in, S, D, _sds((S, D), F32), tk=1024)
        off = 0
        for nm, (p, _) in zip(("c", "q", "k", "v", "m", "g"), pieces):
            g["w_in"] = mm_tn("mm_in_dw_" + nm, sc_a, sv["XN"], L1, p, None, S, D, p.shape[1], g["w_in"], L0, out_joff=off)
            off += p.shape[1]
        dh0, dh0b, g["mix_g"] = rms_bwd("rms_mix_bwd", sc_a, sv["HS"], mix_g3, dxn, dh2, g["mix_g"])
        return dh0, dh0b, g

    grad_x2, _, grads = lax.fori_loop(0, L, bwd_layer, (dh, dhb, grads))

    names = ["w_in", "w_conv_out", "w_attn_out", "w_mem_kv", "w_mem_out", "w_o", "w_up", "w_down"]
    gl = [grads[nm] for nm in names]
    lh = L // 2
    from_sib = swap_halves("swap_halves", gl)
    sums_f, sums_b = [], []
    for k, (g_, r_) in enumerate(zip(gl, from_sib)):
        s_f, s_b = _ew(f"sum_sib{k}", lambda a, b: (a + b, a + b), [_rows2d(r_)], (F32, BF16), tall=(_rows2d(g_), c_sc))
        sums_f.append(s_f.reshape(r_.shape))
        sums_b.append(s_b.reshape(r_.shape))
    own, got = exchange_chip_sums("exchange_chip_sums", sums_f, sums_b, big_ax)
    halves = []
    for k, (o_, r_) in enumerate(zip(own, got)):
        fin, = _ew(f"sum_chips{k}", lambda a, b0, b1, b2: (((a + b0.astype(F32)) + b1.astype(F32)) + b2.astype(F32),),
                   [_rows2d(o_), _rows2d(r_[0]), _rows2d(r_[1]), _rows2d(r_[2])], (F32,))
        halves.append(fin.reshape(o_.shape))
    big_g = join_halves("join_halves", halves)
    big_d, big_nm, big_nv = [], [], []
    for k in range(len(big)):
        d_, m_, v_ = adamw(f"adamw_big{k}", big[k], big_g[k], big_m[k], big_v[k])
        big_d.append(d_), big_nm.append(m_), big_nv.append(v_)

    g_small_full = [
        grads["mix_g"].reshape(L, D), grads["mem_g"].reshape(L, D), grads["gate_b"].reshape(L, 3 * D),
        grads["conv_w"][:, :CW, :], grads["conv_b"].reshape(L, C), grads["ln_g"].reshape(L, C), grads["ln_b"].reshape(L, C),
        grads["rel"][:, :, :rel_bias.shape[2]], grads["ffn_g"].reshape(L, D),
        jnp.concatenate([grads["fw_v"][:, :FW, :], grads["fw_g"][:, :FW, :]], axis=-1),
        jnp.concatenate([grads["fb_v"], grads["fb_g"]], axis=-1).reshape(L, 2 * Fh), d_final_g.reshape(D)]
    summed = _unpack(allreduce_small("allreduce_small", _pack(g_small_full)), g_small_full)
    cws, fws = conv_w.shape[2], ffn_conv_w.shape[2]
    summed[3] = lax.dynamic_slice_in_dim(summed[3], chip * cws, cws, axis=2)
    summed[9] = lax.dynamic_slice_in_dim(summed[9], chip * fws, fws, axis=2)
    small_w = [mix_norm_g, mem_norm_g, gate_b, conv_w, conv_b, conv_ln_g, conv_ln_b, rel_bias, ffn_norm_g, ffn_conv_w, ffn_conv_b, final_norm_g]
    small_m = [m_mix_norm_g, m_mem_norm_g, m_gate_b, m_conv_w, m_conv_b, m_conv_ln_g, m_conv_ln_b, m_rel_bias, m_ffn_norm_g, m_ffn_conv_w, m_ffn_conv_b, m_final_norm_g]
    small_v = [v_mix_norm_g, v_mem_norm_g, v_gate_b, v_conv_w, v_conv_b, v_conv_ln_g, v_conv_ln_b, v_rel_bias, v_ffn_norm_g, v_ffn_conv_w, v_ffn_conv_b, v_final_norm_g]
    sd, sm, sv_ = _ew("adamw_small", _adamw_math, [_pack(small_w), _pack(summed), _pack(small_m), _pack(small_v)], (F32, F32, F32))
    small_d, small_nm, small_nv = _unpack(sd, small_w), _unpack(sm, small_w), _unpack(sv_, small_w)

    order = ["mix_norm_g", "mem_norm_g", "w_in", "gate_b", "conv_w", "conv_b", "conv_ln_g", "conv_ln_b", "w_conv_out",
             "rel_bias", "w_attn_out", "w_mem_kv", "w_mem_out", "w_o", "ffn_norm_g", "w_up", "ffn_conv_w", "ffn_conv_b",
             "w_down", "final_norm_g"]
    small_names = ["mix_norm_g", "mem_norm_g", "gate_b", "conv_w", "conv_b", "conv_ln_g", "conv_ln_b", "rel_bias",
                   "ffn_norm_g", "ffn_conv_w", "ffn_conv_b", "final_norm_g"]

    def collect(bigs, smalls):
        table = dict(zip(names, bigs))
        table.update(zip(small_names, smalls))
        return [table[nm] for nm in order]

    return (loss, grad_x2[None], *collect(big_g, summed), *collect(big_d, small_d), *collect(big_nm, small_nm),
            *collect(big_nv, small_nv))
```

```python
import jax
import jax.numpy as jnp
from jax import lax
from jax.experimental import pallas as pl
from jax.experimental.pallas import tpu as pltpu

F32 = jnp.float32
BF16 = jnp.bfloat16
MESH = pl.DeviceIdType.MESH

CHUNK = 64
LEFT_CHUNKS = 8
BAND = (LEFT_CHUNKS + 1) * CHUNK
BANDP = BAND + CHUNK
MAX_REL = 256
REL_PAD = 640
ATTN_HEAD_DIM = 64
N_MEM_HEADS = 4
QBLK = LEFT_CHUNKS * CHUNK
EPS = 1e-6
NEG_INF = -1e30
LANES = 128
VMEM_LIMIT = 56 * 1024 * 1024

ADAM_LR = 0.001
ADAM_B1 = 0.9
ADAM_B2 = 0.999
ADAM_EPS = 1e-08
ADAM_WD = 0.01
ADAM_STEP = 10


def _pick(dim, target, unit=LANES):
    if dim <= target:
        return dim
    d = (target // unit) * unit
    while d >= unit:
        if dim % d == 0:
            return d
        d -= unit
    raise ValueError(f"no tile for {dim} under {target}")


def _sigmoid(x):
    return 1.0 / (1.0 + jnp.exp(-x))


def _pcall(body, *, name, grid, sc, ins, outs, scratch=(), sem=None):
    arrays = [a for a, _ in ins]
    in_specs = [s for _, s in ins]
    n_in = len(arrays)
    out_shape, out_specs, aliases = [], [], {}
    for k, (o, spec) in enumerate(outs):
        if isinstance(o, jax.ShapeDtypeStruct):
            out_shape.append(o)
        else:
            aliases[1 + len(arrays)] = k
            arrays.append(o)
            in_specs.append(pl.BlockSpec(memory_space=pl.ANY))
            out_shape.append(jax.ShapeDtypeStruct(o.shape, o.dtype))
        out_specs.append(spec)
    n_alias = len(arrays) - n_in

    def wrapped(sc_ref, *refs):
        body(sc_ref, *refs[:n_in], *refs[n_in + n_alias:])

    res = pl.pallas_call(
        wrapped,
        name=name,
        grid_spec=pltpu.PrefetchScalarGridSpec(
            num_scalar_prefetch=1, grid=grid, in_specs=in_specs, out_specs=out_specs, scratch_shapes=list(scratch)),
        out_shape=out_shape,
        input_output_aliases=aliases,
        compiler_params=pltpu.CompilerParams(
            dimension_semantics=sem or ("arbitrary",) * len(grid), vmem_limit_bytes=VMEM_LIMIT),
    )(sc, *arrays)
    return res


def _bs(arr, blk, rc, lead=None):
    nlead = len(arr.shape) - 2

    def imap(*ids):
        sc = ids[-1]
        r, c = rc(*ids)
        return (*(lead(sc) if nlead else ()), r, c)

    return pl.BlockSpec((None,) * nlead + tuple(blk), imap)


def _sds(shape, dtype):
    return jax.ShapeDtypeStruct(tuple(shape), dtype)


L0 = lambda sc: (sc[0],)
L1 = lambda sc: (sc[1],)


def _mm_core(name, grid, sc, a_items, b_item, add_item, out_item, dims, ranges, out_dtype, acc_shape):
    nk = grid[2]
    na = len(a_items)
    has_add = add_item is not None

    def body(sc_ref, *refs):
        a_refs = refs[:na]
        b_ref = refs[na]
        pos = na + 1
        add_ref = refs[pos] if has_add else None
        pos += int(has_add)
        o_ref = refs[pos]
        acc = refs[pos + 1] if nk > 1 else None
        k = pl.program_id(2)

        def contrib(ar):
            return lax.dot_general(ar[...].astype(BF16), b_ref[...].astype(BF16), dims, preferred_element_type=F32)

        def fin(val):
            if has_add:
                val = val + add_ref[...]
            o_ref[...] = val.astype(out_dtype)

        if nk == 1:
            fin(contrib(a_refs[0]))
            return
        for p, (k0, k1) in enumerate(ranges):
            @pl.when((k >= k0) & (k < k1))
            def _(p=p):
                part = contrib(a_refs[p])

                @pl.when(k == 0)
                def _():
                    acc[...] = part

                @pl.when((k > 0) & (k < nk - 1))
                def _():
                    acc[...] += part

                @pl.when(k == nk - 1)
                def _():
                    fin(acc[...] + part)

    ins = list(a_items) + [b_item] + ([add_item] if has_add else [])
    scratch = [pltpu.VMEM(acc_shape, F32)] if nk > 1 else []
    return _pcall(body, name=name, grid=grid, sc=sc, ins=ins, outs=[out_item], scratch=scratch,
                  sem=("parallel", "parallel", "arbitrary"))[0]


def mm_nn(name, sc, a, a_lead, w, M, K, N, out, out_lead=None, out_joff=0, add=None, out_dtype=F32,
          tm=1024, tn=1408, tk=2048):
    tm, tn, tk = _pick(M, tm, 8), _pick(N, tn), _pick(K, tk)
    nk = K // tk
    a_item = (a, _bs(a, (tm, tk), lambda i, j, k, sc: (i, k), a_lead))
    b_item = (w, _bs(w, (tk, tn), lambda i, j, k, sc: (k, j), L0))
    add_item = None if add is None else (add, _bs(add, (tm, tn), lambda i, j, k, sc: (i, j)))
    out_spec = _bs(out, (tm, tn), lambda i, j, k, sc: (i, j + out_joff // tn), out_lead)
    return _mm_core(name, (M // tm, N // tn, nk), sc, [a_item], b_item, add_item, (out, out_spec),
                    (((1,), (0,)), ((), ())), [(0, nk)], out_dtype, (tm, tn))


def mm_nt(name, sc, pieces, w, M, N, out, w_koff=0, out_dtype=F32, tm=1024, tn=1408, tk=2048):
    tm, tn = _pick(M, tm, 8), _pick(N, tn)
    widths = [a.shape[-1] for a, _ in pieces]
    tk = _pick(widths[0], tk)
    offs = [w_koff]
    for wd in widths:
        offs.append(offs[-1] + wd)
    while any(x % tk for x in offs):
        tk = _pick(widths[0], tk - LANES)
    ranges, a_items, k0 = [], [], 0
    for (a, lead), wd in zip(pieces, widths):
        n = wd // tk
        ranges.append((k0, k0 + n))
        a_items.append((a, _bs(a, (tm, tk), lambda i, j, k, sc, k0=k0, n=n: (i, jnp.clip(k - k0, 0, n - 1)), lead)))
        k0 += n
    nk = k0
    b_item = (w, _bs(w, (tn, tk), lambda i, j, k, sc: (j, k + w_koff // tk), L0))
    out_spec = _bs(out, (tm, tn), lambda i, j, k, sc: (i, j))
    return _mm_core(name, (M // tm, N // tn, nk), sc, a_items, b_item, None, (out, out_spec),
                    (((1,), (1,)), ((), ())), ranges, out_dtype, (tm, tn))


def mm_tn(name, sc, a, a_lead, b, b_lead, S, K, N, out, out_lead, out_joff=0, tm=1408, tn=1408, tk=2048):
    tm, tn, tk = _pick(K, tm), _pick(N, tn), _pick(S, tk, 8)
    while out_joff % tn:
        tn = _pick(N, tn - LANES)
    nk = S // tk
    a_item = (a, _bs(a, (tk, tm), lambda i, j, k, sc: (k, i), a_lead))
    b_item = (b, _bs(b, (tk, tn), lambda i, j, k, sc: (k, j), b_lead))
    out_spec = _bs(out, (tm, tn), lambda i, j, k, sc: (i, j + out_joff // tn), out_lead)
    return _mm_core(name, (K // tm, N // tn, nk), sc, [a_item], b_item, None, (out, out_spec),
                    (((0,), (0,)), ((), ())), [(0, nk)], F32, (tm, tn))


def rms_fwd(name, sc, h, g3, xn_buf, hs_buf):
    S, D = h.shape
    tr = _pick(S, 256, 8)

    def body(sc_ref, h_ref, g_ref, xn_ref, hs_ref):
        x = h_ref[...]
        r = lax.rsqrt(jnp.mean(x * x, axis=-1, keepdims=True) + EPS)
        xn_ref[...] = (x * r * g_ref[...]).astype(BF16)
        hs_ref[...] = x

    return _pcall(body, name=name, grid=(S // tr,), sc=sc,
                  ins=[(h, _bs(h, (tr, D), lambda i, sc: (i, 0))), (g3, _bs(g3, (1, D), lambda i, sc: (0, 0), L0))],
                  outs=[(xn_buf, _bs(xn_buf, (tr, D), lambda i, sc: (i, 0), L1)),
                        (hs_buf, _bs(hs_buf, (tr, D), lambda i, sc: (i, 0), L1))])


def rms_bwd(name, sc, hs_buf, g3, dy, dres, dg_buf):
    S, D = dy.shape
    tr = _pick(S, 256, 8)

    def body(sc_ref, x_ref, g_ref, dy_ref, dres_ref, dx_ref, dxb_ref, dg_ref):
        i = pl.program_id(0)
        x = x_ref[...]
        r = lax.rsqrt(jnp.mean(x * x, axis=-1, keepdims=True) + EPS)
        xh = x * r
        dyv = dy_ref[...]
        dxh = dyv * g_ref[...]
        dx = r * (dxh - xh * jnp.mean(dxh * xh, axis=-1, keepdims=True)) + dres_ref[...]
        dx_ref[...] = dx
        dxb_ref[...] = dx.astype(BF16)
        part = jnp.sum(dyv * xh, axis=0, keepdims=True)

        @pl.when(i == 0)
        def _():
            dg_ref[...] = part

        @pl.when(i > 0)
        def _():
            dg_ref[...] += part

    return _pcall(body, name=name, grid=(S // tr,), sc=sc,
                  ins=[(hs_buf, _bs(hs_buf, (tr, D), lambda i, sc: (i, 0), L1)),
                       (g3, _bs(g3, (1, D), lambda i, sc: (0, 0), L0)),
                       (dy, _bs(dy, (tr, D), lambda i, sc: (i, 0))),
                       (dres, _bs(dres, (tr, D), lambda i, sc: (i, 0)))],
                  outs=[(_sds((S, D), F32), pl.BlockSpec((tr, D), lambda i, sc: (i, 0))),
                        (_sds((S, D), BF16), pl.BlockSpec((tr, D), lambda i, sc: (i, 0))),
                        (dg_buf, _bs(dg_buf, (1, D), lambda i, sc: (0, 0), L0))])


CONV_HALO = 32
CONV_RB = 32


def conv_fwd(name, sc, proj, cw, cb, lg, lb, hc_buf, cact_buf, C, W):
    S = proj.shape[1]
    T = _pick(S, 256, CONV_HALO)
    nh = T // CONV_HALO
    off = CONV_HALO - (W - 1)

    def body(sc_ref, ua_ref, ug_ref, pa_ref, pg_ref, w_ref, b_ref, lg_ref, lb_ref, hc_ref, ca_ref, buf):
        i = pl.program_id(0)
        halo = pa_ref[...] * _sigmoid(pg_ref[...])
        buf[0:CONV_HALO, :] = jnp.where(i > 0, halo, 0.0)
        buf[CONV_HALO:CONV_HALO + T, :] = ua_ref[...] * _sigmoid(ug_ref[...])
        for r in range(0, T, CONV_RB):
            acc = jnp.zeros((CONV_RB, C), F32) + b_ref[...]
            for k in range(W):
                acc = acc + w_ref[k:k + 1, :] * buf[r + off + k:r + off + k + CONV_RB, :]
            hc_ref[r:r + CONV_RB, :] = acc
        hc = hc_ref[...]
        mu = jnp.mean(hc, axis=-1, keepdims=True)
        d = hc - mu
        rstd = lax.rsqrt(jnp.mean(d * d, axis=-1, keepdims=True) + EPS)
        y = d * rstd * lg_ref[...] + lb_ref[...]
        ca_ref[...] = (y * _sigmoid(y)).astype(BF16)

    vec = lambda a: (a, _bs(a, (1, C), lambda i, sc: (0, 0), L0))
    return _pcall(body, name=name, grid=(S // T,), sc=sc,
                  ins=[(proj, _bs(proj, (T, C), lambda i, sc: (i, 0), L0)),
                       (proj, _bs(proj, (T, C), lambda i, sc: (i, 1), L0)),
                       (proj, _bs(proj, (CONV_HALO, C), lambda i, sc: (jnp.maximum(i * nh - 1, 0), 0), L0)),
                       (proj, _bs(proj, (CONV_HALO, C), lambda i, sc: (jnp.maximum(i * nh - 1, 0), 1), L0)),
                       (cw, _bs(cw, (CONV_HALO, C), lambda i, sc: (0, 0), L0)), vec(cb), vec(lg), vec(lb)],
                  outs=[(hc_buf, _bs(hc_buf, (T, C), lambda i, sc: (i, 0), L0)),
                        (cact_buf, _bs(cact_buf, (T, C), lambda i, sc: (i, 0), L0))],
                  scratch=[pltpu.VMEM((T + CONV_HALO, C), F32)])


def conv_bwd1(name, sc, hc_buf, dcact, lg, lb, dlg_buf, dlb_buf, dcb_buf):
    S, C = dcact.shape
    tr = _pick(S, 256, 8)

    def body(sc_ref, hc_ref, dc_ref, lg_ref, lb_ref, dhc_ref, dlg_ref, dlb_ref, dcb_ref):
        i = pl.program_id(0)
        hc = hc_ref[...]
        mu = jnp.mean(hc, axis=-1, keepdims=True)
        d = hc - mu
        rstd = lax.rsqrt(jnp.mean(d * d, axis=-1, keepdims=True) + EPS)
        yh = d * rstd
        y = yh * lg_ref[...] + lb_ref[...]
        sg = _sigmoid(y)
        dy = dc_ref[...] * (sg * (1.0 + y * (1.0 - sg)))
        dyh = dy * lg_ref[...]
        dhc = rstd * (dyh - jnp.mean(dyh, axis=-1, keepdims=True) - yh * jnp.mean(dyh * yh, axis=-1, keepdims=True))
        dhc_ref[...] = dhc
        parts = (jnp.sum(dy * yh, axis=0, keepdims=True), jnp.sum(dy, axis=0, keepdims=True),
                 jnp.sum(dhc, axis=0, keepdims=True))

        @pl.when(i == 0)
        def _():
            dlg_ref[...], dlb_ref[...], dcb_ref[...] = parts

        @pl.when(i > 0)
        def _():
            dlg_ref[...] += parts[0]
            dlb_ref[...] += parts[1]
            dcb_ref[...] += parts[2]

    vec = lambda a: (a, _bs(a, (1, C), lambda i, sc: (0, 0), L0))
    return _pcall(body, name=name, grid=(S // tr,), sc=sc,
                  ins=[(hc_buf, _bs(hc_buf, (tr, C), lambda i, sc: (i, 0), L0)),
                       (dcact, _bs(dcact, (tr, C), lambda i, sc: (i, 0))), vec(lg), vec(lb)],
                  outs=[(_sds((S, C), F32), pl.BlockSpec((tr, C), lambda i, sc: (i, 0))),
                        vec(dlg_buf), vec(dlb_buf), vec(dcb_buf)])


def conv_bwd2(name, sc, proj, dhc, cw, dcw_buf, C, W):
    S = proj.shape[1]
    T = _pick(S, 256, CONV_HALO)
    nh = T // CONV_HALO
    nt = S // T
    off = CONV_HALO - (W - 1)

    def body(sc_ref, ua_ref, ug_ref, pa_ref, pg_ref, d_ref, dn_ref, w_ref, o_ref, dw_ref, buf, dbuf, dhg):
        i = pl.program_id(0)
        halo = pa_ref[...] * _sigmoid(pg_ref[...])
        buf[0:CONV_HALO, :] = jnp.where(i > 0, halo, 0.0)
        sg = _sigmoid(ug_ref[...])
        ua = ua_ref[...]
        buf[CONV_HALO:CONV_HALO + T, :] = ua * sg
        dbuf[0:T, :] = d_ref[...]
        dbuf[T:T + CONV_HALO, :] = jnp.where(i < nt - 1, dn_ref[...], 0.0)

        @pl.when(i == 0)
        def _():
            dw_ref[...] = jnp.zeros_like(dw_ref)

        for r in range(0, T, CONV_RB):
            acc = jnp.zeros((CONV_RB, C), F32)
            for k in range(W):
                acc = acc + w_ref[k:k + 1, :] * dbuf[r + (W - 1) - k:r + (W - 1) - k + CONV_RB, :]
            dhg[r:r + CONV_RB, :] = acc
        for k in range(W):
            dw_ref[k:k + 1, :] += jnp.sum(dbuf[0:T, :] * buf[off + k:off + k + T, :], axis=0, keepdims=True)
        dh = dhg[...]
        o_ref[:, 0:C] = (dh * sg).astype(BF16)
        o_ref[:, C:2 * C] = (dh * ua * sg * (1.0 - sg)).astype(BF16)

    return _pcall(body, name=name, grid=(nt,), sc=sc,
                  ins=[(proj, _bs(proj, (T, C), lambda i, sc: (i, 0), L0)),
                       (proj, _bs(proj, (T, C), lambda i, sc: (i, 1), L0)),
                       (proj, _bs(proj, (CONV_HALO, C), lambda i, sc: (jnp.maximum(i * nh - 1, 0), 0), L0)),
                       (proj, _bs(proj, (CONV_HALO, C), lambda i, sc: (jnp.maximum(i * nh - 1, 0), 1), L0)),
                       (dhc, _bs(dhc, (T, C), lambda i, sc: (i, 0))),
                       (dhc, _bs(dhc, (CONV_HALO, C), lambda i, sc: (jnp.minimum((i + 1) * nh, S // CONV_HALO - 1), 0))),
                       (cw, _bs(cw, (CONV_HALO, C), lambda i, sc: (0, 0), L0))],
                  outs=[(_sds((S, 2 * C), BF16), pl.BlockSpec((T, 2 * C), lambda i, sc: (i, 0))),
                        (dcw_buf, _bs(dcw_buf, (CONV_HALO, C), lambda i, sc: (0, 0), L0))],
                  scratch=[pltpu.VMEM((T + CONV_HALO, C), F32), pltpu.VMEM((T + CONV_HALO, C), F32),
                           pltpu.VMEM((T, C), F32)])


def gate_fwd(name, sc, proj, gb3, y_buf, merged_buf, D):
    S = proj.shape[1]
    tr, tc = _pick(S, 256, 8), _pick(D, 1024)
    nc = D // tc

    def body(sc_ref, g0, g1, g2, b0, b1, b2, y0, y1, y2, o_ref):
        acc = None
        for g, b, y in ((g0, b0, y0), (g1, b1, y1), (g2, b2, y2)):
            t = _sigmoid(g[...] + b[...]) * y[...]
            acc = t if acc is None else acc + t
        o_ref[...] = acc.astype(BF16)

    ins = [(proj, _bs(proj, (tr, tc), lambda i, j, sc, b=b: (i, (3 + b) * nc + j), L0)) for b in range(3)]
    ins += [(gb3, _bs(gb3, (1, tc), lambda i, j, sc, b=b: (0, b * nc + j), L0)) for b in range(3)]
    ins += [(y_buf, _bs(y_buf, (tr, tc), lambda i, j, sc: (i, j), lambda sc, b=b: (sc[0], b))) for b in range(3)]
    return _pcall(body, name=name, grid=(S // tr, nc), sc=sc, ins=ins,
                  outs=[(merged_buf, _bs(merged_buf, (tr, tc), lambda i, j, sc: (i, j), L0))])


def gate_bwd(name, sc, proj, gb3, y_buf, dmerged, dgb_buf, D):
    S = proj.shape[1]
    tr, tc = _pick(S, 256, 8), _pick(D, 1024)
    nc = D // tc

    def body(sc_ref, g_ref, b_ref, y_ref, dm_ref, dy_ref, dg_ref, dgb_ref):
        i = pl.program_id(2)
        g = _sigmoid(g_ref[...] + b_ref[...])
        dm = dm_ref[...]
        dy_ref[...] = (dm * g).astype(BF16)
        dgt = dm * y_ref[...] * g * (1.0 - g)
        dg_ref[...] = dgt.astype(BF16)
        part = jnp.sum(dgt, axis=0, keepdims=True)

        @pl.when(i == 0)
        def _():
            dgb_ref[...] = part

        @pl.when(i > 0)
        def _():
            dgb_ref[...] += part

    dy_sds = _sds((3, S, D), BF16)
    return _pcall(body, name=name, grid=(3, nc, S // tr), sc=sc,
                  ins=[(proj, _bs(proj, (tr, tc), lambda b, j, i, sc: (i, (3 + b) * nc + j), L0)),
                       (gb3, _bs(gb3, (1, tc), lambda b, j, i, sc: (0, b * nc + j), L0)),
                       (y_buf, pl.BlockSpec((None, None, tr, tc), lambda b, j, i, sc: (sc[0], b, i, j))),
                       (dmerged, _bs(dmerged, (tr, tc), lambda b, j, i, sc: (i, j)))],
                  outs=[(dy_sds, pl.BlockSpec((None, tr, tc), lambda b, j, i, sc: (b, i, j))),
                        (_sds((S, 3 * D), BF16), pl.BlockSpec((tr, tc), lambda b, j, i, sc: (i, b * nc + j))),
                        (dgb_buf, _bs(dgb_buf, (1, tc), lambda b, j, i, sc: (0, b * nc + j), L0))])


FFN_HALO = 8


def ffn_fwd(name, sc, up, fw, fb3, act_buf, Fh, W):
    S = up.shape[1]
    T, tc = _pick(S, 256, 8), _pick(Fh, 512)
    nf = Fh // tc
    nh = T // FFN_HALO
    off = FFN_HALO - (W - 1)

    def body(sc_ref, v_ref, g_ref, pv_ref, pg_ref, wv_ref, wg_ref, bv_ref, bg_ref, o_ref, bufv, bufg):
        i = pl.program_id(1)
        res = []
        for m_ref, p_ref, w_ref, b_ref, buf in ((v_ref, pv_ref, wv_ref, bv_ref, bufv), (g_ref, pg_ref, wg_ref, bg_ref, bufg)):
            buf[0:FFN_HALO, :] = jnp.where(i > 0, p_ref[...], 0.0)
            buf[FFN_HALO:FFN_HALO + T, :] = m_ref[...]
            acc = jnp.zeros((T, tc), F32) + b_ref[...]
            for k in range(W):
                acc = acc + w_ref[k:k + 1, :] * buf[off + k:off + k + T, :]
            res.append(acc)
        val, gt = res
        o_ref[...] = (gt * _sigmoid(gt) * val).astype(BF16)

    prev = lambda i: jnp.maximum(i * nh - 1, 0)
    return _pcall(body, name=name, grid=(nf, S // T), sc=sc,
                  ins=[(up, _bs(up, (T, tc), lambda j, i, sc: (i, j), L0)),
                       (up, _bs(up, (T, tc), lambda j, i, sc: (i, j + nf), L0)),
                       (up, _bs(up, (FFN_HALO, tc), lambda j, i, sc: (prev(i), j), L0)),
                       (up, _bs(up, (FFN_HALO, tc), lambda j, i, sc: (prev(i), j + nf), L0)),
                       (fw, _bs(fw, (FFN_HALO, tc), lambda j, i, sc: (0, j), L0)),
                       (fw, _bs(fw, (FFN_HALO, tc), lambda j, i, sc: (0, j + nf), L0)),
                       (fb3, _bs(fb3, (1, tc), lambda j, i, sc: (0, j), L0)),
                       (fb3, _bs(fb3, (1, tc), lambda j, i, sc: (0, j + nf), L0))],
                  outs=[(act_buf, _bs(act_buf, (T, tc), lambda j, i, sc: (i, j), L0))],
                  scratch=[pltpu.VMEM((T + FFN_HALO, tc), F32), pltpu.VMEM((T + FFN_HALO, tc), F32)])


def ffn_bwd(name, sc, up, dact, fw, fb3, dfw_v_buf, dfw_g_buf, dfb_v_buf, dfb_g_buf, Fh, W):
    S = up.shape[1]
    T, tc = _pick(S, 256, 8), _pick(Fh, 512)
    nf = Fh // tc
    nh = T // FFN_HALO
    nt = S // T
    off = FFN_HALO - (W - 1)
    TE = T + FFN_HALO

    def body(sc_ref, v_ref, g_ref, pv_ref, pg_ref, nv_ref, ng_ref, d_ref, dn_ref, wv_ref, wg_ref, bv_ref, bg_ref,
             ov_ref, og_ref, dwv_ref, dwg_ref, dbv_ref, dbg_ref, bufv, bufg, dv, dg):
        i = pl.program_id(1)
        ups = []
        for m_ref, p_ref, n_ref, w_ref, b_ref, buf in ((v_ref, pv_ref, nv_ref, wv_ref, bv_ref, bufv),
                                                      (g_ref, pg_ref, ng_ref, wg_ref, bg_ref, bufg)):
            buf[0:FFN_HALO, :] = jnp.where(i > 0, p_ref[...], 0.0)
            buf[FFN_HALO:FFN_HALO + T, :] = m_ref[...]
            buf[FFN_HALO + T:FFN_HALO + TE, :] = n_ref[...]
            acc = jnp.zeros((TE, tc), F32) + b_ref[...]
            for k in range(W):
                acc = acc + w_ref[k:k + 1, :] * buf[off + k:off + k + TE, :]
            ups.append(acc)
        val, gt = ups
        dn = jnp.where(i < nt - 1, dn_ref[...], 0.0)
        d = jnp.concatenate([d_ref[...], dn], axis=0)
        sg = _sigmoid(gt)
        dv[0:TE, :] = d * gt * sg
        dg[0:TE, :] = d * val * (sg * (1.0 + gt * (1.0 - sg)))

        @pl.when(i == 0)
        def _():
            for r in (dwv_ref, dwg_ref, dbv_ref, dbg_ref):
                r[...] = jnp.zeros_like(r)

        for dsrc, w_ref, buf, o_ref, dw_ref, db_ref in ((dv, wv_ref, bufv, ov_ref, dwv_ref, dbv_ref),
                                                      (dg, wg_ref, bufg, og_ref, dwg_ref, dbg_ref)):
            acc = jnp.zeros((T, tc), F32)
            for k in range(W):
                acc = acc + w_ref[k:k + 1, :] * dsrc[(W - 1) - k:(W - 1) - k + T, :]
            o_ref[...] = acc.astype(BF16)
            dm = dsrc[0:T, :]
            for k in range(W):
                dw_ref[k:k + 1, :] += jnp.sum(dm * buf[off + k:off + k + T, :], axis=0, keepdims=True)
            db_ref[...] += jnp.sum(dm, axis=0, keepdims=True)

    prev = lambda i: jnp.maximum(i * nh - 1, 0)
    nxt = lambda i: jnp.minimum((i + 1) * nh, S // FFN_HALO - 1)
    o_sds = _sds((S, Fh), BF16)
    return _pcall(body, name=name, grid=(nf, nt), sc=sc,
                  ins=[(up, _bs(up, (T, tc), lambda j, i, sc: (i, j), L0)),
                       (up, _bs(up, (T, tc), lambda j, i, sc: (i, j + nf), L0)),
                       (up, _bs(up, (FFN_HALO, tc), lambda j, i, sc: (prev(i), j), L0)),
                       (up, _bs(up, (FFN_HALO, tc), lambda j, i, sc: (prev(i), j + nf), L0)),
                       (up, _bs(up, (FFN_HALO, tc), lambda j, i, sc: (nxt(i), j), L0)),
                       (up, _bs(up, (FFN_HALO, tc), lambda j, i, sc: (nxt(i), j + nf), L0)),
                       (dact, _bs(dact, (T, tc), lambda j, i, sc: (i, j))),
                       (dact, _bs(dact, (FFN_HALO, tc), lambda j, i, sc: (nxt(i), j))),
                       (fw, _bs(fw, (FFN_HALO, tc), lambda j, i, sc: (0, j), L0)),
                       (fw, _bs(fw, (FFN_HALO, tc), lambda j, i, sc: (0, j + nf), L0)),
                       (fb3, _bs(fb3, (1, tc), lambda j, i, sc: (0, j), L0)),
                       (fb3, _bs(fb3, (1, tc), lambda j, i, sc: (0, j + nf), L0))],
                  outs=[(o_sds, pl.BlockSpec((T, tc), lambda j, i, sc: (i, j))),
                        (o_sds, pl.BlockSpec((T, tc), lambda j, i, sc: (i, j))),
                        (dfw_v_buf, _bs(dfw_v_buf, (FFN_HALO, tc), lambda j, i, sc: (0, j), L0)),
                        (dfw_g_buf, _bs(dfw_g_buf, (FFN_HALO, tc), lambda j, i, sc: (0, j), L0)),
                        (dfb_v_buf, _bs(dfb_v_buf, (1, tc), lambda j, i, sc: (0, j), L0)),
                        (dfb_g_buf, _bs(dfb_g_buf, (1, tc), lambda j, i, sc: (0, j), L0))],
                  scratch=[pltpu.VMEM((T + 2 * FFN_HALO, tc), F32), pltpu.VMEM((T + 2 * FFN_HALO, tc), F32),
                           pltpu.VMEM((T + 2 * FFN_HALO, tc), F32), pltpu.VMEM((T + 2 * FFN_HALO, tc), F32)],
                  sem=("parallel", "arbitrary"))


def final_loss(name, sc, h, g2, target):
    S, D = h.shape
    tr = _pick(S, 256, 8)

    def body(sc_ref, h_ref, g_ref, t_ref, loss_ref, dx_ref, dxb_ref, dg_ref):
        i = pl.program_id(0)
        x = h_ref[...]
        r = lax.rsqrt(jnp.mean(x * x, axis=-1, keepdims=True) + EPS)
        xh = x * r
        err = xh * g_ref[...] - t_ref[...]
        part_loss = 0.5 * jnp.sum(jnp.mean(err * err, axis=-1, keepdims=True), axis=0, keepdims=True)
        dy = err * (1.0 / D)
        dxh = dy * g_ref[...]
        dx = r * (dxh - xh * jnp.mean(dxh * xh, axis=-1, keepdims=True))
        dx_ref[...] = dx
        dxb_ref[...] = dx.astype(BF16)
        part_g = jnp.sum(dy * xh, axis=0, keepdims=True)

        @pl.when(i == 0)
        def _():
            loss_ref[...] = jnp.zeros_like(loss_ref) + part_loss
            dg_ref[...] = part_g

        @pl.when(i > 0)
        def _():
            loss_ref[...] += part_loss
            dg_ref[...] += part_g

    row = lambda a: (a, pl.BlockSpec((tr, D), lambda i, sc: (i, 0)))
    return _pcall(body, name=name, grid=(S // tr,), sc=sc,
                  ins=[row(h), (g2, pl.BlockSpec((1, D), lambda i, sc: (0, 0))), row(target)],
                  outs=[(_sds((8, LANES), F32), pl.BlockSpec((8, LANES), lambda i, sc: (0, 0))),
                        row(_sds((S, D), F32)), row(_sds((S, D), BF16)),
                        (_sds((1, D), F32), pl.BlockSpec((1, D), lambda i, sc: (0, 0)))])


def _split3(a):
    hi = a.astype(BF16)
    r1 = a - hi.astype(F32)
    mid = r1.astype(BF16)
    lo = (r1 - mid.astype(F32)).astype(BF16)
    return hi, mid, lo


def _rel_onehot(qi, rows_are_keys):
    shape = (BANDP, REL_PAD) if rows_are_keys else (REL_PAD, BANDP)
    km = lax.broadcasted_iota(jnp.int32, shape, 0 if rows_are_keys else 1)
    idx = lax.broadcasted_iota(jnp.int32, shape, 1 if rows_are_keys else 0)
    rel = jnp.clip(LEFT_CHUNKS * CHUNK + qi - km, -MAX_REL, MAX_REL) + MAX_REL
    return jnp.where(rel == idx, 1.0, 0.0).astype(BF16)


def bias_expand(name, sc, rel3, H):
    def body(sc_ref, rel_ref, o_ref):
        qi = pl.program_id(0)
        oh = _rel_onehot(qi, False)
        acc = jnp.zeros((H, BANDP), F32)
        for part in _split3(rel_ref[...]):
            acc = acc + jnp.dot(part, oh, preferred_element_type=F32)
        o_ref[...] = acc

    return _pcall(body, name=name, grid=(CHUNK,), sc=sc,
                  ins=[(rel3, _bs(rel3, (H, REL_PAD), lambda q, sc: (0, 0), L0))],
                  outs=[(_sds((CHUNK, H, BANDP), F32), pl.BlockSpec((None, H, BANDP), lambda q, sc: (q, 0, 0)))])[0]


def bias_reduce(name, sc, dbias_q, drel_buf, H):
    def body(sc_ref, d_ref, o_ref):
        qi = pl.program_id(0)
        oh = _rel_onehot(qi, True)
        acc = jnp.zeros((H, REL_PAD), F32)
        for part in _split3(d_ref[...]):
            acc = acc + jnp.dot(part, oh, preferred_element_type=F32)

        @pl.when(qi == 0)
        def _():
            o_ref[...] = acc

        @pl.when(qi > 0)
        def _():
            o_ref[...] += acc

    return _pcall(body, name=name, grid=(CHUNK,), sc=sc,
                  ins=[(dbias_q, pl.BlockSpec((None, H, BANDP), lambda q, sc: (q, 0, 0)))],
                  outs=[(drel_buf, _bs(drel_buf, (H, REL_PAD), lambda q, sc: (0, 0), L0))])[0]


def _attn_probs(q2, k2, bias_h, sub, c, first, scale):
    lane = lax.broadcasted_iota(jnp.int32, q2.shape, 1)
    qm = jnp.where((lane // ATTN_HEAD_DIM) == sub, q2, jnp.zeros_like(q2))
    s = lax.dot_general(qm, k2, (((1,), (1,)), ((), ())), preferred_element_type=F32) * scale + bias_h
    km = lax.broadcasted_iota(jnp.int32, s.shape, 1)
    valid = (km < BAND) & (jnp.logical_not(first) | (km + c * CHUNK >= QBLK))
    s = jnp.where(valid, s, NEG_INF)
    p = jnp.exp(s - jnp.max(s, axis=-1, keepdims=True))
    return qm, p / jnp.sum(p, axis=-1, keepdims=True)


def _fill_window(win, prev_ref, cur_ref, A):
    win[0:QBLK, :] = prev_ref[...].astype(BF16)
    win[QBLK:2 * QBLK, :] = cur_ref[...].astype(BF16)
    win[2 * QBLK:2 * QBLK + CHUNK, :] = jnp.zeros((CHUNK, A), BF16)


def attn_fwd(name, sc, proj, bias_buf, ao_buf, A, qcol):
    S = proj.shape[1]
    nb = S // QBLK
    H = A // ATTN_HEAD_DIM
    npair = A // LANES
    scale = ATTN_HEAD_DIM ** -0.5

    def body(sc_ref, q_ref, kp_ref, kc_ref, vp_ref, vc_ref, b_ref, o_ref, kw, vw):
        first = pl.program_id(0) == 0
        _fill_window(kw, kp_ref, kc_ref, A)
        _fill_window(vw, vp_ref, vc_ref, A)

        def chunk(c, carry):
            r0 = pl.multiple_of(c * CHUNK, CHUNK)
            for hp in range(npair):
                cols = slice(hp * LANES, (hp + 1) * LANES)
                q2 = q_ref[pl.ds(r0, CHUNK), cols].astype(BF16)
                k2 = kw[pl.ds(r0, BANDP), cols]
                v2 = vw[pl.ds(r0, BANDP), cols]
                lane = lax.broadcasted_iota(jnp.int32, (CHUNK, LANES), 1)
                out2 = jnp.zeros((CHUNK, LANES), F32)
                for sub in range(LANES // ATTN_HEAD_DIM):
                    _, p = _attn_probs(q2, k2, b_ref[hp * 2 + sub], sub, c, first, scale)
                    o = jnp.dot(p.astype(BF16), v2, preferred_element_type=F32)
                    out2 = jnp.where((lane // ATTN_HEAD_DIM) == sub, o, out2)
                o_ref[pl.ds(r0, CHUNK), cols] = out2.astype(BF16)
            return carry

        lax.fori_loop(0, LEFT_CHUNKS, chunk, 0, unroll=2)

    prevb = lambda i: jnp.maximum(i - 1, 0)
    blk = lambda rowf, col: (proj, _bs(proj, (QBLK, A), lambda i, sc: (rowf(i), col), L0))
    same = lambda i: i
    return _pcall(body, name=name, grid=(nb,), sc=sc,
                  ins=[blk(same, qcol), blk(prevb, qcol + 1), blk(same, qcol + 1), blk(prevb, qcol + 2), blk(same, qcol + 2),
                       (bias_buf, pl.BlockSpec((None, H, CHUNK, BANDP), lambda i, sc: (sc[0], 0, 0, 0)))],
                  outs=[(ao_buf, _bs(ao_buf, (QBLK, A), lambda i, sc: (i, 0), L0))],
                  scratch=[pltpu.VMEM((2 * QBLK + CHUNK, A), BF16), pltpu.VMEM((2 * QBLK + CHUNK, A), BF16)])


def attn_bwd(name, sc, proj, bias_buf, dao, A, qcol):
    S = proj.shape[1]
    nb = S // QBLK
    H = A // ATTN_HEAD_DIM
    npair = A // LANES
    scale = ATTN_HEAD_DIM ** -0.5
    WIN = 2 * QBLK + CHUNK

    def body(sc_ref, q_ref, kp_ref, kc_ref, vp_ref, vc_ref, b_ref, do_ref, dq_ref, dk_ref, dv_ref, db_ref, kw, vw, dkw, dvw):
        i = pl.program_id(0)
        first = i == 0

        @pl.when(first)
        def _():
            dkw[0:QBLK, :] = jnp.zeros((QBLK, A), F32)
            dvw[0:QBLK, :] = jnp.zeros((QBLK, A), F32)
            db_ref[...] = jnp.zeros_like(db_ref)

        @pl.when(i > 0)
        def _():
            dkw[0:QBLK, :] = dkw[QBLK:2 * QBLK, :]
            dvw[0:QBLK, :] = dvw[QBLK:2 * QBLK, :]

        dkw[QBLK:WIN, :] = jnp.zeros((WIN - QBLK, A), F32)
        dvw[QBLK:WIN, :] = jnp.zeros((WIN - QBLK, A), F32)

        @pl.when(i < nb)
        def _():
            _fill_window(kw, kp_ref, kc_ref, A)
            _fill_window(vw, vp_ref, vc_ref, A)

            def chunk(c, carry):
                r0 = pl.multiple_of(c * CHUNK, CHUNK)
                for hp in range(npair):
                    cols = slice(hp * LANES, (hp + 1) * LANES)
                    q2 = q_ref[pl.ds(r0, CHUNK), cols].astype(BF16)
                    k2 = kw[pl.ds(r0, BANDP), cols]
                    v2 = vw[pl.ds(r0, BANDP), cols]
                    do2 = do_ref[pl.ds(r0, CHUNK), cols]
                    lane = lax.broadcasted_iota(jnp.int32, (CHUNK, LANES), 1)
                    dq2 = jnp.zeros((CHUNK, LANES), F32)
                    dk2 = jnp.zeros((BANDP, LANES), F32)
                    dv2 = jnp.zeros((BANDP, LANES), F32)
                    for sub in range(LANES // ATTN_HEAD_DIM):
                        h = hp * 2 + sub
                        qm, p = _attn_probs(q2, k2, b_ref[h], sub, c, first, scale)
                        dom = jnp.where((lane // ATTN_HEAD_DIM) == sub, do2, jnp.zeros_like(do2))
                        dp = lax.dot_general(dom, v2, (((1,), (1,)), ((), ())), preferred_element_type=F32)
                        ds = p * (dp - jnp.sum(dp * p, axis=-1, keepdims=True))
                        db_ref[h] += ds
                        dsb = ds.astype(BF16)
                        dq = jnp.dot(dsb, k2, preferred_element_type=F32) * scale
                        dq2 = jnp.where((lane // ATTN_HEAD_DIM) == sub, dq, dq2)
                        dk2 = dk2 + lax.dot_general(dsb, qm, (((0,), (0,)), ((), ())), preferred_element_type=F32) * scale
                        dv2 = dv2 + lax.dot_general(p.astype(BF16), dom, (((0,), (0,)), ((), ())), preferred_element_type=F32)
                    dq_ref[pl.ds(r0, CHUNK), cols] = dq2.astype(BF16)
                    dkw[pl.ds(r0, BANDP), cols] += dk2
                    dvw[pl.ds(r0, BANDP), cols] += dv2
                return carry

            lax.fori_loop(0, LEFT_CHUNKS, chunk, 0, unroll=2)

        dk_ref[...] = dkw[0:QBLK, :].astype(BF16)
        dv_ref[...] = dvw[0:QBLK, :].astype(BF16)

    cur = lambda i: jnp.minimum(i, nb - 1)
    prevb = lambda i: jnp.maximum(jnp.minimum(i, nb - 1) - 1, 0)
    done = lambda i: jnp.maximum(i - 1, 0)
    blk = lambda rowf, col: (proj, _bs(proj, (QBLK, A), lambda i, sc: (rowf(i), col), L0))
    o_sds = _sds((S, A), BF16)
    return _pcall(body, name=name, grid=(nb + 1,), sc=sc,
                  ins=[blk(cur, qcol), blk(prevb, qcol + 1), blk(cur, qcol + 1), blk(prevb, qcol + 2), blk(cur, qcol + 2),
                       (bias_buf, pl.BlockSpec((None, H, CHUNK, BANDP), lambda i, sc: (sc[0], 0, 0, 0))),
                       (dao, pl.BlockSpec((QBLK, A), lambda i, sc: (cur(i), 0)))],
                  outs=[(o_sds, pl.BlockSpec((QBLK, A), lambda i, sc: (cur(i), 0))),
                        (o_sds, pl.BlockSpec((QBLK, A), lambda i, sc: (done(i), 0))),
                        (o_sds, pl.BlockSpec((QBLK, A), lambda i, sc: (done(i), 0))),
                        (_sds((H, CHUNK, BANDP), F32), pl.BlockSpec((H, CHUNK, BANDP), lambda i, sc: (0, 0, 0)))],
                  scratch=[pltpu.VMEM((WIN, A), BF16), pltpu.VMEM((WIN, A), BF16),
                           pltpu.VMEM((WIN, A), F32), pltpu.VMEM((WIN, A), F32)])


def mem_fwd(name, sc, proj, kv_buf, mo_buf, Dm, qcol):
    S = proj.shape[1]
    NM = kv_buf.shape[1]
    tr = _pick(S, 512, 8)
    hd = Dm // N_MEM_HEADS
    scale = hd ** -0.5

    def body(sc_ref, q_ref, kv_ref, o_ref):
        for h in range(N_MEM_HEADS):
            cols = slice(h * hd, (h + 1) * hd)
            q = q_ref[:, cols].astype(BF16)
            k = kv_ref[:, cols].astype(BF16)
            v = kv_ref[:, Dm + h * hd:Dm + (h + 1) * hd].astype(BF16)
            s = lax.dot_general(q, k, (((1,), (1,)), ((), ())), preferred_element_type=F32) * scale
            p = jnp.exp(s - jnp.max(s, axis=-1, keepdims=True))
            p = p / jnp.sum(p, axis=-1, keepdims=True)
            o_ref[:, cols] = jnp.dot(p.astype(BF16), v, preferred_element_type=F32).astype(BF16)

    return _pcall(body, name=name, grid=(S // tr,), sc=sc,
                  ins=[(proj, _bs(proj, (tr, Dm), lambda i, sc: (i, qcol), L0)),
                       (kv_buf, _bs(kv_buf, (NM, 2 * Dm), lambda i, sc: (0, 0), L0))],
                  outs=[(mo_buf, _bs(mo_buf, (tr, Dm), lambda i, sc: (i, 0), L0))])


def mem_bwd(name, sc, proj, kv_buf, dmo, Dm, qcol):
    S = proj.shape[1]
    NM = kv_buf.shape[1]
    tr = _pick(S, 512, 8)
    hd = Dm // N_MEM_HEADS
    scale = hd ** -0.5

    def body(sc_ref, q_ref, kv_ref, do_ref, dq_ref, dkv_ref):
        i = pl.program_id(0)

        @pl.when(i == 0)
        def _():
            dkv_ref[...] = jnp.zeros_like(dkv_ref)

        for h in range(N_MEM_HEADS):
            cols = slice(h * hd, (h + 1) * hd)
            vcols = slice(Dm + h * hd, Dm + (h + 1) * hd)
            q = q_ref[:, cols].astype(BF16)
            k = kv_ref[:, cols].astype(BF16)
            v = kv_ref[:, vcols].astype(BF16)
            do = do_ref[:, cols]
            s = lax.dot_general(q, k, (((1,), (1,)), ((), ())), preferred_element_type=F32) * scale
            p = jnp.exp(s - jnp.max(s, axis=-1, keepdims=True))
            p = p / jnp.sum(p, axis=-1, keepdims=True)
            dp = lax.dot_general(do, v, (((1,), (1,)), ((), ())), preferred_element_type=F32)
            ds = p * (dp - jnp.sum(dp * p, axis=-1, keepdims=True))
            dsb = ds.astype(BF16)
            dq_ref[:, cols] = (jnp.dot(dsb, k, preferred_element_type=F32) * scale).astype(BF16)
            dkv_ref[:, cols] += lax.dot_general(dsb, q, (((0,), (0,)), ((), ())), preferred_element_type=F32) * scale
            dkv_ref[:, vcols] += lax.dot_general(p.astype(BF16), do, (((0,), (0,)), ((), ())), preferred_element_type=F32)

    return _pcall(body, name=name, grid=(S // tr,), sc=sc,
                  ins=[(proj, _bs(proj, (tr, Dm), lambda i, sc: (i, qcol), L0)),
                       (kv_buf, _bs(kv_buf, (NM, 2 * Dm), lambda i, sc: (0, 0), L0)),
                       (dmo, pl.BlockSpec((tr, Dm), lambda i, sc: (i, 0)))],
                  outs=[(_sds((S, Dm), BF16), pl.BlockSpec((tr, Dm), lambda i, sc: (i, 0))),
                        (_sds((NM, 2 * Dm), F32), pl.BlockSpec((NM, 2 * Dm), lambda i, sc: (0, 0)))])


def _rows2d(a):
    return a.reshape(-1, a.shape[-1])


def _ew(name, fn, ins, out_dtypes):
    R, C = ins[0].shape
    tc = _pick(C, 2048)
    tr = _pick(R, max(8, (1 << 19) // tc), 8)
    n_in = len(ins)

    def body(sc_ref, *refs):
        outs = fn(*[r[...] for r in refs[:n_in]])
        for o_ref, o in zip(refs[n_in:], outs):
            o_ref[...] = o.astype(o_ref.dtype)

    plain = pl.BlockSpec((tr, tc), lambda i, j, sc: (i, j))
    return _pcall(body, name=name, grid=(R // tr, C // tc), sc=jnp.zeros((1,), jnp.int32), ins=[(a, plain) for a in ins],
                  outs=[(_sds((R, C), dt), plain) for dt in out_dtypes], sem=("parallel", "parallel"))


def _ew3(name, fn, sc, dims, ins, outs):
    G, R, C = dims
    tc = _pick(C, 2048)
    tr = _pick(R, max(16, (1 << 19) // tc), 16)
    n_in = len(ins)

    def body(sc_ref, *refs):
        res = fn(*[r[...] for r in refs[:n_in]])
        for o_ref, o in zip(refs[n_in:], res):
            o_ref[...] = o.astype(o_ref.dtype)

    def spec(arr, index):
        nlead = len(arr.shape) - 2

        def imap(g, i, j, s):
            lead, ro, co = index(g, s)
            return (*lead, i + ro // tr, j + co // tc)

        return pl.BlockSpec((None,) * nlead + (tr, tc), imap)

    return _pcall(body, name=name, grid=(G, R // tr, C // tc), sc=sc,
                  ins=[(a, spec(a, ix)) for a, ix in ins], outs=[(o, spec(o, ix)) for o, ix in outs],
                  sem=("parallel", "parallel", "parallel"))


def _at(lead_fn=None, ax=None, size=0):
    def index(g, s):
        lead = (g,) if lead_fn is None else lead_fn(g, s)
        off = s[0] * size
        return lead, (off if ax == 1 else 0), (off if ax == 2 else 0)
    return index


def _adamw_math(w, g, m, v):
    m = ADAM_B1 * m + (1.0 - ADAM_B1) * g
    v = ADAM_B2 * v + (1.0 - ADAM_B2) * (g * g)
    m_hat = m / (1.0 - ADAM_B1 ** ADAM_STEP)
    v_hat = v / (1.0 - ADAM_B2 ** ADAM_STEP)
    delta = -ADAM_LR * (m_hat / (jnp.sqrt(v_hat) + ADAM_EPS) + ADAM_WD * w)
    return delta, m, v


def adamw(name, w, g, m, v):
    shp = w.shape
    d, nm, nv = _ew(name, _adamw_math, [_rows2d(w), _rows2d(g), _rows2d(m), _rows2d(v)], (F32, F32, F32))
    return d.reshape(shp), nm.reshape(shp), nv.reshape(shp)


def _place():
    x, y, c = lax.axis_index("x"), lax.axis_index("y"), lax.axis_index("c")
    chips = [(1 - x, y), (x, 1 - y), (1 - x, 1 - y)]
    return x, y, c, chips


def _sub(ref, ax=None, s=None, n=None, half=None, lh=None):
    lay = slice(None) if half is None else pl.ds(half * lh, lh)
    if ax is None:
        return ref.at[lay]
    cut = pl.ds(pl.multiple_of(s * n, n), n)
    return ref.at[lay, cut, :] if ax == 1 else ref.at[lay, :, cut]


HBM_SPEC = pl.BlockSpec(memory_space=pl.ANY)


def gather_weights(bufs, axes):
    n = len(bufs)
    L = bufs[0].shape[0]
    lh = L // 2

    def body(*refs):
        dst = refs[n:2 * n]
        ici_send, ici_recv, d2d_send, d2d_recv = refs[2 * n:]
        x, y, c, chips = _place()
        sibling = (x, y, 1 - c)
        remote = []
        for k in range(n):
            ax = axes[k]
            ns = dst[k].shape[ax] // 4
            mine = _sub(dst[k], ax, 2 * x + y, ns, c, lh)
            for j, (px, py) in enumerate(chips):
                cp = pltpu.make_async_remote_copy(
                    src_ref=mine, dst_ref=mine, send_sem=ici_send.at[k, j], recv_sem=ici_recv.at[k, j],
                    device_id=(px, py, c), device_id_type=MESH)
                cp.start()
                remote.append(cp)
        for k in range(n):
            ax = axes[k]
            ns = dst[k].shape[ax] // 4
            for j, (px, py) in enumerate(chips):
                theirs = _sub(dst[k], ax, 2 * px + py, ns, c, lh)
                pltpu.make_async_remote_copy(
                    src_ref=theirs, dst_ref=theirs, send_sem=ici_send.at[k, j], recv_sem=ici_recv.at[k, j],
                    device_id=(px, py, c), device_id_type=MESH).wait_recv()
                fwd = pltpu.make_async_remote_copy(
                    src_ref=theirs, dst_ref=theirs, send_sem=d2d_send.at[k, j], recv_sem=d2d_recv.at[k, j],
                    device_id=sibling, device_id_type=MESH)
                fwd.start()
                remote.append(fwd)
        for k in range(n):
            ax = axes[k]
            ns = dst[k].shape[ax] // 4
            for j, (px, py) in enumerate(chips):
                got = _sub(dst[k], ax, 2 * px + py, ns, 1 - c, lh)
                pltpu.make_async_remote_copy(
                    src_ref=got, dst_ref=got, send_sem=d2d_send.at[k, j], recv_sem=d2d_recv.at[k, j],
                    device_id=sibling, device_id_type=MESH).wait_recv()
        for cp in remote:
            cp.wait_send()

    return pl.pallas_call(
        body, name="gather_weights", out_shape=[_sds(b.shape, b.dtype) for b in bufs],
        in_specs=[HBM_SPEC] * n, out_specs=[HBM_SPEC] * n, input_output_aliases={k: k for k in range(n)},
        scratch_shapes=[pltpu.SemaphoreType.DMA((n, 3)), pltpu.SemaphoreType.DMA((n, 3)),
                        pltpu.SemaphoreType.DMA((n, 3)), pltpu.SemaphoreType.DMA((n, 3))],
    )(*bufs)


def swap_halves(name, grads):
    n = len(grads)
    lh = grads[0].shape[0] // 2

    def body(*refs):
        src, dst = refs[:n], refs[n:2 * n]
        send, recv = refs[2 * n:]
        x, y, c, _ = _place()
        cps = []
        for k in range(n):
            cp = pltpu.make_async_remote_copy(
                src_ref=_sub(src[k], half=1 - c, lh=lh), dst_ref=dst[k], send_sem=send.at[k], recv_sem=recv.at[k],
                device_id=(x, y, 1 - c), device_id_type=MESH)
            cp.start()
            cps.append(cp)
        for cp in cps:
            cp.wait()

    return pl.pallas_call(
        body, name=name, out_shape=[_sds((lh,) + g.shape[1:], g.dtype) for g in grads],
        in_specs=[HBM_SPEC] * n, out_specs=[HBM_SPEC] * n,
        scratch_shapes=[pltpu.SemaphoreType.DMA((n,)), pltpu.SemaphoreType.DMA((n,))],
    )(*grads)


def exchange_chip_sums(name, sums_bf16, axes):
    n = len(sums_bf16)

    def body(*refs):
        sb, got = refs[:n], refs[n:2 * n]
        send, recv = refs[2 * n:]
        x, y, c, chips = _place()
        cps = []
        for k in range(n):
            ax = axes[k]
            ns = got[k].shape[1 + ax]
            for j, (px, py) in enumerate(chips):
                cp = pltpu.make_async_remote_copy(
                    src_ref=_sub(sb[k], ax, 2 * px + py, ns), dst_ref=got[k].at[j],
                    send_sem=send.at[k, j], recv_sem=recv.at[k, j], device_id=(px, py, c), device_id_type=MESH)
                cp.start()
                cps.append(cp)
        for cp in cps:
            cp.wait()

    out_shape = []
    for a, ax in zip(sums_bf16, axes):
        shp = list(a.shape)
        shp[ax] //= 4
        out_shape.append(_sds([3] + shp, BF16))
    return pl.pallas_call(
        body, name=name, out_shape=out_shape, in_specs=[HBM_SPEC] * n, out_specs=[HBM_SPEC] * n,
        scratch_shapes=[pltpu.SemaphoreType.DMA((n, 3)), pltpu.SemaphoreType.DMA((n, 3))],
    )(*sums_bf16)


def join_halves(name, bufs):
    n = len(bufs)
    lh = bufs[0].shape[0] // 2

    def body(*refs):
        dst = refs[n:2 * n]
        send, recv = refs[2 * n:]
        x, y, c, _ = _place()
        cps = []
        for k in range(n):
            mine = _sub(dst[k], half=c, lh=lh)
            cp = pltpu.make_async_remote_copy(
                src_ref=mine, dst_ref=mine, send_sem=send.at[k], recv_sem=recv.at[k],
                device_id=(x, y, 1 - c), device_id_type=MESH)
            cp.start()
            cps.append(cp)
        for k, cp in enumerate(cps):
            cp.wait_send()
            theirs = _sub(dst[k], half=1 - c, lh=lh)
            pltpu.make_async_remote_copy(
                src_ref=theirs, dst_ref=theirs, send_sem=send.at[k], recv_sem=recv.at[k],
                device_id=(x, y, 1 - c), device_id_type=MESH).wait_recv()

    return pl.pallas_call(
        body, name=name, out_shape=[_sds(b.shape, b.dtype) for b in bufs],
        in_specs=[HBM_SPEC] * n, out_specs=[HBM_SPEC] * n, input_output_aliases={k: k for k in range(n)},
        scratch_shapes=[pltpu.SemaphoreType.DMA((n,)), pltpu.SemaphoreType.DMA((n,))],
    )(*bufs)


def allreduce_small(name, packed):
    R = packed.shape[0]

    def body(x_ref, o_ref, buf, send, recv, loc):
        x, y, c, _ = _place()
        me = 4 * x + 2 * y + c
        cps = [pltpu.make_async_copy(x_ref, buf.at[me], loc)]
        cps[0].start()
        for d in range(1, 8):
            px, py, pc = x ^ (d >> 2), y ^ ((d >> 1) & 1), c ^ (d & 1)
            cp = pltpu.make_async_remote_copy(
                src_ref=x_ref, dst_ref=buf.at[me], send_sem=send.at[d - 1], recv_sem=recv.at[d - 1],
                device_id=(px, py, pc), device_id_type=MESH)
            cp.start()
            cps.append(cp)
        for cp in cps:
            cp.wait()
        acc = buf[0]
        for s in range(1, 8):
            acc = acc + buf[s]
        o_ref[...] = acc

    return pl.pallas_call(
        body, name=name, out_shape=_sds((R, LANES), F32),
        in_specs=[pl.BlockSpec(memory_space=pltpu.VMEM)], out_specs=pl.BlockSpec(memory_space=pltpu.VMEM),
        scratch_shapes=[pltpu.VMEM((8, R, LANES), F32), pltpu.SemaphoreType.DMA((7,)), pltpu.SemaphoreType.DMA((7,)),
                        pltpu.SemaphoreType.DMA],
        compiler_params=pltpu.CompilerParams(vmem_limit_bytes=VMEM_LIMIT),
    )(packed)


def _pack(arrs):
    flat = jnp.concatenate([a.reshape(-1) for a in arrs])
    pad = (-flat.shape[0]) % (8 * LANES)
    return jnp.pad(flat, (0, pad)).reshape(-1, LANES)


def _unpack(packed, like):
    flat = packed.reshape(-1)
    out, pos = [], 0
    for a in like:
        out.append(flat[pos:pos + a.size].reshape(a.shape))
        pos += a.size
    return out


def kernel(x, mem, mix_norm_g, mem_norm_g, w_in, gate_b, conv_w, conv_b, conv_ln_g, conv_ln_b, w_conv_out, rel_bias, w_attn_out, w_mem_kv, w_mem_out, w_o, ffn_norm_g, w_up, ffn_conv_w, ffn_conv_b, w_down, final_norm_g, loss_target, m_mix_norm_g, m_mem_norm_g, m_w_in, m_gate_b, m_conv_w, m_conv_b, m_conv_ln_g, m_conv_ln_b, m_w_conv_out, m_rel_bias, m_w_attn_out, m_w_mem_kv, m_w_mem_out, m_w_o, m_ffn_norm_g, m_w_up, m_ffn_conv_w, m_ffn_conv_b, m_w_down, m_final_norm_g, v_mix_norm_g, v_mem_norm_g, v_w_in, v_gate_b, v_conv_w, v_conv_b, v_conv_ln_g, v_conv_ln_b, v_w_conv_out, v_rel_bias, v_w_attn_out, v_w_mem_kv, v_w_mem_out, v_w_o, v_ffn_norm_g, v_w_up, v_ffn_conv_w, v_ffn_conv_b, v_w_down, v_final_norm_g):
    S, D = x.shape[1], x.shape[2]
    NM = mem.shape[1]
    L = w_in.shape[0]
    C = conv_b.shape[1]
    A = w_attn_out.shape[1]
    Dm = w_mem_out.shape[1]
    Fh = w_down.shape[1] * 4
    D_IN = w_in.shape[2] * 4
    CW = conv_w.shape[1]
    FW = ffn_conv_w.shape[1]
    H = A // ATTN_HEAD_DIM
    assert 2 * C == 2 * A == 2 * Dm == D and D_IN == 6 * D and S % QBLK == 0 and L % 2 == 0
    xi, yi, ci = lax.axis_index("x"), lax.axis_index("y"), lax.axis_index("c")
    chip = 2 * xi + yi
    place_sc = jnp.stack([chip, ci]).astype(jnp.int32)
    zero_sc = jnp.zeros((1,), jnp.int32)
    lh = L // 2

    big = [w_in, w_conv_out, w_attn_out, w_mem_kv, w_mem_out, w_o, w_up, w_down]
    big_m = [m_w_in, m_w_conv_out, m_w_attn_out, m_w_mem_kv, m_w_mem_out, m_w_o, m_w_up, m_w_down]
    big_v = [v_w_in, v_w_conv_out, v_w_attn_out, v_w_mem_kv, v_w_mem_out, v_w_o, v_w_up, v_w_down]
    big_ax = [2, 2, 2, 1, 2, 1, 2, 1]
    placed = []
    for k, (w, ax, dt) in enumerate(zip(big + [conv_w, ffn_conv_w], big_ax + [2, 2], [BF16] * len(big) + [F32, F32])):
        shp = list(w.shape)
        shp[ax] *= 4
        placed.append(_ew3(f"place_w{k}", lambda a: (a,), place_sc, w.shape, [(w, _at())],
                           [(_sds(shp, dt), _at(ax=ax, size=w.shape[ax]))])[0])
    W_in, W_co, W_ao, W_kv, W_mo, W_o, W_up, W_dn, conv_w_f, ffn_conv_w_f = gather_weights(placed, big_ax + [2, 2])

    r3 = lambda a: a.reshape(L, 1, a.shape[-1])
    mix_g3, mem_g3, ffn_g3, gb3 = r3(mix_norm_g), r3(mem_norm_g), r3(ffn_norm_g), r3(gate_b)
    cb3, lg3, lb3, fb3 = r3(conv_b), r3(conv_ln_g), r3(conv_ln_b), r3(ffn_conv_b)
    cw_p = jnp.pad(conv_w_f, ((0, 0), (0, CONV_HALO - CW), (0, 0)))
    fw_p = jnp.pad(ffn_conv_w_f, ((0, 0), (0, FFN_HALO - FW), (0, 0)))
    rel_p = jnp.pad(rel_bias, ((0, 0), (0, 0), (0, REL_PAD - rel_bias.shape[2])))

    x2, mem2, tgt2 = x[0], mem[0], loss_target[0]
    empty = lambda shape, dt: lax.empty(shape, dt)

    def layer_scalars(l):
        i32 = lambda *v: jnp.stack(v).astype(jnp.int32)
        return i32(l, 2 * l, 2 * l + 1), i32(l, 2 * l), i32(l, 2 * l + 1), i32(l, l)

    saved = dict(
        XN=empty((2 * L, S, D), BF16), HS=empty((2 * L, S, D), F32), PROJ=empty((L, S, D_IN), F32),
        HC=empty((L, S, C), F32), CACT=empty((L, S, C), BF16), AO=empty((L, S, A), BF16), MO=empty((L, S, Dm), BF16),
        Y=empty((L, 3, S, D), F32), MERGED=empty((L, S, D), BF16), UP=empty((L, S, 2 * Fh), F32),
        ACT=empty((L, S, Fh), BF16), MN=empty((L, NM, D), BF16), MEMS=empty((L, NM, D), F32),
        KV=empty((L, NM, 2 * Dm), F32), BIAS=empty((L, H, CHUNK, BANDP), F32))

    def fwd_layer(l, carry):
        h, sv = carry
        sv = dict(sv)
        sc, sc_a, sc_f, sc_m = layer_scalars(l)
        sv["XN"], sv["HS"] = rms_fwd("rms_mix", sc_a, h, mix_g3, sv["XN"], sv["HS"])
        sv["PROJ"] = mm_nn("mm_in", sc_a, sv["XN"], L1, W_in, S, D, D_IN, sv["PROJ"], L0)
        sv["HC"], sv["CACT"] = conv_fwd("conv_fwd", sc, sv["PROJ"], cw_p, cb3, lg3, lb3, sv["HC"], sv["CACT"], C, CW)
        bias_q = bias_expand("bias_expand", sc, rel_p, H)
        sv["BIAS"] = lax.dynamic_update_slice(sv["BIAS"], jnp.transpose(bias_q, (1, 0, 2))[None], (l, 0, 0, 0))
        sv["AO"], = attn_fwd("attn_fwd", sc, sv["PROJ"], sv["BIAS"], sv["AO"], A, 2)
        sv["MN"], sv["MEMS"] = rms_fwd("rms_mem", sc_m, mem2, mem_g3, sv["MN"], sv["MEMS"])
        sv["KV"] = mm_nn("mm_kv", sc, sv["MN"], L0, W_kv, NM, D, 2 * Dm, sv["KV"], L0)
        sv["MO"], = mem_fwd("mem_fwd", sc, sv["PROJ"], sv["KV"], sv["MO"], Dm, 5)
        for b, (src, w, nm) in enumerate(((sv["CACT"], W_co, "mm_co"), (sv["AO"], W_ao, "mm_ao"), (sv["MO"], W_mo, "mm_mo"))):
            sv["Y"] = mm_nn(nm, sc, src, L0, w, S, C, D, sv["Y"], lambda s, b=b: (s[0], b))
        sv["MERGED"], = gate_fwd("gate_fwd", sc, sv["PROJ"], gb3, sv["Y"], sv["MERGED"], D)
        h2 = mm_nn("mm_o", sc, sv["MERGED"], L0, W_o, S, D, D, _sds((S, D), F32), add=h)
        sv["XN"], sv["HS"] = rms_fwd("rms_ffn", sc_f, h2, ffn_g3, sv["XN"], sv["HS"])
        sv["UP"] = mm_nn("mm_up", sc_f, sv["XN"], L1, W_up, S, D, 2 * Fh, sv["UP"], L0)
        sv["ACT"], = ffn_fwd("ffn_fwd", sc, sv["UP"], fw_p, fb3, sv["ACT"], Fh, FW)
        h3 = mm_nn("mm_down", sc, sv["ACT"], L0, W_dn, S, Fh, D, _sds((S, D), F32), add=h2)
        return h3, sv

    h_last, saved = lax.fori_loop(0, L, fwd_layer, (x2, saved))
    loss_t, dh, dhb, d_final_g = final_loss("final_loss", zero_sc, h_last, final_norm_g.reshape(1, D), tgt2)
    loss = lax.psum(loss_t[0, 0], ("x", "y", "c"))

    zeros = lambda shape: jnp.zeros(shape, F32)
    grads = dict(
        w_in=empty((L, D, D_IN), F32), w_conv_out=empty((L, C, D), F32), w_attn_out=empty((L, A, D), F32),
        w_mem_kv=empty((L, D, 2 * Dm), F32), w_mem_out=empty((L, Dm, D), F32), w_o=empty((L, D, D), F32),
        w_up=empty((L, D, 2 * Fh), F32), w_down=empty((L, Fh, D), F32),
        mix_g=zeros((L, 1, D)), mem_g=zeros((L, 1, D)), ffn_g=zeros((L, 1, D)), gate_b=zeros((L, 1, 3 * D)),
        conv_w=zeros((L, CONV_HALO, C)), conv_b=zeros((L, 1, C)), ln_g=zeros((L, 1, C)), ln_b=zeros((L, 1, C)),
        rel=zeros((L, H, REL_PAD)), fw_v=zeros((L, FFN_HALO, Fh)), fw_g=zeros((L, FFN_HALO, Fh)),
        fb_v=zeros((L, 1, Fh)), fb_g=zeros((L, 1, Fh)))
    zero_mem = jnp.zeros((NM, D), F32)

    def bwd_layer(it, carry):
        dh, dhb, g = carry
        g = dict(g)
        l = L - 1 - it
        sc, sc_a, sc_f, sc_m = layer_scalars(l)
        sv = saved
        dact = mm_nt("mm_down_dx", sc, [(dhb, None)], W_dn, S, Fh, _sds((S, Fh), F32))
        g["w_down"] = mm_tn("mm_down_dw", sc, sv["ACT"], L0, dhb, None, S, Fh, D, g["w_down"], L0)
        dupv, dupg, g["fw_v"], g["fw_g"], g["fb_v"], g["fb_g"] = ffn_bwd(
            "ffn_bwd", sc, sv["UP"], dact, fw_p, fb3, g["fw_v"], g["fw_g"], g["fb_v"], g["fb_g"], Fh, FW)
        dhn = mm_nt("mm_up_dx", sc, [(dupv, None), (dupg, None)], W_up, S, D, _sds((S, D), F32))
        g["w_up"] = mm_tn("mm_up_dw_v", sc_f, sv["XN"], L1, dupv, None, S, D, Fh, g["w_up"], L0)
        g["w_up"] = mm_tn("mm_up_dw_g", sc_f, sv["XN"], L1, dupg, None, S, D, Fh, g["w_up"], L0, out_joff=Fh)
        dh2, dh2b, g["ffn_g"] = rms_bwd("rms_ffn_bwd", sc_f, sv["HS"], ffn_g3, dhn, dh, g["ffn_g"])
        dmerged = mm_nt("mm_o_dx", sc, [(dh2b, None)], W_o, S, D, _sds((S, D), F32))
        g["w_o"] = mm_tn("mm_o_dw", sc, sv["MERGED"], L0, dh2b, None, S, D, D, g["w_o"], L0)
        dy, dgates, g["gate_b"] = gate_bwd("gate_bwd", sc, sv["PROJ"], gb3, sv["Y"], dmerged, g["gate_b"], D)
        dcact = mm_nt("mm_co_dx", sc, [(dy, lambda s: (0,))], W_co, S, C, _sds((S, C), F32))
        dao = mm_nt("mm_ao_dx", sc, [(dy, lambda s: (1,))], W_ao, S, A, _sds((S, A), BF16), out_dtype=BF16)
        dmo = mm_nt("mm_mo_dx", sc, [(dy, lambda s: (2,))], W_mo, S, Dm, _sds((S, Dm), BF16), out_dtype=BF16)
        g["w_conv_out"] = mm_tn("mm_co_dw", sc, sv["CACT"], L0, dy, lambda s: (0,), S, C, D, g["w_conv_out"], L0)
        g["w_attn_out"] = mm_tn("mm_ao_dw", sc, sv["AO"], L0, dy, lambda s: (1,), S, A, D, g["w_attn_out"], L0)
        g["w_mem_out"] = mm_tn("mm_mo_dw", sc, sv["MO"], L0, dy, lambda s: (2,), S, Dm, D, g["w_mem_out"], L0)
        dhc, g["ln_g"], g["ln_b"], g["conv_b"] = conv_bwd1(
            "conv_bwd1", sc, sv["HC"], dcact, lg3, lb3, g["ln_g"], g["ln_b"], g["conv_b"])
        dconv, g["conv_w"] = conv_bwd2("conv_bwd2", sc, sv["PROJ"], dhc, cw_p, g["conv_w"], C, CW)
        dq, dk, dv, dbias = attn_bwd("attn_bwd", sc, sv["PROJ"], sv["BIAS"], dao, A, 2)
        g["rel"] = bias_reduce("bias_reduce", sc, jnp.transpose(dbias, (1, 0, 2)), g["rel"], H)
        dqm, dkv = mem_bwd("mem_bwd", sc, sv["PROJ"], sv["KV"], dmo, Dm, 5)
        g["w_mem_kv"] = mm_tn("mm_kv_dw", sc, sv["MN"], L0, dkv, None, NM, D, 2 * Dm, g["w_mem_kv"], L0)
        dmn = mm_nt("mm_kv_dx", sc, [(dkv, None)], W_kv, NM, D, _sds((NM, D), F32))
        _, _, g["mem_g"] = rms_bwd("rms_mem_bwd", sc_m, sv["MEMS"], mem_g3, dmn, zero_mem, g["mem_g"])
        pieces = [(dconv, None), (dq, None), (dk, None), (dv, None), (dqm, None), (dgates, None)]
        dxn = mm_nt("mm_in_dx", sc, pieces, W_in, S, D, _sds((S, D), F32), tk=1024)
        off = 0
        for nm, (p, _) in zip(("c", "q", "k", "v", "m", "g"), pieces):
            g["w_in"] = mm_tn("mm_in_dw_" + nm, sc_a, sv["XN"], L1, p, None, S, D, p.shape[1], g["w_in"], L0, out_joff=off)
            off += p.shape[1]
        dh0, dh0b, g["mix_g"] = rms_bwd("rms_mix_bwd", sc_a, sv["HS"], mix_g3, dxn, dh2, g["mix_g"])
        return dh0, dh0b, g

    grad_x2, _, grads = lax.fori_loop(0, L, bwd_layer, (dh, dhb, grads))

    names = ["w_in", "w_conv_out", "w_attn_out", "w_mem_kv", "w_mem_out", "w_o", "w_up", "w_down"]
    gl = [grads[nm] for nm in names]
    from_sib = swap_halves("swap_halves", gl)
    my_layer = lambda g_, s: (s[1] * lh + g_,)
    sums_f, sums_b = [], []
    for k, (g_, r_) in enumerate(zip(gl, from_sib)):
        s_f, s_b = _ew3(f"sum_sib{k}", lambda a, b: (a + b, a + b), place_sc, r_.shape,
                        [(g_, _at(my_layer)), (r_, _at())], [(_sds(r_.shape, F32), _at()), (_sds(r_.shape, BF16), _at())])
        sums_f.append(s_f)
        sums_b.append(s_b)
    got = exchange_chip_sums("exchange_chip_sums", sums_b, big_ax)
    halves = []
    for k, (s_f, r_, w, ax) in enumerate(zip(sums_f, got, big, big_ax)):
        fin, = _ew3(f"sum_chips{k}", lambda a, b0, b1, b2: (((a + b0.astype(F32)) + b1.astype(F32)) + b2.astype(F32),),
                    place_sc, (lh,) + w.shape[1:],
                    [(s_f, _at(ax=ax, size=w.shape[ax]))] + [(r_, _at(lambda g_, s, j=j: (j, g_))) for j in range(3)],
                    [(_sds(w.shape, F32), _at(my_layer))])
        halves.append(fin)
    big_g = join_halves("join_halves", halves)
    big_d, big_nm, big_nv = [], [], []
    for k in range(len(big)):
        d_, m_, v_ = adamw(f"adamw_big{k}", big[k], big_g[k], big_m[k], big_v[k])
        big_d.append(d_), big_nm.append(m_), big_nv.append(v_)

    g_small_full = [
        grads["mix_g"].reshape(L, D), grads["mem_g"].reshape(L, D), grads["gate_b"].reshape(L, 3 * D),
        grads["conv_w"][:, :CW, :], grads["conv_b"].reshape(L, C), grads["ln_g"].reshape(L, C), grads["ln_b"].reshape(L, C),
        grads["rel"][:, :, :rel_bias.shape[2]], grads["ffn_g"].reshape(L, D),
        jnp.concatenate([grads["fw_v"][:, :FW, :], grads["fw_g"][:, :FW, :]], axis=-1),
        jnp.concatenate([grads["fb_v"], grads["fb_g"]], axis=-1).reshape(L, 2 * Fh), d_final_g.reshape(D)]
    summed = _unpack(allreduce_small("allreduce_small", _pack(g_small_full)), g_small_full)
    cws, fws = conv_w.shape[2], ffn_conv_w.shape[2]
    summed[3] = lax.dynamic_slice_in_dim(summed[3], chip * cws, cws, axis=2)
    summed[9] = lax.dynamic_slice_in_dim(summed[9], chip * fws, fws, axis=2)
    small_w = [mix_norm_g, mem_norm_g, gate_b, conv_w, conv_b, conv_ln_g, conv_ln_b, rel_bias, ffn_norm_g, ffn_conv_w, ffn_conv_b, final_norm_g]
    small_m = [m_mix_norm_g, m_mem_norm_g, m_gate_b, m_conv_w, m_conv_b, m_conv_ln_g, m_conv_ln_b, m_rel_bias, m_ffn_norm_g, m_ffn_conv_w, m_ffn_conv_b, m_final_norm_g]
    small_v = [v_mix_norm_g, v_mem_norm_g, v_gate_b, v_conv_w, v_conv_b, v_conv_ln_g, v_conv_ln_b, v_rel_bias, v_ffn_norm_g, v_ffn_conv_w, v_ffn_conv_b, v_final_norm_g]
    sd, sm, sv_ = _ew("adamw_small", _adamw_math, [_pack(small_w), _pack(summed), _pack(small_m), _pack(small_v)], (F32, F32, F32))
    small_d, small_nm, small_nv = _unpack(sd, small_w), _unpack(sm, small_w), _unpack(sv_, small_w)

    order = ["mix_norm_g", "mem_norm_g", "w_in", "gate_b", "conv_w", "conv_b", "conv_ln_g", "conv_ln_b", "w_conv_out",
             "rel_bias", "w_attn_out", "w_mem_kv", "w_mem_out", "w_o", "ffn_norm_g", "w_up", "ffn_conv_w", "ffn_conv_b",
             "w_down", "final_norm_g"]
    small_names = ["mix_norm_g", "mem_norm_g", "gate_b", "conv_w", "conv_b", "conv_ln_g", "conv_ln_b", "rel_bias",
                   "ffn_norm_g", "ffn_conv_w", "ffn_conv_b", "final_norm_g"]

    def collect(bigs, smalls):
        table = dict(zip(names, bigs))
        table.update(zip(small_names, smalls))
        return [table[nm] for nm in order]

    return (loss, grad_x2[None], *collect(big_g, summed), *collect(big_d, small_d), *collect(big_nm, small_nm),
            *collect(big_nv, small_nv))
```

```python
import jax
import jax.numpy as jnp
from jax import lax
from jax.experimental import pallas as pl
from jax.experimental.pallas import tpu as pltpu

F32 = jnp.float32
BF16 = jnp.bfloat16
MESH = pl.DeviceIdType.MESH

CHUNK = 64
LEFT_CHUNKS = 8
BAND = (LEFT_CHUNKS + 1) * CHUNK
BANDP = BAND + CHUNK
MAX_REL = 256
REL_PAD = 640
ATTN_HEAD_DIM = 64
N_MEM_HEADS = 4
QBLK = LEFT_CHUNKS * CHUNK
EPS = 1e-6
NEG_INF = -1e30
LANES = 128
VMEM_LIMIT = 56 * 1024 * 1024

ADAM_LR = 0.001
ADAM_B1 = 0.9
ADAM_B2 = 0.999
ADAM_EPS = 1e-08
ADAM_WD = 0.01
ADAM_STEP = 10


def _pick(dim, target, unit=LANES):
    if dim <= target:
        return dim
    d = (target // unit) * unit
    while d >= unit:
        if dim % d == 0:
            return d
        d -= unit
    raise ValueError(f"no tile for {dim} under {target}")


def _sigmoid(x):
    return 1.0 / (1.0 + jnp.exp(-x))


def _pcall(body, *, name, grid, sc, ins, outs, scratch=(), sem=None):
    arrays = [a for a, _ in ins]
    in_specs = [s for _, s in ins]
    n_in = len(arrays)
    out_shape, out_specs, aliases = [], [], {}
    for k, (o, spec) in enumerate(outs):
        if isinstance(o, jax.ShapeDtypeStruct):
            out_shape.append(o)
        else:
            aliases[1 + len(arrays)] = k
            arrays.append(o)
            in_specs.append(pl.BlockSpec(memory_space=pl.ANY))
            out_shape.append(jax.ShapeDtypeStruct(o.shape, o.dtype))
        out_specs.append(spec)
    n_alias = len(arrays) - n_in

    def wrapped(sc_ref, *refs):
        body(sc_ref, *refs[:n_in], *refs[n_in + n_alias:])

    res = pl.pallas_call(
        wrapped,
        name=name,
        grid_spec=pltpu.PrefetchScalarGridSpec(
            num_scalar_prefetch=1, grid=grid, in_specs=in_specs, out_specs=out_specs, scratch_shapes=list(scratch)),
        out_shape=out_shape,
        input_output_aliases=aliases,
        compiler_params=pltpu.CompilerParams(
            dimension_semantics=sem or ("arbitrary",) * len(grid), vmem_limit_bytes=VMEM_LIMIT),
    )(sc, *arrays)
    return res


def _bs(arr, blk, rc, lead=None):
    nlead = len(arr.shape) - 2

    def imap(*ids):
        sc = ids[-1]
        r, c = rc(*ids)
        return (*(lead(sc) if nlead else ()), r, c)

    return pl.BlockSpec((None,) * nlead + tuple(blk), imap)


def _sds(shape, dtype):
    return jax.ShapeDtypeStruct(tuple(shape), dtype)


L0 = lambda sc: (sc[0],)
L1 = lambda sc: (sc[1],)


def _mm_core(name, grid, sc, a_items, b_item, add_item, out_item, dims, ranges, out_dtype, acc_shape):
    nk = grid[2]
    na = len(a_items)
    has_add = add_item is not None

    def body(sc_ref, *refs):
        a_refs = refs[:na]
        b_ref = refs[na]
        pos = na + 1
        add_ref = refs[pos] if has_add else None
        pos += int(has_add)
        o_ref = refs[pos]
        acc = refs[pos + 1] if nk > 1 else None
        k = pl.program_id(2)

        def contrib(ar):
            return lax.dot_general(ar[...].astype(BF16), b_ref[...].astype(BF16), dims, preferred_element_type=F32)

        def fin(val):
            if has_add:
                val = val + add_ref[...]
            o_ref[...] = val.astype(out_dtype)

        if nk == 1:
            fin(contrib(a_refs[0]))
            return
        for p, (k0, k1) in enumerate(ranges):
            @pl.when((k >= k0) & (k < k1))
            def _(p=p):
                part = contrib(a_refs[p])

                @pl.when(k == 0)
                def _():
                    acc[...] = part

                @pl.when((k > 0) & (k < nk - 1))
                def _():
                    acc[...] += part

                @pl.when(k == nk - 1)
                def _():
                    fin(acc[...] + part)

    ins = list(a_items) + [b_item] + ([add_item] if has_add else [])
    scratch = [pltpu.VMEM(acc_shape, F32)] if nk > 1 else []
    return _pcall(body, name=name, grid=grid, sc=sc, ins=ins, outs=[out_item], scratch=scratch,
                  sem=("parallel", "parallel", "arbitrary"))[0]


def mm_nn(name, sc, a, a_lead, w, M, K, N, out, out_lead=None, out_joff=0, add=None, out_dtype=F32,
          tm=1024, tn=1408, tk=2048):
    tm, tn, tk = _pick(M, tm, 8), _pick(N, tn), _pick(K, tk)
    nk = K // tk
    a_item = (a, _bs(a, (tm, tk), lambda i, j, k, sc: (i, k), a_lead))
    b_item = (w, _bs(w, (tk, tn), lambda i, j, k, sc: (k, j), L0))
    add_item = None if add is None else (add, _bs(add, (tm, tn), lambda i, j, k, sc: (i, j)))
    out_spec = _bs(out, (tm, tn), lambda i, j, k, sc: (i, j + out_joff // tn), out_lead)
    return _mm_core(name, (M // tm, N // tn, nk), sc, [a_item], b_item, add_item, (out, out_spec),
                    (((1,), (0,)), ((), ())), [(0, nk)], out_dtype, (tm, tn))


def mm_nt(name, sc, pieces, w, M, N, out, w_koff=0, out_dtype=F32, tm=1024, tn=1408, tk=2048):
    tm, tn = _pick(M, tm, 8), _pick(N, tn)
    widths = [a.shape[-1] for a, _ in pieces]
    tk = _pick(widths[0], tk)
    offs = [w_koff]
    for wd in widths:
        offs.append(offs[-1] + wd)
    while any(x % tk for x in offs):
        tk = _pick(widths[0], tk - LANES)
    ranges, a_items, k0 = [], [], 0
    for (a, lead), wd in zip(pieces, widths):
        n = wd // tk
        ranges.append((k0, k0 + n))
        a_items.append((a, _bs(a, (tm, tk), lambda i, j, k, sc, k0=k0, n=n: (i, jnp.clip(k - k0, 0, n - 1)), lead)))
        k0 += n
    nk = k0
    b_item = (w, _bs(w, (tn, tk), lambda i, j, k, sc: (j, k + w_koff // tk), L0))
    out_spec = _bs(out, (tm, tn), lambda i, j, k, sc: (i, j))
    return _mm_core(name, (M // tm, N // tn, nk), sc, a_items, b_item, None, (out, out_spec),
                    (((1,), (1,)), ((), ())), ranges, out_dtype, (tm, tn))


def mm_tn(name, sc, a, a_lead, b, b_lead, S, K, N, out, out_lead, out_joff=0, tm=1408, tn=1408, tk=2048):
    tm, tn, tk = _pick(K, tm), _pick(N, tn), _pick(S, tk, 8)
    while out_joff % tn:
        tn = _pick(N, tn - LANES)
    nk = S // tk
    a_item = (a, _bs(a, (tk, tm), lambda i, j, k, sc: (k, i), a_lead))
    b_item = (b, _bs(b, (tk, tn), lambda i, j, k, sc: (k, j), b_lead))
    out_spec = _bs(out, (tm, tn), lambda i, j, k, sc: (i, j + out_joff // tn), out_lead)
    return _mm_core(name, (K // tm, N // tn, nk), sc, [a_item], b_item, None, (out, out_spec),
                    (((0,), (0,)), ((), ())), [(0, nk)], F32, (tm, tn))


def rms_fwd(name, sc, h, g3, xn_buf, hs_buf):
    S, D = h.shape
    tr = _pick(S, 256, 8)

    def body(sc_ref, h_ref, g_ref, xn_ref, hs_ref):
        x = h_ref[...]
        r = lax.rsqrt(jnp.mean(x * x, axis=-1, keepdims=True) + EPS)
        xn_ref[...] = (x * r * g_ref[...]).astype(BF16)
        hs_ref[...] = x

    return _pcall(body, name=name, grid=(S // tr,), sc=sc,
                  ins=[(h, _bs(h, (tr, D), lambda i, sc: (i, 0))), (g3, _bs(g3, (1, D), lambda i, sc: (0, 0), L0))],
                  outs=[(xn_buf, _bs(xn_buf, (tr, D), lambda i, sc: (i, 0), L1)),
                        (hs_buf, _bs(hs_buf, (tr, D), lambda i, sc: (i, 0), L1))])


def rms_bwd(name, sc, hs_buf, g3, dy, dres, dg_buf):
    S, D = dy.shape
    tr = _pick(S, 256, 8)

    def body(sc_ref, x_ref, g_ref, dy_ref, dres_ref, dx_ref, dxb_ref, dg_ref):
        i = pl.program_id(0)
        x = x_ref[...]
        r = lax.rsqrt(jnp.mean(x * x, axis=-1, keepdims=True) + EPS)
        xh = x * r
        dyv = dy_ref[...]
        dxh = dyv * g_ref[...]
        dx = r * (dxh - xh * jnp.mean(dxh * xh, axis=-1, keepdims=True)) + dres_ref[...]
        dx_ref[...] = dx
        dxb_ref[...] = dx.astype(BF16)
        part = jnp.sum(dyv * xh, axis=0, keepdims=True)

        @pl.when(i == 0)
        def _():
            dg_ref[...] = part

        @pl.when(i > 0)
        def _():
            dg_ref[...] += part

    return _pcall(body, name=name, grid=(S // tr,), sc=sc,
                  ins=[(hs_buf, _bs(hs_buf, (tr, D), lambda i, sc: (i, 0), L1)),
                       (g3, _bs(g3, (1, D), lambda i, sc: (0, 0), L0)),
                       (dy, _bs(dy, (tr, D), lambda i, sc: (i, 0))),
                       (dres, _bs(dres, (tr, D), lambda i, sc: (i, 0)))],
                  outs=[(_sds((S, D), F32), pl.BlockSpec((tr, D), lambda i, sc: (i, 0))),
                        (_sds((S, D), BF16), pl.BlockSpec((tr, D), lambda i, sc: (i, 0))),
                        (dg_buf, _bs(dg_buf, (1, D), lambda i, sc: (0, 0), L0))])


CONV_HALO = 32
CONV_RB = 32


def conv_fwd(name, sc, proj, cw, cb, lg, lb, hc_buf, cact_buf, C, W):
    S = proj.shape[1]
    T = _pick(S, 256, CONV_HALO)
    nh = T // CONV_HALO
    off = CONV_HALO - (W - 1)

    def body(sc_ref, ua_ref, ug_ref, pa_ref, pg_ref, w_ref, b_ref, lg_ref, lb_ref, hc_ref, ca_ref, buf):
        i = pl.program_id(0)
        halo = pa_ref[...] * _sigmoid(pg_ref[...])
        buf[0:CONV_HALO, :] = jnp.where(i > 0, halo, 0.0)
        buf[CONV_HALO:CONV_HALO + T, :] = ua_ref[...] * _sigmoid(ug_ref[...])
        for r in range(0, T, CONV_RB):
            acc = jnp.zeros((CONV_RB, C), F32) + b_ref[...]
            for k in range(W):
                acc = acc + w_ref[k:k + 1, :] * buf[r + off + k:r + off + k + CONV_RB, :]
            hc_ref[r:r + CONV_RB, :] = acc
        hc = hc_ref[...]
        mu = jnp.mean(hc, axis=-1, keepdims=True)
        d = hc - mu
        rstd = lax.rsqrt(jnp.mean(d * d, axis=-1, keepdims=True) + EPS)
        y = d * rstd * lg_ref[...] + lb_ref[...]
        ca_ref[...] = (y * _sigmoid(y)).astype(BF16)

    vec = lambda a: (a, _bs(a, (1, C), lambda i, sc: (0, 0), L0))
    return _pcall(body, name=name, grid=(S // T,), sc=sc,
                  ins=[(proj, _bs(proj, (T, C), lambda i, sc: (i, 0), L0)),
                       (proj, _bs(proj, (T, C), lambda i, sc: (i, 1), L0)),
                       (proj, _bs(proj, (CONV_HALO, C), lambda i, sc: (jnp.maximum(i * nh - 1, 0), 0), L0)),
                       (proj, _bs(proj, (CONV_HALO, C), lambda i, sc: (jnp.maximum(i * nh - 1, 0), 1), L0)),
                       (cw, _bs(cw, (CONV_HALO, C), lambda i, sc: (0, 0), L0)), vec(cb), vec(lg), vec(lb)],
                  outs=[(hc_buf, _bs(hc_buf, (T, C), lambda i, sc: (i, 0), L0)),
                        (cact_buf, _bs(cact_buf, (T, C), lambda i, sc: (i, 0), L0))],
                  scratch=[pltpu.VMEM((T + CONV_HALO, C), F32)])


def conv_bwd1(name, sc, hc_buf, dcact, lg, lb, dlg_buf, dlb_buf, dcb_buf):
    S, C = dcact.shape
    tr = _pick(S, 256, 8)

    def body(sc_ref, hc_ref, dc_ref, lg_ref, lb_ref, dhc_ref, dlg_ref, dlb_ref, dcb_ref):
        i = pl.program_id(0)
        hc = hc_ref[...]
        mu = jnp.mean(hc, axis=-1, keepdims=True)
        d = hc - mu
        rstd = lax.rsqrt(jnp.mean(d * d, axis=-1, keepdims=True) + EPS)
        yh = d * rstd
        y = yh * lg_ref[...] + lb_ref[...]
        sg = _sigmoid(y)
        dy = dc_ref[...] * (sg * (1.0 + y * (1.0 - sg)))
        dyh = dy * lg_ref[...]
        dhc = rstd * (dyh - jnp.mean(dyh, axis=-1, keepdims=True) - yh * jnp.mean(dyh * yh, axis=-1, keepdims=True))
        dhc_ref[...] = dhc
        parts = (jnp.sum(dy * yh, axis=0, keepdims=True), jnp.sum(dy, axis=0, keepdims=True),
                 jnp.sum(dhc, axis=0, keepdims=True))

        @pl.when(i == 0)
        def _():
            dlg_ref[...], dlb_ref[...], dcb_ref[...] = parts

        @pl.when(i > 0)
        def _():
            dlg_ref[...] += parts[0]
            dlb_ref[...] += parts[1]
            dcb_ref[...] += parts[2]

    vec = lambda a: (a, _bs(a, (1, C), lambda i, sc: (0, 0), L0))
    return _pcall(body, name=name, grid=(S // tr,), sc=sc,
                  ins=[(hc_buf, _bs(hc_buf, (tr, C), lambda i, sc: (i, 0), L0)),
                       (dcact, _bs(dcact, (tr, C), lambda i, sc: (i, 0))), vec(lg), vec(lb)],
                  outs=[(_sds((S, C), F32), pl.BlockSpec((tr, C), lambda i, sc: (i, 0))),
                        vec(dlg_buf), vec(dlb_buf), vec(dcb_buf)])


def conv_bwd2(name, sc, proj, dhc, cw, dcw_buf, C, W):
    S = proj.shape[1]
    T = _pick(S, 256, CONV_HALO)
    nh = T // CONV_HALO
    nt = S // T
    off = CONV_HALO - (W - 1)

    def body(sc_ref, ua_ref, ug_ref, pa_ref, pg_ref, d_ref, dn_ref, w_ref, o_ref, dw_ref, buf, dbuf, dhg):
        i = pl.program_id(0)
        halo = pa_ref[...] * _sigmoid(pg_ref[...])
        buf[0:CONV_HALO, :] = jnp.where(i > 0, halo, 0.0)
        sg = _sigmoid(ug_ref[...])
        ua = ua_ref[...]
        buf[CONV_HALO:CONV_HALO + T, :] = ua * sg
        dbuf[0:T, :] = d_ref[...]
        dbuf[T:T + CONV_HALO, :] = jnp.where(i < nt - 1, dn_ref[...], 0.0)

        @pl.when(i == 0)
        def _():
            dw_ref[...] = jnp.zeros_like(dw_ref)

        for r in range(0, T, CONV_RB):
            acc = jnp.zeros((CONV_RB, C), F32)
            for k in range(W):
                acc = acc + w_ref[k:k + 1, :] * dbuf[r + (W - 1) - k:r + (W - 1) - k + CONV_RB, :]
            dhg[r:r + CONV_RB, :] = acc
        for k in range(W):
            dw_ref[k:k + 1, :] += jnp.sum(dbuf[0:T, :] * buf[off + k:off + k + T, :], axis=0, keepdims=True)
        dh = dhg[...]
        o_ref[:, 0:C] = (dh * sg).astype(BF16)
        o_ref[:, C:2 * C] = (dh * ua * sg * (1.0 - sg)).astype(BF16)

    return _pcall(body, name=name, grid=(nt,), sc=sc,
                  ins=[(proj, _bs(proj, (T, C), lambda i, sc: (i, 0), L0)),
                       (proj, _bs(proj, (T, C), lambda i, sc: (i, 1), L0)),
                       (proj, _bs(proj, (CONV_HALO, C), lambda i, sc: (jnp.maximum(i * nh - 1, 0), 0), L0)),
                       (proj, _bs(proj, (CONV_HALO, C), lambda i, sc: (jnp.maximum(i * nh - 1, 0), 1), L0)),
                       (dhc, _bs(dhc, (T, C), lambda i, sc: (i, 0))),
                       (dhc, _bs(dhc, (CONV_HALO, C), lambda i, sc: (jnp.minimum((i + 1) * nh, S // CONV_HALO - 1), 0))),
                       (cw, _bs(cw, (CONV_HALO, C), lambda i, sc: (0, 0), L0))],
                  outs=[(_sds((S, 2 * C), BF16), pl.BlockSpec((T, 2 * C), lambda i, sc: (i, 0))),
                        (dcw_buf, _bs(dcw_buf, (CONV_HALO, C), lambda i, sc: (0, 0), L0))],
                  scratch=[pltpu.VMEM((T + CONV_HALO, C), F32), pltpu.VMEM((T + CONV_HALO, C), F32),
                           pltpu.VMEM((T, C), F32)])


def gate_fwd(name, sc, proj, gb3, y_buf, merged_buf, D):
    S = proj.shape[1]
    tr, tc = _pick(S, 256, 8), _pick(D, 1024)
    nc = D // tc

    def body(sc_ref, g0, g1, g2, b0, b1, b2, y0, y1, y2, o_ref):
        acc = None
        for g, b, y in ((g0, b0, y0), (g1, b1, y1), (g2, b2, y2)):
            t = _sigmoid(g[...] + b[...]) * y[...]
            acc = t if acc is None else acc + t
        o_ref[...] = acc.astype(BF16)

    ins = [(proj, _bs(proj, (tr, tc), lambda i, j, sc, b=b: (i, (3 + b) * nc + j), L0)) for b in range(3)]
    ins += [(gb3, _bs(gb3, (1, tc), lambda i, j, sc, b=b: (0, b * nc + j), L0)) for b in range(3)]
    ins += [(y_buf, _bs(y_buf, (tr, tc), lambda i, j, sc: (i, j), lambda sc, b=b: (sc[0], b))) for b in range(3)]
    return _pcall(body, name=name, grid=(S // tr, nc), sc=sc, ins=ins,
                  outs=[(merged_buf, _bs(merged_buf, (tr, tc), lambda i, j, sc: (i, j), L0))])


def gate_bwd(name, sc, proj, gb3, y_buf, dmerged, dgb_buf, D):
    S = proj.shape[1]
    tr, tc = _pick(S, 256, 8), _pick(D, 1024)
    nc = D // tc

    def body(sc_ref, g_ref, b_ref, y_ref, dm_ref, dy_ref, dg_ref, dgb_ref):
        i = pl.program_id(2)
        g = _sigmoid(g_ref[...] + b_ref[...])
        dm = dm_ref[...]
        dy_ref[...] = (dm * g).astype(BF16)
        dgt = dm * y_ref[...] * g * (1.0 - g)
        dg_ref[...] = dgt.astype(BF16)
        part = jnp.sum(dgt, axis=0, keepdims=True)

        @pl.when(i == 0)
        def _():
            dgb_ref[...] = part

        @pl.when(i > 0)
        def _():
            dgb_ref[...] += part

    dy_sds = _sds((3, S, D), BF16)
    return _pcall(body, name=name, grid=(3, nc, S // tr), sc=sc,
                  ins=[(proj, _bs(proj, (tr, tc), lambda b, j, i, sc: (i, (3 + b) * nc + j), L0)),
                       (gb3, _bs(gb3, (1, tc), lambda b, j, i, sc: (0, b * nc + j), L0)),
                       (y_buf, pl.BlockSpec((None, None, tr, tc), lambda b, j, i, sc: (sc[0], b, i, j))),
                       (dmerged, _bs(dmerged, (tr, tc), lambda b, j, i, sc: (i, j)))],
                  outs=[(dy_sds, pl.BlockSpec((None, tr, tc), lambda b, j, i, sc: (b, i, j))),
                        (_sds((S, 3 * D), BF16), pl.BlockSpec((tr, tc), lambda b, j, i, sc: (i, b * nc + j))),
                        (dgb_buf, _bs(dgb_buf, (1, tc), lambda b, j, i, sc: (0, b * nc + j), L0))])


FFN_HALO = 8
FFN_RB = 16


def _taps(buf, w_ref, init, start, n, W, reverse=False):
    acc = init
    for k in range(W):
        s = start + ((W - 1 - k) if reverse else k)
        acc = acc + w_ref[k:k + 1, :] * buf[s:s + n, :]
    return acc


def ffn_fwd(name, sc, up, fw, fb3, act_buf, Fh, W):
    S = up.shape[1]
    T, tc = _pick(S, 512, FFN_RB), _pick(Fh, 512)
    nf = Fh // tc
    nh = T // FFN_HALO
    off = FFN_HALO - (W - 1)

    def body(sc_ref, v_ref, g_ref, pv_ref, pg_ref, wv_ref, wg_ref, bv_ref, bg_ref, o_ref, bufv, bufg):
        i = pl.program_id(1)
        for m_ref, p_ref, buf in ((v_ref, pv_ref, bufv), (g_ref, pg_ref, bufg)):
            buf[0:FFN_HALO, :] = jnp.where(i > 0, p_ref[...], 0.0)
            buf[FFN_HALO:FFN_HALO + T, :] = m_ref[...]
        for r in range(0, T, FFN_RB):
            val = _taps(bufv, wv_ref, bv_ref[...], r + off, FFN_RB, W)
            gt = _taps(bufg, wg_ref, bg_ref[...], r + off, FFN_RB, W)
            o_ref[r:r + FFN_RB, :] = (gt * _sigmoid(gt) * val).astype(BF16)

    prev = lambda i: jnp.maximum(i * nh - 1, 0)
    return _pcall(body, name=name, grid=(nf, S // T), sc=sc,
                  ins=[(up, _bs(up, (T, tc), lambda j, i, sc: (i, j), L0)),
                       (up, _bs(up, (T, tc), lambda j, i, sc: (i, j + nf), L0)),
                       (up, _bs(up, (FFN_HALO, tc), lambda j, i, sc: (prev(i), j), L0)),
                       (up, _bs(up, (FFN_HALO, tc), lambda j, i, sc: (prev(i), j + nf), L0)),
                       (fw, _bs(fw, (FFN_HALO, tc), lambda j, i, sc: (0, j), L0)),
                       (fw, _bs(fw, (FFN_HALO, tc), lambda j, i, sc: (0, j + nf), L0)),
                       (fb3, _bs(fb3, (1, tc), lambda j, i, sc: (0, j), L0)),
                       (fb3, _bs(fb3, (1, tc), lambda j, i, sc: (0, j + nf), L0))],
                  outs=[(act_buf, _bs(act_buf, (T, tc), lambda j, i, sc: (i, j), L0))],
                  scratch=[pltpu.VMEM((T + FFN_HALO, tc), F32), pltpu.VMEM((T + FFN_HALO, tc), F32)])


def ffn_bwd(name, sc, up, dact, fw, fb3, dfw_v_buf, dfw_g_buf, dfb_v_buf, dfb_g_buf, Fh, W):
    S = up.shape[1]
    T, tc = _pick(S, 512, FFN_RB), _pick(Fh, 512)
    nf = Fh // tc
    nh = T // FFN_HALO
    nt = S // T
    off = FFN_HALO - (W - 1)
    TE = T + FFN_HALO

    def body(sc_ref, v_ref, g_ref, pv_ref, pg_ref, nv_ref, ng_ref, d_ref, dn_ref, wv_ref, wg_ref, bv_ref, bg_ref,
             ov_ref, og_ref, dwv_ref, dwg_ref, dbv_ref, dbg_ref, bufv, bufg, dv, dg):
        i = pl.program_id(1)
        for m_ref, p_ref, n_ref, buf in ((v_ref, pv_ref, nv_ref, bufv), (g_ref, pg_ref, ng_ref, bufg)):
            buf[0:FFN_HALO, :] = jnp.where(i > 0, p_ref[...], 0.0)
            buf[FFN_HALO:FFN_HALO + T, :] = m_ref[...]
            buf[FFN_HALO + T:FFN_HALO + TE, :] = n_ref[...]

        def d_up(r, n, d):
            val = _taps(bufv, wv_ref, bv_ref[...], r + off, n, W)
            gt = _taps(bufg, wg_ref, bg_ref[...], r + off, n, W)
            sg = _sigmoid(gt)
            dv[r:r + n, :] = d * gt * sg
            dg[r:r + n, :] = d * val * (sg * (1.0 + gt * (1.0 - sg)))

        for r in range(0, T, FFN_RB):
            d_up(r, FFN_RB, d_ref[r:r + FFN_RB, :])
        d_up(T, FFN_HALO, jnp.where(i < nt - 1, dn_ref[...], 0.0))

        @pl.when(i == 0)
        def _():
            for r in (dwv_ref, dwg_ref, dbv_ref, dbg_ref):
                r[...] = jnp.zeros_like(r)

        fold = lambda a: a[0:8, :] + a[8:16, :]
        for dsrc, w_ref, buf, o_ref, dw_ref, db_ref in ((dv, wv_ref, bufv, ov_ref, dwv_ref, dbv_ref),
                                                      (dg, wg_ref, bufg, og_ref, dwg_ref, dbg_ref)):
            acc_w = [jnp.zeros((8, tc), F32) for _ in range(W)]
            acc_b = jnp.zeros((8, tc), F32)
            for r in range(0, T, FFN_RB):
                o_ref[r:r + FFN_RB, :] = _taps(dsrc, w_ref, 0.0, r, FFN_RB, W, reverse=True).astype(BF16)
                dm = dsrc[r:r + FFN_RB, :]
                for k in range(W):
                    acc_w[k] = acc_w[k] + fold(dm * buf[r + off + k:r + off + k + FFN_RB, :])
                acc_b = acc_b + fold(dm)
            for k in range(W):
                dw_ref[k:k + 1, :] += jnp.sum(acc_w[k], axis=0, keepdims=True)
            db_ref[...] += jnp.sum(acc_b, axis=0, keepdims=True)

    prev = lambda i: jnp.maximum(i * nh - 1, 0)
    nxt = lambda i: jnp.minimum((i + 1) * nh, S // FFN_HALO - 1)
    o_sds = _sds((S, Fh), BF16)
    return _pcall(body, name=name, grid=(nf, nt), sc=sc,
                  ins=[(up, _bs(up, (T, tc), lambda j, i, sc: (i, j), L0)),
                       (up, _bs(up, (T, tc), lambda j, i, sc: (i, j + nf), L0)),
                       (up, _bs(up, (FFN_HALO, tc), lambda j, i, sc: (prev(i), j), L0)),
                       (up, _bs(up, (FFN_HALO, tc), lambda j, i, sc: (prev(i), j + nf), L0)),
                       (up, _bs(up, (FFN_HALO, tc), lambda j, i, sc: (nxt(i), j), L0)),
                       (up, _bs(up, (FFN_HALO, tc), lambda j, i, sc: (nxt(i), j + nf), L0)),
                       (dact, _bs(dact, (T, tc), lambda j, i, sc: (i, j))),
                       (dact, _bs(dact, (FFN_HALO, tc), lambda j, i, sc: (nxt(i), j))),
                       (fw, _bs(fw, (FFN_HALO, tc), lambda j, i, sc: (0, j), L0)),
                       (fw, _bs(fw, (FFN_HALO, tc), lambda j, i, sc: (0, j + nf), L0)),
                       (fb3, _bs(fb3, (1, tc), lambda j, i, sc: (0, j), L0)),
                       (fb3, _bs(fb3, (1, tc), lambda j, i, sc: (0, j + nf), L0))],
                  outs=[(o_sds, pl.BlockSpec((T, tc), lambda j, i, sc: (i, j))),
                        (o_sds, pl.BlockSpec((T, tc), lambda j, i, sc: (i, j))),
                        (dfw_v_buf, _bs(dfw_v_buf, (FFN_HALO, tc), lambda j, i, sc: (0, j), L0)),
                        (dfw_g_buf, _bs(dfw_g_buf, (FFN_HALO, tc), lambda j, i, sc: (0, j), L0)),
                        (dfb_v_buf, _bs(dfb_v_buf, (1, tc), lambda j, i, sc: (0, j), L0)),
                        (dfb_g_buf, _bs(dfb_g_buf, (1, tc), lambda j, i, sc: (0, j), L0))],
                  scratch=[pltpu.VMEM((T + 2 * FFN_HALO, tc), F32), pltpu.VMEM((T + 2 * FFN_HALO, tc), F32),
                           pltpu.VMEM((T + 2 * FFN_HALO, tc), F32), pltpu.VMEM((T + 2 * FFN_HALO, tc), F32)],
                  sem=("parallel", "arbitrary"))


def final_loss(name, sc, h, g2, target):
    S, D = h.shape
    tr = _pick(S, 256, 8)

    def body(sc_ref, h_ref, g_ref, t_ref, loss_ref, dx_ref, dxb_ref, dg_ref):
        i = pl.program_id(0)
        x = h_ref[...]
        r = lax.rsqrt(jnp.mean(x * x, axis=-1, keepdims=True) + EPS)
        xh = x * r
        err = xh * g_ref[...] - t_ref[...]
        part_loss = 0.5 * jnp.sum(jnp.mean(err * err, axis=-1, keepdims=True), axis=0, keepdims=True)
        dy = err * (1.0 / D)
        dxh = dy * g_ref[...]
        dx = r * (dxh - xh * jnp.mean(dxh * xh, axis=-1, keepdims=True))
        dx_ref[...] = dx
        dxb_ref[...] = dx.astype(BF16)
        part_g = jnp.sum(dy * xh, axis=0, keepdims=True)

        @pl.when(i == 0)
        def _():
            loss_ref[...] = jnp.zeros_like(loss_ref) + part_loss
            dg_ref[...] = part_g

        @pl.when(i > 0)
        def _():
            loss_ref[...] += part_loss
            dg_ref[...] += part_g

    row = lambda a: (a, pl.BlockSpec((tr, D), lambda i, sc: (i, 0)))
    return _pcall(body, name=name, grid=(S // tr,), sc=sc,
                  ins=[row(h), (g2, pl.BlockSpec((1, D), lambda i, sc: (0, 0))), row(target)],
                  outs=[(_sds((8, LANES), F32), pl.BlockSpec((8, LANES), lambda i, sc: (0, 0))),
                        row(_sds((S, D), F32)), row(_sds((S, D), BF16)),
                        (_sds((1, D), F32), pl.BlockSpec((1, D), lambda i, sc: (0, 0)))])


def _split3(a):
    hi = a.astype(BF16)
    r1 = a - hi.astype(F32)
    mid = r1.astype(BF16)
    lo = (r1 - mid.astype(F32)).astype(BF16)
    return hi, mid, lo


def _rel_onehot(qi, rows_are_keys):
    shape = (BANDP, REL_PAD) if rows_are_keys else (REL_PAD, BANDP)
    km = lax.broadcasted_iota(jnp.int32, shape, 0 if rows_are_keys else 1)
    idx = lax.broadcasted_iota(jnp.int32, shape, 1 if rows_are_keys else 0)
    rel = jnp.clip(LEFT_CHUNKS * CHUNK + qi - km, -MAX_REL, MAX_REL) + MAX_REL
    return jnp.where(rel == idx, 1.0, 0.0).astype(BF16)


def bias_expand(name, sc, rel3, H):
    def body(sc_ref, rel_ref, o_ref):
        qi = pl.program_id(0)
        oh = _rel_onehot(qi, False)
        acc = jnp.zeros((H, BANDP), F32)
        for part in _split3(rel_ref[...]):
            acc = acc + jnp.dot(part, oh, preferred_element_type=F32)
        o_ref[...] = acc

    return _pcall(body, name=name, grid=(CHUNK,), sc=sc,
                  ins=[(rel3, _bs(rel3, (H, REL_PAD), lambda q, sc: (0, 0), L0))],
                  outs=[(_sds((CHUNK, H, BANDP), F32), pl.BlockSpec((None, H, BANDP), lambda q, sc: (q, 0, 0)))])[0]


def bias_reduce(name, sc, dbias_q, drel_buf, H):
    def body(sc_ref, d_ref, o_ref):
        qi = pl.program_id(0)
        oh = _rel_onehot(qi, True)
        acc = jnp.zeros((H, REL_PAD), F32)
        for part in _split3(d_ref[...]):
            acc = acc + jnp.dot(part, oh, preferred_element_type=F32)

        @pl.when(qi == 0)
        def _():
            o_ref[...] = acc

        @pl.when(qi > 0)
        def _():
            o_ref[...] += acc

    return _pcall(body, name=name, grid=(CHUNK,), sc=sc,
                  ins=[(dbias_q, pl.BlockSpec((None, H, BANDP), lambda q, sc: (q, 0, 0)))],
                  outs=[(drel_buf, _bs(drel_buf, (H, REL_PAD), lambda q, sc: (0, 0), L0))])[0]


def _head_blocks(a2):
    lo = lax.broadcasted_iota(jnp.int32, a2.shape, 1) < ATTN_HEAD_DIM
    zero = jnp.zeros_like(a2)
    return jnp.concatenate([jnp.where(lo, a2, zero), jnp.where(lo, zero, a2)], axis=0)


def _head_diag(r):
    lo = lax.broadcasted_iota(jnp.int32, (CHUNK, LANES), 1) < ATTN_HEAD_DIM
    return jnp.where(lo, r[0:CHUNK], r[CHUNK:2 * CHUNK])


def _attn_probs_t(q2, k2, bias_t, c, first, scale):
    qbd = _head_blocks(q2)
    s = lax.dot_general(k2, qbd, (((1,), (1,)), ((), ())), preferred_element_type=F32) * scale + bias_t
    km = lax.broadcasted_iota(jnp.int32, s.shape, 0)
    valid = (km < BAND) & (jnp.logical_not(first) | (km + c * CHUNK >= QBLK))
    s = jnp.where(valid, s, NEG_INF)
    p = jnp.exp(s - jnp.max(s, axis=0, keepdims=True))
    return qbd, p / jnp.sum(p, axis=0, keepdims=True)


def bias_to_lanes(bias_q, npair):
    t = jnp.transpose(bias_q, (1, 2, 0)).reshape(npair, 2, BANDP, CHUNK)
    return jnp.transpose(t, (0, 2, 1, 3)).reshape(npair, BANDP, 2 * CHUNK)


def bias_from_lanes(bias_t, npair):
    t = jnp.transpose(bias_t.reshape(npair, BANDP, 2, CHUNK), (3, 0, 2, 1))
    return t.reshape(CHUNK, 2 * npair, BANDP)


def _fill_window(win, prev_ref, cur_ref, A):
    win[0:QBLK, :] = prev_ref[...].astype(BF16)
    win[QBLK:2 * QBLK, :] = cur_ref[...].astype(BF16)
    win[2 * QBLK:2 * QBLK + CHUNK, :] = jnp.zeros((CHUNK, A), BF16)


def attn_fwd(name, sc, proj, bias_buf, ao_buf, A, qcol):
    S = proj.shape[1]
    nb = S // QBLK
    H = A // ATTN_HEAD_DIM
    npair = A // LANES
    scale = ATTN_HEAD_DIM ** -0.5

    def body(sc_ref, q_ref, kp_ref, kc_ref, vp_ref, vc_ref, b_ref, o_ref, kw, vw):
        first = pl.program_id(0) == 0
        _fill_window(kw, kp_ref, kc_ref, A)
        _fill_window(vw, vp_ref, vc_ref, A)

        def chunk(c, carry):
            r0 = pl.multiple_of(c * CHUNK, CHUNK)
            for hp in range(npair):
                cols = slice(hp * LANES, (hp + 1) * LANES)
                q2 = q_ref[pl.ds(r0, CHUNK), cols].astype(BF16)
                k2 = kw[pl.ds(r0, BANDP), cols]
                v2 = vw[pl.ds(r0, BANDP), cols]
                _, p = _attn_probs_t(q2, k2, b_ref[hp], c, first, scale)
                o = lax.dot_general(p.astype(BF16), v2, (((0,), (0,)), ((), ())), preferred_element_type=F32)
                o_ref[pl.ds(r0, CHUNK), cols] = _head_diag(o).astype(BF16)
            return carry

        lax.fori_loop(0, LEFT_CHUNKS, chunk, 0, unroll=2)

    prevb = lambda i: jnp.maximum(i - 1, 0)
    blk = lambda rowf, col: (proj, _bs(proj, (QBLK, A), lambda i, sc: (rowf(i), col), L0))
    same = lambda i: i
    return _pcall(body, name=name, grid=(nb,), sc=sc,
                  ins=[blk(same, qcol), blk(prevb, qcol + 1), blk(same, qcol + 1), blk(prevb, qcol + 2), blk(same, qcol + 2),
                       (bias_buf, pl.BlockSpec((None, npair, BANDP, LANES), lambda i, sc: (sc[0], 0, 0, 0)))],
                  outs=[(ao_buf, _bs(ao_buf, (QBLK, A), lambda i, sc: (i, 0), L0))],
                  scratch=[pltpu.VMEM((2 * QBLK + CHUNK, A), BF16), pltpu.VMEM((2 * QBLK + CHUNK, A), BF16)])


def attn_bwd(name, sc, proj, bias_buf, dao, A, qcol):
    S = proj.shape[1]
    nb = S // QBLK
    H = A // ATTN_HEAD_DIM
    npair = A // LANES
    scale = ATTN_HEAD_DIM ** -0.5
    WIN = 2 * QBLK + CHUNK

    def body(sc_ref, q_ref, kp_ref, kc_ref, vp_ref, vc_ref, b_ref, do_ref, dq_ref, dk_ref, dv_ref, db_ref, kw, vw, dkw, dvw):
        i = pl.program_id(0)
        first = i == 0

        @pl.when(first)
        def _():
            dkw[0:QBLK, :] = jnp.zeros((QBLK, A), F32)
            dvw[0:QBLK, :] = jnp.zeros((QBLK, A), F32)
            db_ref[...] = jnp.zeros_like(db_ref)

        @pl.when(i > 0)
        def _():
            dkw[0:QBLK, :] = dkw[QBLK:2 * QBLK, :]
            dvw[0:QBLK, :] = dvw[QBLK:2 * QBLK, :]

        dkw[QBLK:WIN, :] = jnp.zeros((WIN - QBLK, A), F32)
        dvw[QBLK:WIN, :] = jnp.zeros((WIN - QBLK, A), F32)

        @pl.when(i < nb)
        def _():
            _fill_window(kw, kp_ref, kc_ref, A)
            _fill_window(vw, vp_ref, vc_ref, A)

            def chunk(c, carry):
                r0 = pl.multiple_of(c * CHUNK, CHUNK)
                for hp in range(npair):
                    cols = slice(hp * LANES, (hp + 1) * LANES)
                    q2 = q_ref[pl.ds(r0, CHUNK), cols].astype(BF16)
                    k2 = kw[pl.ds(r0, BANDP), cols]
                    v2 = vw[pl.ds(r0, BANDP), cols]
                    dobd = _head_blocks(do_ref[pl.ds(r0, CHUNK), cols])
                    qbd, p = _attn_probs_t(q2, k2, b_ref[hp], c, first, scale)
                    dp = lax.dot_general(v2, dobd, (((1,), (1,)), ((), ())), preferred_element_type=F32)
                    ds = p * (dp - jnp.sum(dp * p, axis=0, keepdims=True))
                    db_ref[hp] += ds
                    dsb = ds.astype(BF16)
                    dq = lax.dot_general(dsb, k2, (((0,), (0,)), ((), ())), preferred_element_type=F32) * scale
                    dq_ref[pl.ds(r0, CHUNK), cols] = _head_diag(dq).astype(BF16)
                    dkw[pl.ds(r0, BANDP), cols] += jnp.dot(dsb, qbd, preferred_element_type=F32) * scale
                    dvw[pl.ds(r0, BANDP), cols] += jnp.dot(p.astype(BF16), dobd, preferred_element_type=F32)
                return carry

            lax.fori_loop(0, LEFT_CHUNKS, chunk, 0, unroll=2)

        dk_ref[...] = dkw[0:QBLK, :].astype(BF16)
        dv_ref[...] = dvw[0:QBLK, :].astype(BF16)

    cur = lambda i: jnp.minimum(i, nb - 1)
    prevb = lambda i: jnp.maximum(jnp.minimum(i, nb - 1) - 1, 0)
    done = lambda i: jnp.maximum(i - 1, 0)
    blk = lambda rowf, col: (proj, _bs(proj, (QBLK, A), lambda i, sc: (rowf(i), col), L0))
    o_sds = _sds((S, A), BF16)
    return _pcall(body, name=name, grid=(nb + 1,), sc=sc,
                  ins=[blk(cur, qcol), blk(prevb, qcol + 1), blk(cur, qcol + 1), blk(prevb, qcol + 2), blk(cur, qcol + 2),
                       (bias_buf, pl.BlockSpec((None, npair, BANDP, LANES), lambda i, sc: (sc[0], 0, 0, 0))),
                       (dao, pl.BlockSpec((QBLK, A), lambda i, sc: (cur(i), 0)))],
                  outs=[(o_sds, pl.BlockSpec((QBLK, A), lambda i, sc: (cur(i), 0))),
                        (o_sds, pl.BlockSpec((QBLK, A), lambda i, sc: (done(i), 0))),
                        (o_sds, pl.BlockSpec((QBLK, A), lambda i, sc: (done(i), 0))),
                        (_sds((npair, BANDP, LANES), F32), pl.BlockSpec((npair, BANDP, LANES), lambda i, sc: (0, 0, 0)))],
                  scratch=[pltpu.VMEM((WIN, A), BF16), pltpu.VMEM((WIN, A), BF16),
                           pltpu.VMEM((WIN, A), F32), pltpu.VMEM((WIN, A), F32)])


def mem_fwd(name, sc, proj, kv_buf, mo_buf, Dm, qcol):
    S = proj.shape[1]
    NM = kv_buf.shape[1]
    tr = _pick(S, 512, 8)
    hd = Dm // N_MEM_HEADS
    scale = hd ** -0.5

    def body(sc_ref, q_ref, kv_ref, o_ref):
        for h in range(N_MEM_HEADS):
            cols = slice(h * hd, (h + 1) * hd)
            q = q_ref[:, cols].astype(BF16)
            k = kv_ref[:, cols].astype(BF16)
            v = kv_ref[:, Dm + h * hd:Dm + (h + 1) * hd].astype(BF16)
            s = lax.dot_general(q, k, (((1,), (1,)), ((), ())), preferred_element_type=F32) * scale
            p = jnp.exp(s - jnp.max(s, axis=-1, keepdims=True))
            p = p / jnp.sum(p, axis=-1, keepdims=True)
            o_ref[:, cols] = jnp.dot(p.astype(BF16), v, preferred_element_type=F32).astype(BF16)

    return _pcall(body, name=name, grid=(S // tr,), sc=sc,
                  ins=[(proj, _bs(proj, (tr, Dm), lambda i, sc: (i, qcol), L0)),
                       (kv_buf, _bs(kv_buf, (NM, 2 * Dm), lambda i, sc: (0, 0), L0))],
                  outs=[(mo_buf, _bs(mo_buf, (tr, Dm), lambda i, sc: (i, 0), L0))])


def mem_bwd(name, sc, proj, kv_buf, dmo, Dm, qcol):
    S = proj.shape[1]
    NM = kv_buf.shape[1]
    tr = _pick(S, 512, 8)
    hd = Dm // N_MEM_HEADS
    scale = hd ** -0.5

    def body(sc_ref, q_ref, kv_ref, do_ref, dq_ref, dkv_ref):
        i = pl.program_id(0)

        @pl.when(i == 0)
        def _():
            dkv_ref[...] = jnp.zeros_like(dkv_ref)

        for h in range(N_MEM_HEADS):
            cols = slice(h * hd, (h + 1) * hd)
            vcols = slice(Dm + h * hd, Dm + (h + 1) * hd)
            q = q_ref[:, cols].astype(BF16)
            k = kv_ref[:, cols].astype(BF16)
            v = kv_ref[:, vcols].astype(BF16)
            do = do_ref[:, cols]
            s = lax.dot_general(q, k, (((1,), (1,)), ((), ())), preferred_element_type=F32) * scale
            p = jnp.exp(s - jnp.max(s, axis=-1, keepdims=True))
            p = p / jnp.sum(p, axis=-1, keepdims=True)
            dp = lax.dot_general(do, v, (((1,), (1,)), ((), ())), preferred_element_type=F32)
            ds = p * (dp - jnp.sum(dp * p, axis=-1, keepdims=True))
            dsb = ds.astype(BF16)
            dq_ref[:, cols] = (jnp.dot(dsb, k, preferred_element_type=F32) * scale).astype(BF16)
            dkv_ref[:, cols] += lax.dot_general(dsb, q, (((0,), (0,)), ((), ())), preferred_element_type=F32) * scale
            dkv_ref[:, vcols] += lax.dot_general(p.astype(BF16), do, (((0,), (0,)), ((), ())), preferred_element_type=F32)

    return _pcall(body, name=name, grid=(S // tr,), sc=sc,
                  ins=[(proj, _bs(proj, (tr, Dm), lambda i, sc: (i, qcol), L0)),
                       (kv_buf, _bs(kv_buf, (NM, 2 * Dm), lambda i, sc: (0, 0), L0)),
                       (dmo, pl.BlockSpec((tr, Dm), lambda i, sc: (i, 0)))],
                  outs=[(_sds((S, Dm), BF16), pl.BlockSpec((tr, Dm), lambda i, sc: (i, 0))),
                        (_sds((NM, 2 * Dm), F32), pl.BlockSpec((NM, 2 * Dm), lambda i, sc: (0, 0)))])


def _rows2d(a):
    return a.reshape(-1, a.shape[-1])


def _ew(name, fn, ins, out_dtypes):
    R, C = ins[0].shape
    tc = _pick(C, 2048)
    tr = _pick(R, max(8, (1 << 19) // tc), 8)
    n_in = len(ins)

    def body(sc_ref, *refs):
        outs = fn(*[r[...] for r in refs[:n_in]])
        for o_ref, o in zip(refs[n_in:], outs):
            o_ref[...] = o.astype(o_ref.dtype)

    plain = pl.BlockSpec((tr, tc), lambda i, j, sc: (i, j))
    return _pcall(body, name=name, grid=(R // tr, C // tc), sc=jnp.zeros((1,), jnp.int32), ins=[(a, plain) for a in ins],
                  outs=[(_sds((R, C), dt), plain) for dt in out_dtypes], sem=("parallel", "parallel"))


def _ew3(name, fn, sc, dims, ins, outs):
    G, R, C = dims
    tc = _pick(C, 2048)
    tr = _pick(R, max(16, (1 << 19) // tc), 16)
    n_in = len(ins)

    def body(sc_ref, *refs):
        res = fn(*[r[...] for r in refs[:n_in]])
        for o_ref, o in zip(refs[n_in:], res):
            o_ref[...] = o.astype(o_ref.dtype)

    def spec(arr, index):
        nlead = len(arr.shape) - 2

        def imap(g, i, j, s):
            lead, ro, co = index(g, s)
            return (*lead, i + ro // tr, j + co // tc)

        return pl.BlockSpec((None,) * nlead + (tr, tc), imap)

    return _pcall(body, name=name, grid=(G, R // tr, C // tc), sc=sc,
                  ins=[(a, spec(a, ix)) for a, ix in ins], outs=[(o, spec(o, ix)) for o, ix in outs],
                  sem=("parallel", "parallel", "parallel"))


def _at(lead_fn=None, ax=None, size=0):
    def index(g, s):
        lead = (g,) if lead_fn is None else lead_fn(g, s)
        off = s[0] * size
        return lead, (off if ax == 1 else 0), (off if ax == 2 else 0)
    return index


def _adamw_math(w, g, m, v):
    m = ADAM_B1 * m + (1.0 - ADAM_B1) * g
    v = ADAM_B2 * v + (1.0 - ADAM_B2) * (g * g)
    m_hat = m / (1.0 - ADAM_B1 ** ADAM_STEP)
    v_hat = v / (1.0 - ADAM_B2 ** ADAM_STEP)
    delta = -ADAM_LR * (m_hat / (jnp.sqrt(v_hat) + ADAM_EPS) + ADAM_WD * w)
    return delta, m, v


def adamw(name, w, g, m, v):
    shp = w.shape
    d, nm, nv = _ew(name, _adamw_math, [_rows2d(w), _rows2d(g), _rows2d(m), _rows2d(v)], (F32, F32, F32))
    return d.reshape(shp), nm.reshape(shp), nv.reshape(shp)


def _place():
    x, y, c = lax.axis_index("x"), lax.axis_index("y"), lax.axis_index("c")
    chips = [(1 - x, y), (x, 1 - y), (1 - x, 1 - y)]
    return x, y, c, chips


def _sub(ref, ax=None, s=None, n=None, half=None, lh=None):
    lay = slice(None) if half is None else pl.ds(half * lh, lh)
    if ax is None:
        return ref.at[lay]
    cut = pl.ds(pl.multiple_of(s * n, n), n)
    return ref.at[lay, cut, :] if ax == 1 else ref.at[lay, :, cut]


HBM_SPEC = pl.BlockSpec(memory_space=pl.ANY)


def gather_weights(bufs, axes):
    n = len(bufs)
    L = bufs[0].shape[0]
    lh = L // 2

    def body(*refs):
        dst = refs[n:2 * n]
        ici_send, ici_recv, d2d_send, d2d_recv = refs[2 * n:]
        x, y, c, chips = _place()
        sibling = (x, y, 1 - c)
        remote = []
        for k in range(n):
            ax = axes[k]
            ns = dst[k].shape[ax] // 4
            mine = _sub(dst[k], ax, 2 * x + y, ns, c, lh)
            for j, (px, py) in enumerate(chips):
                cp = pltpu.make_async_remote_copy(
                    src_ref=mine, dst_ref=mine, send_sem=ici_send.at[k, j], recv_sem=ici_recv.at[k, j],
                    device_id=(px, py, c), device_id_type=MESH)
                cp.start()
                remote.append(cp)
        for k in range(n):
            ax = axes[k]
            ns = dst[k].shape[ax] // 4
            for j, (px, py) in enumerate(chips):
                theirs = _sub(dst[k], ax, 2 * px + py, ns, c, lh)
                pltpu.make_async_remote_copy(
                    src_ref=theirs, dst_ref=theirs, send_sem=ici_send.at[k, j], recv_sem=ici_recv.at[k, j],
                    device_id=(px, py, c), device_id_type=MESH).wait_recv()
                fwd = pltpu.make_async_remote_copy(
                    src_ref=theirs, dst_ref=theirs, send_sem=d2d_send.at[k, j], recv_sem=d2d_recv.at[k, j],
                    device_id=sibling, device_id_type=MESH)
                fwd.start()
                remote.append(fwd)
        for k in range(n):
            ax = axes[k]
            ns = dst[k].shape[ax] // 4
            for j, (px, py) in enumerate(chips):
                got = _sub(dst[k], ax, 2 * px + py, ns, 1 - c, lh)
                pltpu.make_async_remote_copy(
                    src_ref=got, dst_ref=got, send_sem=d2d_send.at[k, j], recv_sem=d2d_recv.at[k, j],
                    device_id=sibling, device_id_type=MESH).wait_recv()
        for cp in remote:
            cp.wait_send()

    return pl.pallas_call(
        body, name="gather_weights", out_shape=[_sds(b.shape, b.dtype) for b in bufs],
        in_specs=[HBM_SPEC] * n, out_specs=[HBM_SPEC] * n, input_output_aliases={k: k for k in range(n)},
        scratch_shapes=[pltpu.SemaphoreType.DMA((n, 3)), pltpu.SemaphoreType.DMA((n, 3)),
                        pltpu.SemaphoreType.DMA((n, 3)), pltpu.SemaphoreType.DMA((n, 3))],
    )(*bufs)


def swap_halves(name, grads):
    n = len(grads)
    lh = grads[0].shape[0] // 2

    def body(*refs):
        src, dst = refs[:n], refs[n:2 * n]
        send, recv = refs[2 * n:]
        x, y, c, _ = _place()
        cps = []
        for k in range(n):
            cp = pltpu.make_async_remote_copy(
                src_ref=_sub(src[k], half=1 - c, lh=lh), dst_ref=dst[k], send_sem=send.at[k], recv_sem=recv.at[k],
                device_id=(x, y, 1 - c), device_id_type=MESH)
            cp.start()
            cps.append(cp)
        for cp in cps:
            cp.wait()

    return pl.pallas_call(
        body, name=name, out_shape=[_sds((lh,) + g.shape[1:], g.dtype) for g in grads],
        in_specs=[HBM_SPEC] * n, out_specs=[HBM_SPEC] * n,
        scratch_shapes=[pltpu.SemaphoreType.DMA((n,)), pltpu.SemaphoreType.DMA((n,))],
    )(*grads)


def exchange_chip_sums(name, sums_bf16, axes):
    n = len(sums_bf16)

    def body(*refs):
        sb, got = refs[:n], refs[n:2 * n]
        send, recv = refs[2 * n:]
        x, y, c, chips = _place()
        cps = []
        for k in range(n):
            ax = axes[k]
            ns = got[k].shape[1 + ax]
            for j, (px, py) in enumerate(chips):
                cp = pltpu.make_async_remote_copy(
                    src_ref=_sub(sb[k], ax, 2 * px + py, ns), dst_ref=got[k].at[j],
                    send_sem=send.at[k, j], recv_sem=recv.at[k, j], device_id=(px, py, c), device_id_type=MESH)
                cp.start()
                cps.append(cp)
        for cp in cps:
            cp.wait()

    out_shape = []
    for a, ax in zip(sums_bf16, axes):
        shp = list(a.shape)
        shp[ax] //= 4
        out_shape.append(_sds([3] + shp, BF16))
    return pl.pallas_call(
        body, name=name, out_shape=out_shape, in_specs=[HBM_SPEC] * n, out_specs=[HBM_SPEC] * n,
        scratch_shapes=[pltpu.SemaphoreType.DMA((n, 3)), pltpu.SemaphoreType.DMA((n, 3))],
    )(*sums_bf16)


def join_halves(name, bufs):
    n = len(bufs)
    lh = bufs[0].shape[0] // 2

    def body(*refs):
        dst = refs[n:2 * n]
        send, recv = refs[2 * n:]
        x, y, c, _ = _place()
        cps = []
        for k in range(n):
            mine = _sub(dst[k], half=c, lh=lh)
            cp = pltpu.make_async_remote_copy(
                src_ref=mine, dst_ref=mine, send_sem=send.at[k], recv_sem=recv.at[k],
                device_id=(x, y, 1 - c), device_id_type=MESH)
            cp.start()
            cps.append(cp)
        for k, cp in enumerate(cps):
            cp.wait_send()
            theirs = _sub(dst[k], half=1 - c, lh=lh)
            pltpu.make_async_remote_copy(
                src_ref=theirs, dst_ref=theirs, send_sem=send.at[k], recv_sem=recv.at[k],
                device_id=(x, y, 1 - c), device_id_type=MESH).wait_recv()

    return pl.pallas_call(
        body, name=name, out_shape=[_sds(b.shape, b.dtype) for b in bufs],
        in_specs=[HBM_SPEC] * n, out_specs=[HBM_SPEC] * n, input_output_aliases={k: k for k in range(n)},
        scratch_shapes=[pltpu.SemaphoreType.DMA((n,)), pltpu.SemaphoreType.DMA((n,))],
    )(*bufs)


def allreduce_small(name, packed):
    R = packed.shape[0]

    def body(x_ref, o_ref, buf, send, recv, loc):
        x, y, c, _ = _place()
        me = 4 * x + 2 * y + c
        cps = [pltpu.make_async_copy(x_ref, buf.at[me], loc)]
        cps[0].start()
        for d in range(1, 8):
            px, py, pc = x ^ (d >> 2), y ^ ((d >> 1) & 1), c ^ (d & 1)
            cp = pltpu.make_async_remote_copy(
                src_ref=x_ref, dst_ref=buf.at[me], send_sem=send.at[d - 1], recv_sem=recv.at[d - 1],
                device_id=(px, py, pc), device_id_type=MESH)
            cp.start()
            cps.append(cp)
        for cp in cps:
            cp.wait()
        acc = buf[0]
        for s in range(1, 8):
            acc = acc + buf[s]
        o_ref[...] = acc

    return pl.pallas_call(
        body, name=name, out_shape=_sds((R, LANES), F32),
        in_specs=[pl.BlockSpec(memory_space=pltpu.VMEM)], out_specs=pl.BlockSpec(memory_space=pltpu.VMEM),
        scratch_shapes=[pltpu.VMEM((8, R, LANES), F32), pltpu.SemaphoreType.DMA((7,)), pltpu.SemaphoreType.DMA((7,)),
                        pltpu.SemaphoreType.DMA],
        compiler_params=pltpu.CompilerParams(vmem_limit_bytes=VMEM_LIMIT),
    )(packed)


def _pack(arrs):
    flat = jnp.concatenate([a.reshape(-1) for a in arrs])
    pad = (-flat.shape[0]) % (8 * LANES)
    return jnp.pad(flat, (0, pad)).reshape(-1, LANES)


def _unpack(packed, like):
    flat = packed.reshape(-1)
    out, pos = [], 0
    for a in like:
        out.append(flat[pos:pos + a.size].reshape(a.shape))
        pos += a.size
    return out


def kernel(x, mem, mix_norm_g, mem_norm_g, w_in, gate_b, conv_w, conv_b, conv_ln_g, conv_ln_b, w_conv_out, rel_bias, w_attn_out, w_mem_kv, w_mem_out, w_o, ffn_norm_g, w_up, ffn_conv_w, ffn_conv_b, w_down, final_norm_g, loss_target, m_mix_norm_g, m_mem_norm_g, m_w_in, m_gate_b, m_conv_w, m_conv_b, m_conv_ln_g, m_conv_ln_b, m_w_conv_out, m_rel_bias, m_w_attn_out, m_w_mem_kv, m_w_mem_out, m_w_o, m_ffn_norm_g, m_w_up, m_ffn_conv_w, m_ffn_conv_b, m_w_down, m_final_norm_g, v_mix_norm_g, v_mem_norm_g, v_w_in, v_gate_b, v_conv_w, v_conv_b, v_conv_ln_g, v_conv_ln_b, v_w_conv_out, v_rel_bias, v_w_attn_out, v_w_mem_kv, v_w_mem_out, v_w_o, v_ffn_norm_g, v_w_up, v_ffn_conv_w, v_ffn_conv_b, v_w_down, v_final_norm_g):
    S, D = x.shape[1], x.shape[2]
    NM = mem.shape[1]
    L = w_in.shape[0]
    C = conv_b.shape[1]
    A = w_attn_out.shape[1]
    Dm = w_mem_out.shape[1]
    Fh = w_down.shape[1] * 4
    D_IN = w_in.shape[2] * 4
    CW = conv_w.shape[1]
    FW = ffn_conv_w.shape[1]
    H = A // ATTN_HEAD_DIM
    assert 2 * C == 2 * A == 2 * Dm == D and D_IN == 6 * D and S % QBLK == 0 and L % 2 == 0
    xi, yi, ci = lax.axis_index("x"), lax.axis_index("y"), lax.axis_index("c")
    chip = 2 * xi + yi
    place_sc = jnp.stack([chip, ci]).astype(jnp.int32)
    zero_sc = jnp.zeros((1,), jnp.int32)
    lh = L // 2

    big = [w_in, w_conv_out, w_attn_out, w_mem_kv, w_mem_out, w_o, w_up, w_down]
    big_m = [m_w_in, m_w_conv_out, m_w_attn_out, m_w_mem_kv, m_w_mem_out, m_w_o, m_w_up, m_w_down]
    big_v = [v_w_in, v_w_conv_out, v_w_attn_out, v_w_mem_kv, v_w_mem_out, v_w_o, v_w_up, v_w_down]
    big_ax = [2, 2, 2, 1, 2, 1, 2, 1]
    placed = []
    for k, (w, ax, dt) in enumerate(zip(big + [conv_w, ffn_conv_w], big_ax + [2, 2], [BF16] * len(big) + [F32, F32])):
        shp = list(w.shape)
        shp[ax] *= 4
        placed.append(_ew3(f"place_w{k}", lambda a: (a,), place_sc, w.shape, [(w, _at())],
                           [(_sds(shp, dt), _at(ax=ax, size=w.shape[ax]))])[0])
    W_in, W_co, W_ao, W_kv, W_mo, W_o, W_up, W_dn, conv_w_f, ffn_conv_w_f = gather_weights(placed, big_ax + [2, 2])

    r3 = lambda a: a.reshape(L, 1, a.shape[-1])
    mix_g3, mem_g3, ffn_g3, gb3 = r3(mix_norm_g), r3(mem_norm_g), r3(ffn_norm_g), r3(gate_b)
    cb3, lg3, lb3, fb3 = r3(conv_b), r3(conv_ln_g), r3(conv_ln_b), r3(ffn_conv_b)
    cw_p = jnp.pad(conv_w_f, ((0, 0), (0, CONV_HALO - CW), (0, 0)))
    fw_p = jnp.pad(ffn_conv_w_f, ((0, 0), (0, FFN_HALO - FW), (0, 0)))
    rel_p = jnp.pad(rel_bias, ((0, 0), (0, 0), (0, REL_PAD - rel_bias.shape[2])))

    x2, mem2, tgt2 = x[0], mem[0], loss_target[0]
    empty = lambda shape, dt: lax.empty(shape, dt)

    def layer_scalars(l):
        i32 = lambda *v: jnp.stack(v).astype(jnp.int32)
        return i32(l, 2 * l, 2 * l + 1), i32(l, 2 * l), i32(l, 2 * l + 1), i32(l, l)

    saved = dict(
        XN=empty((2 * L, S, D), BF16), HS=empty((2 * L, S, D), F32), PROJ=empty((L, S, D_IN), F32),
        HC=empty((L, S, C), F32), CACT=empty((L, S, C), BF16), AO=empty((L, S, A), BF16), MO=empty((L, S, Dm), BF16),
        Y=empty((L, 3, S, D), F32), MERGED=empty((L, S, D), BF16), UP=empty((L, S, 2 * Fh), F32),
        ACT=empty((L, S, Fh), BF16), MN=empty((L, NM, D), BF16), MEMS=empty((L, NM, D), F32),
        KV=empty((L, NM, 2 * Dm), F32), BIAS=empty((L, A // LANES, BANDP, LANES), F32))

    def fwd_layer(l, carry):
        h, sv = carry
        sv = dict(sv)
        sc, sc_a, sc_f, sc_m = layer_scalars(l)
        sv["XN"], sv["HS"] = rms_fwd("rms_mix", sc_a, h, mix_g3, sv["XN"], sv["HS"])
        sv["PROJ"] = mm_nn("mm_in", sc_a, sv["XN"], L1, W_in, S, D, D_IN, sv["PROJ"], L0)
        sv["HC"], sv["CACT"] = conv_fwd("conv_fwd", sc, sv["PROJ"], cw_p, cb3, lg3, lb3, sv["HC"], sv["CACT"], C, CW)
        bias_q = bias_expand("bias_expand", sc, rel_p, H)
        sv["BIAS"] = lax.dynamic_update_slice(sv["BIAS"], bias_to_lanes(bias_q, A // LANES)[None], (l, 0, 0, 0))
        sv["AO"], = attn_fwd("attn_fwd", sc, sv["PROJ"], sv["BIAS"], sv["AO"], A, 2)
        sv["MN"], sv["MEMS"] = rms_fwd("rms_mem", sc_m, mem2, mem_g3, sv["MN"], sv["MEMS"])
        sv["KV"] = mm_nn("mm_kv", sc, sv["MN"], L0, W_kv, NM, D, 2 * Dm, sv["KV"], L0)
        sv["MO"], = mem_fwd("mem_fwd", sc, sv["PROJ"], sv["KV"], sv["MO"], Dm, 5)
        for b, (src, w, nm) in enumerate(((sv["CACT"], W_co, "mm_co"), (sv["AO"], W_ao, "mm_ao"), (sv["MO"], W_mo, "mm_mo"))):
            sv["Y"] = mm_nn(nm, sc, src, L0, w, S, C, D, sv["Y"], lambda s, b=b: (s[0], b))
        sv["MERGED"], = gate_fwd("gate_fwd", sc, sv["PROJ"], gb3, sv["Y"], sv["MERGED"], D)
        h2 = mm_nn("mm_o", sc, sv["MERGED"], L0, W_o, S, D, D, _sds((S, D), F32), add=h)
        sv["XN"], sv["HS"] = rms_fwd("rms_ffn", sc_f, h2, ffn_g3, sv["XN"], sv["HS"])
        sv["UP"] = mm_nn("mm_up", sc_f, sv["XN"], L1, W_up, S, D, 2 * Fh, sv["UP"], L0)
        sv["ACT"], = ffn_fwd("ffn_fwd", sc, sv["UP"], fw_p, fb3, sv["ACT"], Fh, FW)
        h3 = mm_nn("mm_down", sc, sv["ACT"], L0, W_dn, S, Fh, D, _sds((S, D), F32), add=h2)
        return h3, sv

    h_last, saved = lax.fori_loop(0, L, fwd_layer, (x2, saved))
    loss_t, dh, dhb, d_final_g = final_loss("final_loss", zero_sc, h_last, final_norm_g.reshape(1, D), tgt2)
    loss = lax.psum(loss_t[0, 0], ("x", "y", "c"))

    zeros = lambda shape: jnp.zeros(shape, F32)
    grads = dict(
        w_in=empty((L, D, D_IN), F32), w_conv_out=empty((L, C, D), F32), w_attn_out=empty((L, A, D), F32),
        w_mem_kv=empty((L, D, 2 * Dm), F32), w_mem_out=empty((L, Dm, D), F32), w_o=empty((L, D, D), F32),
        w_up=empty((L, D, 2 * Fh), F32), w_down=empty((L, Fh, D), F32),
        mix_g=zeros((L, 1, D)), mem_g=zeros((L, 1, D)), ffn_g=zeros((L, 1, D)), gate_b=zeros((L, 1, 3 * D)),
        conv_w=zeros((L, CONV_HALO, C)), conv_b=zeros((L, 1, C)), ln_g=zeros((L, 1, C)), ln_b=zeros((L, 1, C)),
        rel=zeros((L, H, REL_PAD)), fw_v=zeros((L, FFN_HALO, Fh)), fw_g=zeros((L, FFN_HALO, Fh)),
        fb_v=zeros((L, 1, Fh)), fb_g=zeros((L, 1, Fh)))
    zero_mem = jnp.zeros((NM, D), F32)

    def bwd_layer(it, carry):
        dh, dhb, g = carry
        g = dict(g)
        l = L - 1 - it
        sc, sc_a, sc_f, sc_m = layer_scalars(l)
        sv = saved
        dact = mm_nt("mm_down_dx", sc, [(dhb, None)], W_dn, S, Fh, _sds((S, Fh), F32))
        g["w_down"] = mm_tn("mm_down_dw", sc, sv["ACT"], L0, dhb, None, S, Fh, D, g["w_down"], L0)
        dupv, dupg, g["fw_v"], g["fw_g"], g["fb_v"], g["fb_g"] = ffn_bwd(
            "ffn_bwd", sc, sv["UP"], dact, fw_p, fb3, g["fw_v"], g["fw_g"], g["fb_v"], g["fb_g"], Fh, FW)
        dhn = mm_nt("mm_up_dx", sc, [(dupv, None), (dupg, None)], W_up, S, D, _sds((S, D), F32))
        g["w_up"] = mm_tn("mm_up_dw_v", sc_f, sv["XN"], L1, dupv, None, S, D, Fh, g["w_up"], L0)
        g["w_up"] = mm_tn("mm_up_dw_g", sc_f, sv["XN"], L1, dupg, None, S, D, Fh, g["w_up"], L0, out_joff=Fh)
        dh2, dh2b, g["ffn_g"] = rms_bwd("rms_ffn_bwd", sc_f, sv["HS"], ffn_g3, dhn, dh, g["ffn_g"])
        dmerged = mm_nt("mm_o_dx", sc, [(dh2b, None)], W_o, S, D, _sds((S, D), F32))
        g["w_o"] = mm_tn("mm_o_dw", sc, sv["MERGED"], L0, dh2b, None, S, D, D, g["w_o"], L0)
        dy, dgates, g["gate_b"] = gate_bwd("gate_bwd", sc, sv["PROJ"], gb3, sv["Y"], dmerged, g["gate_b"], D)
        dcact = mm_nt("mm_co_dx", sc, [(dy, lambda s: (0,))], W_co, S, C, _sds((S, C), F32))
        dao = mm_nt("mm_ao_dx", sc, [(dy, lambda s: (1,))], W_ao, S, A, _sds((S, A), BF16), out_dtype=BF16)
        dmo = mm_nt("mm_mo_dx", sc, [(dy, lambda s: (2,))], W_mo, S, Dm, _sds((S, Dm), BF16), out_dtype=BF16)
        g["w_conv_out"] = mm_tn("mm_co_dw", sc, sv["CACT"], L0, dy, lambda s: (0,), S, C, D, g["w_conv_out"], L0)
        g["w_attn_out"] = mm_tn("mm_ao_dw", sc, sv["AO"], L0, dy, lambda s: (1,), S, A, D, g["w_attn_out"], L0)
        g["w_mem_out"] = mm_tn("mm_mo_dw", sc, sv["MO"], L0, dy, lambda s: (2,), S, Dm, D, g["w_mem_out"], L0)
        dhc, g["ln_g"], g["ln_b"], g["conv_b"] = conv_bwd1(
            "conv_bwd1", sc, sv["HC"], dcact, lg3, lb3, g["ln_g"], g["ln_b"], g["conv_b"])
        dconv, g["conv_w"] = conv_bwd2("conv_bwd2", sc, sv["PROJ"], dhc, cw_p, g["conv_w"], C, CW)
        dq, dk, dv, dbias = attn_bwd("attn_bwd", sc, sv["PROJ"], sv["BIAS"], dao, A, 2)
        g["rel"] = bias_reduce("bias_reduce", sc, bias_from_lanes(dbias, A // LANES), g["rel"], H)
        dqm, dkv = mem_bwd("mem_bwd", sc, sv["PROJ"], sv["KV"], dmo, Dm, 5)
        g["w_mem_kv"] = mm_tn("mm_kv_dw", sc, sv["MN"], L0, dkv, None, NM, D, 2 * Dm, g["w_mem_kv"], L0)
        dmn = mm_nt("mm_kv_dx", sc, [(dkv, None)], W_kv, NM, D, _sds((NM, D), F32))
        _, _, g["mem_g"] = rms_bwd("rms_mem_bwd", sc_m, sv["MEMS"], mem_g3, dmn, zero_mem, g["mem_g"])
        pieces = [(dconv, None), (dq, None), (dk, None), (dv, None), (dqm, None), (dgates, None)]
        dxn = mm_nt("mm_in_dx", sc, pieces, W_in, S, D, _sds((S, D), F32), tk=1024)
        off = 0
        for nm, (p, _) in zip(("c", "q", "k", "v", "m", "g"), pieces):
            g["w_in"] = mm_tn("mm_in_dw_" + nm, sc_a, sv["XN"], L1, p, None, S, D, p.shape[1], g["w_in"], L0, out_joff=off)
            off += p.shape[1]
        dh0, dh0b, g["mix_g"] = rms_bwd("rms_mix_bwd", sc_a, sv["HS"], mix_g3, dxn, dh2, g["mix_g"])
        return dh0, dh0b, g

    grad_x2, _, grads = lax.fori_loop(0, L, bwd_layer, (dh, dhb, grads))

    names = ["w_in", "w_conv_out", "w_attn_out", "w_mem_kv", "w_mem_out", "w_o", "w_up", "w_down"]
    gl = [grads[nm] for nm in names]
    from_sib = swap_halves("swap_halves", gl)
    my_layer = lambda g_, s: (s[1] * lh + g_,)
    sums_f, sums_b = [], []
    for k, (g_, r_) in enumerate(zip(gl, from_sib)):
        s_f, s_b = _ew3(f"sum_sib{k}", lambda a, b: (a + b, a + b), place_sc, r_.shape,
                        [(g_, _at(my_layer)), (r_, _at())], [(_sds(r_.shape, F32), _at()), (_sds(r_.shape, BF16), _at())])
        sums_f.append(s_f)
        sums_b.append(s_b)
    got = exchange_chip_sums("exchange_chip_sums", sums_b, big_ax)
    halves = []
    for k, (s_f, r_, w, ax) in enumerate(zip(sums_f, got, big, big_ax)):
        fin, = _ew3(f"sum_chips{k}", lambda a, b0, b1, b2: (((a + b0.astype(F32)) + b1.astype(F32)) + b2.astype(F32),),
                    place_sc, (lh,) + w.shape[1:],
                    [(s_f, _at(ax=ax, size=w.shape[ax]))] + [(r_, _at(lambda g_, s, j=j: (j, g_))) for j in range(3)],
                    [(_sds(w.shape, F32), _at(my_layer))])
        halves.append(fin)
    big_g = join_halves("join_halves", halves)
    big_d, big_nm, big_nv = [], [], []
    for k in range(len(big)):
        d_, m_, v_ = adamw(f"adamw_big{k}", big[k], big_g[k], big_m[k], big_v[k])
        big_d.append(d_), big_nm.append(m_), big_nv.append(v_)

    g_small_full = [
        grads["mix_g"].reshape(L, D), grads["mem_g"].reshape(L, D), grads["gate_b"].reshape(L, 3 * D),
        grads["conv_w"][:, :CW, :], grads["conv_b"].reshape(L, C), grads["ln_g"].reshape(L, C), grads["ln_b"].reshape(L, C),
        grads["rel"][:, :, :rel_bias.shape[2]], grads["ffn_g"].reshape(L, D),
        jnp.concatenate([grads["fw_v"][:, :FW, :], grads["fw_g"][:, :FW, :]], axis=-1),
        jnp.concatenate([grads["fb_v"], grads["fb_g"]], axis=-1).reshape(L, 2 * Fh), d_final_g.reshape(D)]
    summed = _unpack(allreduce_small("allreduce_small", _pack(g_small_full)), g_small_full)
    cws, fws = conv_w.shape[2], ffn_conv_w.shape[2]
    summed[3] = lax.dynamic_slice_in_dim(summed[3], chip * cws, cws, axis=2)
    summed[9] = lax.dynamic_slice_in_dim(summed[9], chip * fws, fws, axis=2)
    small_w = [mix_norm_g, mem_norm_g, gate_b, conv_w, conv_b, conv_ln_g, conv_ln_b, rel_bias, ffn_norm_g, ffn_conv_w, ffn_conv_b, final_norm_g]
    small_m = [m_mix_norm_g, m_mem_norm_g, m_gate_b, m_conv_w, m_conv_b, m_conv_ln_g, m_conv_ln_b, m_rel_bias, m_ffn_norm_g, m_ffn_conv_w, m_ffn_conv_b, m_final_norm_g]
    small_v = [v_mix_norm_g, v_mem_norm_g, v_gate_b, v_conv_w, v_conv_b, v_conv_ln_g, v_conv_ln_b, v_rel_bias, v_ffn_norm_g, v_ffn_conv_w, v_ffn_conv_b, v_final_norm_g]
    sd, sm, sv_ = _ew("adamw_small", _adamw_math, [_pack(small_w), _pack(summed), _pack(small_m), _pack(small_v)], (F32, F32, F32))
    small_d, small_nm, small_nv = _unpack(sd, small_w), _unpack(sm, small_w), _unpack(sv_, small_w)

    order = ["mix_norm_g", "mem_norm_g", "w_in", "gate_b", "conv_w", "conv_b", "conv_ln_g", "conv_ln_b", "w_conv_out",
             "rel_bias", "w_attn_out", "w_mem_kv", "w_mem_out", "w_o", "ffn_norm_g", "w_up", "ffn_conv_w", "ffn_conv_b",
             "w_down", "final_norm_g"]
    small_names = ["mix_norm_g", "mem_norm_g", "gate_b", "conv_w", "conv_b", "conv_ln_g", "conv_ln_b", "rel_bias",
                   "ffn_norm_g", "ffn_conv_w", "ffn_conv_b", "final_norm_g"]

    def collect(bigs, smalls):
        table = dict(zip(names, bigs))
        table.update(zip(small_names, smalls))
        return [table[nm] for nm in order]

    return (loss, grad_x2[None], *collect(big_g, summed), *collect(big_d, small_d), *collect(big_nm, small_nm),
            *collect(big_nv, small_nv))
```

```python
import jax
import jax.numpy as jnp
from jax import lax
from jax.experimental import pallas as pl
from jax.experimental.pallas import tpu as pltpu

F32 = jnp.float32
BF16 = jnp.bfloat16
MESH = pl.DeviceIdType.MESH

CHUNK = 64
LEFT_CHUNKS = 8
BAND = (LEFT_CHUNKS + 1) * CHUNK
BANDP = BAND + CHUNK
MAX_REL = 256
REL_PAD = 640
ATTN_HEAD_DIM = 64
N_MEM_HEADS = 4
QBLK = LEFT_CHUNKS * CHUNK
EPS = 1e-6
NEG_INF = -1e30
LANES = 128
VMEM_LIMIT = 56 * 1024 * 1024

ADAM_LR = 0.001
ADAM_B1 = 0.9
ADAM_B2 = 0.999
ADAM_EPS = 1e-08
ADAM_WD = 0.01
ADAM_STEP = 10


def _pick(dim, target, unit=LANES):
    if dim <= target:
        return dim
    d = (target // unit) * unit
    while d >= unit:
        if dim % d == 0:
            return d
        d -= unit
    raise ValueError(f"no tile for {dim} under {target}")


def _sigmoid(x):
    return 1.0 / (1.0 + jnp.exp(-x))


def _pcall(body, *, name, grid, sc, ins, outs, scratch=(), sem=None):
    arrays = [a for a, _ in ins]
    in_specs = [s for _, s in ins]
    n_in = len(arrays)
    out_shape, out_specs, aliases = [], [], {}
    for k, (o, spec) in enumerate(outs):
        if isinstance(o, jax.ShapeDtypeStruct):
            out_shape.append(o)
        else:
            aliases[1 + len(arrays)] = k
            arrays.append(o)
            in_specs.append(pl.BlockSpec(memory_space=pl.ANY))
            out_shape.append(jax.ShapeDtypeStruct(o.shape, o.dtype))
        out_specs.append(spec)
    n_alias = len(arrays) - n_in

    def wrapped(sc_ref, *refs):
        body(sc_ref, *refs[:n_in], *refs[n_in + n_alias:])

    res = pl.pallas_call(
        wrapped,
        name=name,
        grid_spec=pltpu.PrefetchScalarGridSpec(
            num_scalar_prefetch=1, grid=grid, in_specs=in_specs, out_specs=out_specs, scratch_shapes=list(scratch)),
        out_shape=out_shape,
        input_output_aliases=aliases,
        compiler_params=pltpu.CompilerParams(
            dimension_semantics=sem or ("arbitrary",) * len(grid), vmem_limit_bytes=VMEM_LIMIT),
    )(sc, *arrays)
    return res


def _bs(arr, blk, rc, lead=None):
    nlead = len(arr.shape) - 2

    def imap(*ids):
        sc = ids[-1]
        r, c = rc(*ids)
        return (*(lead(sc) if nlead else ()), r, c)

    return pl.BlockSpec((None,) * nlead + tuple(blk), imap)


def _sds(shape, dtype):
    return jax.ShapeDtypeStruct(tuple(shape), dtype)


L0 = lambda sc: (sc[0],)
L1 = lambda sc: (sc[1],)


def _mm_core(name, grid, sc, a_items, b_item, add_item, out_item, dims, ranges, out_dtype, acc_shape, gather=None):
    nk = grid[2]
    na = len(a_items)
    has_add = add_item is not None
    ng = len(gather[0]) if gather else 0

    def body(sc_ref, *refs):
        a_refs = refs[:na]
        b_ref = refs[na]
        pos = na + 1
        add_ref = refs[pos] if has_add else None
        pos += int(has_add)
        o_ref = refs[pos]
        g_refs = refs[pos + 1:pos + 1 + ng]
        pos += 1 + ng
        acc = refs[pos] if nk > 1 else None
        k = pl.program_id(2)
        if ng:
            sems = refs[pos + int(nk > 1):]
            at = [pl.program_id(d) for d in range(3)]
            first = (at[0] == 0) & (at[1] == 0) & (at[2] == 0)
            last = (at[0] == grid[0] - 1) & (at[1] == grid[1] - 1) & (at[2] == grid[2] - 1)
            nxt = sc_ref[0] + 1
            more = nxt < g_refs[0].shape[0]

            @pl.when(first & more)
            def _():
                _gather_layer_start(g_refs, gather[1], nxt, sems)

        def contrib(ar):
            return lax.dot_general(ar[...].astype(BF16), b_ref[...].astype(BF16), dims, preferred_element_type=F32)

        def fin(val):
            if has_add:
                val = val + add_ref[...]
            o_ref[...] = val.astype(out_dtype)

        if nk == 1:
            fin(contrib(a_refs[0]))
        for p, (k0, k1) in enumerate(ranges if nk > 1 else []):
            @pl.when((k >= k0) & (k < k1))
            def _(p=p):
                part = contrib(a_refs[p])

                @pl.when(k == 0)
                def _():
                    acc[...] = part

                @pl.when((k > 0) & (k < nk - 1))
                def _():
                    acc[...] += part

                @pl.when(k == nk - 1)
                def _():
                    fin(acc[...] + part)

        if ng:
            @pl.when(last & more)
            def _():
                _gather_layer_finish(g_refs, gather[1], nxt, sems)

    ins = list(a_items) + [b_item] + ([add_item] if has_add else [])
    scratch = [pltpu.VMEM(acc_shape, F32)] if nk > 1 else []
    outs = [out_item]
    if ng:
        outs += [(b, HBM_SPEC) for b in gather[0]]
        scratch += [pltpu.SemaphoreType.DMA((ng, 3)) for _ in range(4)]
    res = _pcall(body, name=name, grid=grid, sc=sc, ins=ins, outs=outs, scratch=scratch,
                 sem=("arbitrary",) * 3 if ng else ("parallel", "parallel", "arbitrary"))
    return res if ng else res[0]


def mm_nn(name, sc, a, a_lead, w, M, K, N, out, out_lead=None, out_joff=0, add=None, out_dtype=F32,
          tm=1024, tn=1408, tk=2048, gather=None):
    tm, tn, tk = _pick(M, tm, 8), _pick(N, tn), _pick(K, tk)
    nk = K // tk
    a_item = (a, _bs(a, (tm, tk), lambda i, j, k, sc: (i, k), a_lead))
    b_item = (w, _bs(w, (tk, tn), lambda i, j, k, sc: (k, j), L0))
    add_item = None if add is None else (add, _bs(add, (tm, tn), lambda i, j, k, sc: (i, j)))
    out_spec = _bs(out, (tm, tn), lambda i, j, k, sc: (i, j + out_joff // tn), out_lead)
    return _mm_core(name, (M // tm, N // tn, nk), sc, [a_item], b_item, add_item, (out, out_spec),
                    (((1,), (0,)), ((), ())), [(0, nk)], out_dtype, (tm, tn), gather=gather)


def mm_nt(name, sc, pieces, w, M, N, out, w_koff=0, out_dtype=F32, tm=1024, tn=1408, tk=2048):
    tm, tn = _pick(M, tm, 8), _pick(N, tn)
    widths = [a.shape[-1] for a, _ in pieces]
    tk = _pick(widths[0], tk)
    offs = [w_koff]
    for wd in widths:
        offs.append(offs[-1] + wd)
    while any(x % tk for x in offs):
        tk = _pick(widths[0], tk - LANES)
    ranges, a_items, k0 = [], [], 0
    for (a, lead), wd in zip(pieces, widths):
        n = wd // tk
        ranges.append((k0, k0 + n))
        a_items.append((a, _bs(a, (tm, tk), lambda i, j, k, sc, k0=k0, n=n: (i, jnp.clip(k - k0, 0, n - 1)), lead)))
        k0 += n
    nk = k0
    b_item = (w, _bs(w, (tn, tk), lambda i, j, k, sc: (j, k + w_koff // tk), L0))
    out_spec = _bs(out, (tm, tn), lambda i, j, k, sc: (i, j))
    return _mm_core(name, (M // tm, N // tn, nk), sc, a_items, b_item, None, (out, out_spec),
                    (((1,), (1,)), ((), ())), ranges, out_dtype, (tm, tn))


def mm_tn(name, sc, a, a_lead, b, b_lead, S, K, N, out, out_lead, out_joff=0, tm=1408, tn=1408, tk=2048):
    tm, tn, tk = _pick(K, tm), _pick(N, tn), _pick(S, tk, 8)
    while out_joff % tn:
        tn = _pick(N, tn - LANES)
    nk = S // tk
    a_item = (a, _bs(a, (tk, tm), lambda i, j, k, sc: (k, i), a_lead))
    b_item = (b, _bs(b, (tk, tn), lambda i, j, k, sc: (k, j), b_lead))
    out_spec = _bs(out, (tm, tn), lambda i, j, k, sc: (i, j + out_joff // tn), out_lead)
    return _mm_core(name, (K // tm, N // tn, nk), sc, [a_item], b_item, None, (out, out_spec),
                    (((0,), (0,)), ((), ())), [(0, nk)], F32, (tm, tn))


def rms_fwd(name, sc, h, g3, xn_buf, hs_buf):
    S, D = h.shape
    tr = _pick(S, 256, 8)

    def body(sc_ref, h_ref, g_ref, xn_ref, hs_ref):
        x = h_ref[...]
        r = lax.rsqrt(jnp.mean(x * x, axis=-1, keepdims=True) + EPS)
        xn_ref[...] = (x * r * g_ref[...]).astype(BF16)
        hs_ref[...] = x

    return _pcall(body, name=name, grid=(S // tr,), sc=sc,
                  ins=[(h, _bs(h, (tr, D), lambda i, sc: (i, 0))), (g3, _bs(g3, (1, D), lambda i, sc: (0, 0), L0))],
                  outs=[(xn_buf, _bs(xn_buf, (tr, D), lambda i, sc: (i, 0), L1)),
                        (hs_buf, _bs(hs_buf, (tr, D), lambda i, sc: (i, 0), L1))])


def rms_bwd(name, sc, hs_buf, g3, dy, dres, dg_buf):
    S, D = dy.shape
    tr = _pick(S, 256, 8)

    def body(sc_ref, x_ref, g_ref, dy_ref, dres_ref, dx_ref, dxb_ref, dg_ref):
        i = pl.program_id(0)
        x = x_ref[...]
        r = lax.rsqrt(jnp.mean(x * x, axis=-1, keepdims=True) + EPS)
        xh = x * r
        dyv = dy_ref[...]
        dxh = dyv * g_ref[...]
        dx = r * (dxh - xh * jnp.mean(dxh * xh, axis=-1, keepdims=True)) + dres_ref[...]
        dx_ref[...] = dx
        dxb_ref[...] = dx.astype(BF16)
        part = jnp.sum(dyv * xh, axis=0, keepdims=True)

        @pl.when(i == 0)
        def _():
            dg_ref[...] = part

        @pl.when(i > 0)
        def _():
            dg_ref[...] += part

    return _pcall(body, name=name, grid=(S // tr,), sc=sc,
                  ins=[(hs_buf, _bs(hs_buf, (tr, D), lambda i, sc: (i, 0), L1)),
                       (g3, _bs(g3, (1, D), lambda i, sc: (0, 0), L0)),
                       (dy, _bs(dy, (tr, D), lambda i, sc: (i, 0))),
                       (dres, _bs(dres, (tr, D), lambda i, sc: (i, 0)))],
                  outs=[(_sds((S, D), F32), pl.BlockSpec((tr, D), lambda i, sc: (i, 0))),
                        (_sds((S, D), BF16), pl.BlockSpec((tr, D), lambda i, sc: (i, 0))),
                        (dg_buf, _bs(dg_buf, (1, D), lambda i, sc: (0, 0), L0))])


CONV_HALO = 32
CONV_RB = 32


def conv_fwd(name, sc, proj, cw, cb, lg, lb, hc_buf, cact_buf, C, W):
    S = proj.shape[1]
    T = _pick(S, 256, CONV_HALO)
    nh = T // CONV_HALO
    off = CONV_HALO - (W - 1)

    def body(sc_ref, ua_ref, ug_ref, pa_ref, pg_ref, w_ref, b_ref, lg_ref, lb_ref, hc_ref, ca_ref, buf):
        i = pl.program_id(0)
        halo = pa_ref[...] * _sigmoid(pg_ref[...])
        buf[0:CONV_HALO, :] = jnp.where(i > 0, halo, 0.0)
        buf[CONV_HALO:CONV_HALO + T, :] = ua_ref[...] * _sigmoid(ug_ref[...])
        for r in range(0, T, CONV_RB):
            acc = jnp.zeros((CONV_RB, C), F32) + b_ref[...]
            for k in range(W):
                acc = acc + w_ref[k:k + 1, :] * buf[r + off + k:r + off + k + CONV_RB, :]
            hc_ref[r:r + CONV_RB, :] = acc
        hc = hc_ref[...]
        mu = jnp.mean(hc, axis=-1, keepdims=True)
        d = hc - mu
        rstd = lax.rsqrt(jnp.mean(d * d, axis=-1, keepdims=True) + EPS)
        y = d * rstd * lg_ref[...] + lb_ref[...]
        ca_ref[...] = (y * _sigmoid(y)).astype(BF16)

    vec = lambda a: (a, _bs(a, (1, C), lambda i, sc: (0, 0), L0))
    return _pcall(body, name=name, grid=(S // T,), sc=sc,
                  ins=[(proj, _bs(proj, (T, C), lambda i, sc: (i, 0), L0)),
                       (proj, _bs(proj, (T, C), lambda i, sc: (i, 1), L0)),
                       (proj, _bs(proj, (CONV_HALO, C), lambda i, sc: (jnp.maximum(i * nh - 1, 0), 0), L0)),
                       (proj, _bs(proj, (CONV_HALO, C), lambda i, sc: (jnp.maximum(i * nh - 1, 0), 1), L0)),
                       (cw, _bs(cw, (CONV_HALO, C), lambda i, sc: (0, 0), L0)), vec(cb), vec(lg), vec(lb)],
                  outs=[(hc_buf, _bs(hc_buf, (T, C), lambda i, sc: (i, 0), L0)),
                        (cact_buf, _bs(cact_buf, (T, C), lambda i, sc: (i, 0), L0))],
                  scratch=[pltpu.VMEM((T + CONV_HALO, C), F32)])


def conv_bwd1(name, sc, hc_buf, dcact, lg, lb, dlg_buf, dlb_buf, dcb_buf):
    S, C = dcact.shape
    tr = _pick(S, 256, 8)

    def body(sc_ref, hc_ref, dc_ref, lg_ref, lb_ref, dhc_ref, dlg_ref, dlb_ref, dcb_ref):
        i = pl.program_id(0)
        hc = hc_ref[...]
        mu = jnp.mean(hc, axis=-1, keepdims=True)
        d = hc - mu
        rstd = lax.rsqrt(jnp.mean(d * d, axis=-1, keepdims=True) + EPS)
        yh = d * rstd
        y = yh * lg_ref[...] + lb_ref[...]
        sg = _sigmoid(y)
        dy = dc_ref[...] * (sg * (1.0 + y * (1.0 - sg)))
        dyh = dy * lg_ref[...]
        dhc = rstd * (dyh - jnp.mean(dyh, axis=-1, keepdims=True) - yh * jnp.mean(dyh * yh, axis=-1, keepdims=True))
        dhc_ref[...] = dhc
        parts = (jnp.sum(dy * yh, axis=0, keepdims=True), jnp.sum(dy, axis=0, keepdims=True),
                 jnp.sum(dhc, axis=0, keepdims=True))

        @pl.when(i == 0)
        def _():
            dlg_ref[...], dlb_ref[...], dcb_ref[...] = parts

        @pl.when(i > 0)
        def _():
            dlg_ref[...] += parts[0]
            dlb_ref[...] += parts[1]
            dcb_ref[...] += parts[2]

    vec = lambda a: (a, _bs(a, (1, C), lambda i, sc: (0, 0), L0))
    return _pcall(body, name=name, grid=(S // tr,), sc=sc,
                  ins=[(hc_buf, _bs(hc_buf, (tr, C), lambda i, sc: (i, 0), L0)),
                       (dcact, _bs(dcact, (tr, C), lambda i, sc: (i, 0))), vec(lg), vec(lb)],
                  outs=[(_sds((S, C), F32), pl.BlockSpec((tr, C), lambda i, sc: (i, 0))),
                        vec(dlg_buf), vec(dlb_buf), vec(dcb_buf)])


def conv_bwd2(name, sc, proj, dhc, cw, dcw_buf, C, W):
    S = proj.shape[1]
    T = _pick(S, 256, CONV_HALO)
    nh = T // CONV_HALO
    nt = S // T
    off = CONV_HALO - (W - 1)

    def body(sc_ref, ua_ref, ug_ref, pa_ref, pg_ref, d_ref, dn_ref, w_ref, o_ref, dw_ref, buf, dbuf, dhg):
        i = pl.program_id(0)
        halo = pa_ref[...] * _sigmoid(pg_ref[...])
        buf[0:CONV_HALO, :] = jnp.where(i > 0, halo, 0.0)
        sg = _sigmoid(ug_ref[...])
        ua = ua_ref[...]
        buf[CONV_HALO:CONV_HALO + T, :] = ua * sg
        dbuf[0:T, :] = d_ref[...]
        dbuf[T:T + CONV_HALO, :] = jnp.where(i < nt - 1, dn_ref[...], 0.0)

        @pl.when(i == 0)
        def _():
            dw_ref[...] = jnp.zeros_like(dw_ref)

        for r in range(0, T, CONV_RB):
            acc = jnp.zeros((CONV_RB, C), F32)
            for k in range(W):
                acc = acc + w_ref[k:k + 1, :] * dbuf[r + (W - 1) - k:r + (W - 1) - k + CONV_RB, :]
            dhg[r:r + CONV_RB, :] = acc
        for k in range(W):
            dw_ref[k:k + 1, :] += jnp.sum(dbuf[0:T, :] * buf[off + k:off + k + T, :], axis=0, keepdims=True)
        dh = dhg[...]
        o_ref[:, 0:C] = (dh * sg).astype(BF16)
        o_ref[:, C:2 * C] = (dh * ua * sg * (1.0 - sg)).astype(BF16)

    return _pcall(body, name=name, grid=(nt,), sc=sc,
                  ins=[(proj, _bs(proj, (T, C), lambda i, sc: (i, 0), L0)),
                       (proj, _bs(proj, (T, C), lambda i, sc: (i, 1), L0)),
                       (proj, _bs(proj, (CONV_HALO, C), lambda i, sc: (jnp.maximum(i * nh - 1, 0), 0), L0)),
                       (proj, _bs(proj, (CONV_HALO, C), lambda i, sc: (jnp.maximum(i * nh - 1, 0), 1), L0)),
                       (dhc, _bs(dhc, (T, C), lambda i, sc: (i, 0))),
                       (dhc, _bs(dhc, (CONV_HALO, C), lambda i, sc: (jnp.minimum((i + 1) * nh, S // CONV_HALO - 1), 0))),
                       (cw, _bs(cw, (CONV_HALO, C), lambda i, sc: (0, 0), L0))],
                  outs=[(_sds((S, 2 * C), BF16), pl.BlockSpec((T, 2 * C), lambda i, sc: (i, 0))),
                        (dcw_buf, _bs(dcw_buf, (CONV_HALO, C), lambda i, sc: (0, 0), L0))],
                  scratch=[pltpu.VMEM((T + CONV_HALO, C), F32), pltpu.VMEM((T + CONV_HALO, C), F32),
                           pltpu.VMEM((T, C), F32)])


def gate_fwd(name, sc, proj, gb3, y_buf, merged_buf, D):
    S = proj.shape[1]
    tr, tc = _pick(S, 256, 8), _pick(D, 1024)
    nc = D // tc

    def body(sc_ref, g0, g1, g2, b0, b1, b2, y0, y1, y2, o_ref):
        acc = None
        for g, b, y in ((g0, b0, y0), (g1, b1, y1), (g2, b2, y2)):
            t = _sigmoid(g[...] + b[...]) * y[...]
            acc = t if acc is None else acc + t
        o_ref[...] = acc.astype(BF16)

    ins = [(proj, _bs(proj, (tr, tc), lambda i, j, sc, b=b: (i, (3 + b) * nc + j), L0)) for b in range(3)]
    ins += [(gb3, _bs(gb3, (1, tc), lambda i, j, sc, b=b: (0, b * nc + j), L0)) for b in range(3)]
    ins += [(y_buf, _bs(y_buf, (tr, tc), lambda i, j, sc: (i, j), lambda sc, b=b: (sc[0], b))) for b in range(3)]
    return _pcall(body, name=name, grid=(S // tr, nc), sc=sc, ins=ins,
                  outs=[(merged_buf, _bs(merged_buf, (tr, tc), lambda i, j, sc: (i, j), L0))])


def gate_bwd(name, sc, proj, gb3, y_buf, dmerged, dgb_buf, D):
    S = proj.shape[1]
    tr, tc = _pick(S, 256, 8), _pick(D, 1024)
    nc = D // tc

    def body(sc_ref, g_ref, b_ref, y_ref, dm_ref, dy_ref, dg_ref, dgb_ref):
        i = pl.program_id(2)
        g = _sigmoid(g_ref[...] + b_ref[...])
        dm = dm_ref[...]
        dy_ref[...] = (dm * g).astype(BF16)
        dgt = dm * y_ref[...] * g * (1.0 - g)
        dg_ref[...] = dgt.astype(BF16)
        part = jnp.sum(dgt, axis=0, keepdims=True)

        @pl.when(i == 0)
        def _():
            dgb_ref[...] = part

        @pl.when(i > 0)
        def _():
            dgb_ref[...] += part

    dy_sds = _sds((3, S, D), BF16)
    return _pcall(body, name=name, grid=(3, nc, S // tr), sc=sc,
                  ins=[(proj, _bs(proj, (tr, tc), lambda b, j, i, sc: (i, (3 + b) * nc + j), L0)),
                       (gb3, _bs(gb3, (1, tc), lambda b, j, i, sc: (0, b * nc + j), L0)),
                       (y_buf, pl.BlockSpec((None, None, tr, tc), lambda b, j, i, sc: (sc[0], b, i, j))),
                       (dmerged, _bs(dmerged, (tr, tc), lambda b, j, i, sc: (i, j)))],
                  outs=[(dy_sds, pl.BlockSpec((None, tr, tc), lambda b, j, i, sc: (b, i, j))),
                        (_sds((S, 3 * D), BF16), pl.BlockSpec((tr, tc), lambda b, j, i, sc: (i, b * nc + j))),
                        (dgb_buf, _bs(dgb_buf, (1, tc), lambda b, j, i, sc: (0, b * nc + j), L0))])


FFN_HALO = 8
FFN_RB = 16


def _taps(buf, w_ref, init, start, n, W, reverse=False):
    acc = init
    for k in range(W):
        s = start + ((W - 1 - k) if reverse else k)
        acc = acc + w_ref[k:k + 1, :] * buf[s:s + n, :]
    return acc


def ffn_fwd(name, sc, up, fw, fb3, act_buf, Fh, W):
    S = up.shape[1]
    T, tc = _pick(S, 512, FFN_RB), _pick(Fh, 512)
    nf = Fh // tc
    nh = T // FFN_HALO
    off = FFN_HALO - (W - 1)

    def body(sc_ref, v_ref, g_ref, pv_ref, pg_ref, wv_ref, wg_ref, bv_ref, bg_ref, o_ref, bufv, bufg):
        i = pl.program_id(1)
        for m_ref, p_ref, buf in ((v_ref, pv_ref, bufv), (g_ref, pg_ref, bufg)):
            buf[0:FFN_HALO, :] = jnp.where(i > 0, p_ref[...], 0.0)
            buf[FFN_HALO:FFN_HALO + T, :] = m_ref[...]
        for r in range(0, T, FFN_RB):
            val = _taps(bufv, wv_ref, bv_ref[...], r + off, FFN_RB, W)
            gt = _taps(bufg, wg_ref, bg_ref[...], r + off, FFN_RB, W)
            o_ref[r:r + FFN_RB, :] = (gt * _sigmoid(gt) * val).astype(BF16)

    prev = lambda i: jnp.maximum(i * nh - 1, 0)
    return _pcall(body, name=name, grid=(nf, S // T), sc=sc,
                  ins=[(up, _bs(up, (T, tc), lambda j, i, sc: (i, j), L0)),
                       (up, _bs(up, (T, tc), lambda j, i, sc: (i, j + nf), L0)),
                       (up, _bs(up, (FFN_HALO, tc), lambda j, i, sc: (prev(i), j), L0)),
                       (up, _bs(up, (FFN_HALO, tc), lambda j, i, sc: (prev(i), j + nf), L0)),
                       (fw, _bs(fw, (FFN_HALO, tc), lambda j, i, sc: (0, j), L0)),
                       (fw, _bs(fw, (FFN_HALO, tc), lambda j, i, sc: (0, j + nf), L0)),
                       (fb3, _bs(fb3, (1, tc), lambda j, i, sc: (0, j), L0)),
                       (fb3, _bs(fb3, (1, tc), lambda j, i, sc: (0, j + nf), L0))],
                  outs=[(act_buf, _bs(act_buf, (T, tc), lambda j, i, sc: (i, j), L0))],
                  scratch=[pltpu.VMEM((T + FFN_HALO, tc), F32), pltpu.VMEM((T + FFN_HALO, tc), F32)])


def ffn_bwd(name, sc, up, dact, fw, fb3, dfw_v_buf, dfw_g_buf, dfb_v_buf, dfb_g_buf, Fh, W):
    S = up.shape[1]
    T, tc = _pick(S, 512, FFN_RB), _pick(Fh, 512)
    nf = Fh // tc
    nh = T // FFN_HALO
    nt = S // T
    off = FFN_HALO - (W - 1)
    TE = T + FFN_HALO

    def body(sc_ref, v_ref, g_ref, pv_ref, pg_ref, nv_ref, ng_ref, d_ref, dn_ref, wv_ref, wg_ref, bv_ref, bg_ref,
             ov_ref, og_ref, dwv_ref, dwg_ref, dbv_ref, dbg_ref, bufv, bufg, dv, dg):
        i = pl.program_id(1)
        for m_ref, p_ref, n_ref, buf in ((v_ref, pv_ref, nv_ref, bufv), (g_ref, pg_ref, ng_ref, bufg)):
            buf[0:FFN_HALO, :] = jnp.where(i > 0, p_ref[...], 0.0)
            buf[FFN_HALO:FFN_HALO + T, :] = m_ref[...]
            buf[FFN_HALO + T:FFN_HALO + TE, :] = n_ref[...]

        def d_up(r, n, d):
            val = _taps(bufv, wv_ref, bv_ref[...], r + off, n, W)
            gt = _taps(bufg, wg_ref, bg_ref[...], r + off, n, W)
            sg = _sigmoid(gt)
            dv[r:r + n, :] = d * gt * sg
            dg[r:r + n, :] = d * val * (sg * (1.0 + gt * (1.0 - sg)))

        for r in range(0, T, FFN_RB):
            d_up(r, FFN_RB, d_ref[r:r + FFN_RB, :])
        d_up(T, FFN_HALO, jnp.where(i < nt - 1, dn_ref[...], 0.0))

        @pl.when(i == 0)
        def _():
            for r in (dwv_ref, dwg_ref, dbv_ref, dbg_ref):
                r[...] = jnp.zeros_like(r)

        fold = lambda a: a[0:8, :] + a[8:16, :]
        for dsrc, w_ref, buf, o_ref, dw_ref, db_ref in ((dv, wv_ref, bufv, ov_ref, dwv_ref, dbv_ref),
                                                      (dg, wg_ref, bufg, og_ref, dwg_ref, dbg_ref)):
            acc_w = [jnp.zeros((8, tc), F32) for _ in range(W)]
            acc_b = jnp.zeros((8, tc), F32)
            for r in range(0, T, FFN_RB):
                o_ref[r:r + FFN_RB, :] = _taps(dsrc, w_ref, 0.0, r, FFN_RB, W, reverse=True).astype(BF16)
                dm = dsrc[r:r + FFN_RB, :]
                for k in range(W):
                    acc_w[k] = acc_w[k] + fold(dm * buf[r + off + k:r + off + k + FFN_RB, :])
                acc_b = acc_b + fold(dm)
            for k in range(W):
                dw_ref[k:k + 1, :] += jnp.sum(acc_w[k], axis=0, keepdims=True)
            db_ref[...] += jnp.sum(acc_b, axis=0, keepdims=True)

    prev = lambda i: jnp.maximum(i * nh - 1, 0)
    nxt = lambda i: jnp.minimum((i + 1) * nh, S // FFN_HALO - 1)
    o_sds = _sds((S, Fh), BF16)
    return _pcall(body, name=name, grid=(nf, nt), sc=sc,
                  ins=[(up, _bs(up, (T, tc), lambda j, i, sc: (i, j), L0)),
                       (up, _bs(up, (T, tc), lambda j, i, sc: (i, j + nf), L0)),
                       (up, _bs(up, (FFN_HALO, tc), lambda j, i, sc: (prev(i), j), L0)),
                       (up, _bs(up, (FFN_HALO, tc), lambda j, i, sc: (prev(i), j + nf), L0)),
                       (up, _bs(up, (FFN_HALO, tc), lambda j, i, sc: (nxt(i), j), L0)),
                       (up, _bs(up, (FFN_HALO, tc), lambda j, i, sc: (nxt(i), j + nf), L0)),
                       (dact, _bs(dact, (T, tc), lambda j, i, sc: (i, j))),
                       (dact, _bs(dact, (FFN_HALO, tc), lambda j, i, sc: (nxt(i), j))),
                       (fw, _bs(fw, (FFN_HALO, tc), lambda j, i, sc: (0, j), L0)),
                       (fw, _bs(fw, (FFN_HALO, tc), lambda j, i, sc: (0, j + nf), L0)),
                       (fb3, _bs(fb3, (1, tc), lambda j, i, sc: (0, j), L0)),
                       (fb3, _bs(fb3, (1, tc), lambda j, i, sc: (0, j + nf), L0))],
                  outs=[(o_sds, pl.BlockSpec((T, tc), lambda j, i, sc: (i, j))),
                        (o_sds, pl.BlockSpec((T, tc), lambda j, i, sc: (i, j))),
                        (dfw_v_buf, _bs(dfw_v_buf, (FFN_HALO, tc), lambda j, i, sc: (0, j), L0)),
                        (dfw_g_buf, _bs(dfw_g_buf, (FFN_HALO, tc), lambda j, i, sc: (0, j), L0)),
                        (dfb_v_buf, _bs(dfb_v_buf, (1, tc), lambda j, i, sc: (0, j), L0)),
                        (dfb_g_buf, _bs(dfb_g_buf, (1, tc), lambda j, i, sc: (0, j), L0))],
                  scratch=[pltpu.VMEM((T + 2 * FFN_HALO, tc), F32), pltpu.VMEM((T + 2 * FFN_HALO, tc), F32),
                           pltpu.VMEM((T + 2 * FFN_HALO, tc), F32), pltpu.VMEM((T + 2 * FFN_HALO, tc), F32)],
                  sem=("parallel", "arbitrary"))


def final_loss(name, sc, h, g2, target):
    S, D = h.shape
    tr = _pick(S, 256, 8)

    def body(sc_ref, h_ref, g_ref, t_ref, loss_ref, dx_ref, dxb_ref, dg_ref):
        i = pl.program_id(0)
        x = h_ref[...]
        r = lax.rsqrt(jnp.mean(x * x, axis=-1, keepdims=True) + EPS)
        xh = x * r
        err = xh * g_ref[...] - t_ref[...]
        part_loss = 0.5 * jnp.sum(jnp.mean(err * err, axis=-1, keepdims=True), axis=0, keepdims=True)
        dy = err * (1.0 / D)
        dxh = dy * g_ref[...]
        dx = r * (dxh - xh * jnp.mean(dxh * xh, axis=-1, keepdims=True))
        dx_ref[...] = dx
        dxb_ref[...] = dx.astype(BF16)
        part_g = jnp.sum(dy * xh, axis=0, keepdims=True)

        @pl.when(i == 0)
        def _():
            loss_ref[...] = jnp.zeros_like(loss_ref) + part_loss
            dg_ref[...] = part_g

        @pl.when(i > 0)
        def _():
            loss_ref[...] += part_loss
            dg_ref[...] += part_g

    row = lambda a: (a, pl.BlockSpec((tr, D), lambda i, sc: (i, 0)))
    return _pcall(body, name=name, grid=(S // tr,), sc=sc,
                  ins=[row(h), (g2, pl.BlockSpec((1, D), lambda i, sc: (0, 0))), row(target)],
                  outs=[(_sds((8, LANES), F32), pl.BlockSpec((8, LANES), lambda i, sc: (0, 0))),
                        row(_sds((S, D), F32)), row(_sds((S, D), BF16)),
                        (_sds((1, D), F32), pl.BlockSpec((1, D), lambda i, sc: (0, 0)))])


def _split3(a):
    hi = a.astype(BF16)
    r1 = a - hi.astype(F32)
    mid = r1.astype(BF16)
    lo = (r1 - mid.astype(F32)).astype(BF16)
    return hi, mid, lo


def _rel_onehot(qi, rows_are_keys):
    shape = (BANDP, REL_PAD) if rows_are_keys else (REL_PAD, BANDP)
    km = lax.broadcasted_iota(jnp.int32, shape, 0 if rows_are_keys else 1)
    idx = lax.broadcasted_iota(jnp.int32, shape, 1 if rows_are_keys else 0)
    rel = jnp.clip(LEFT_CHUNKS * CHUNK + qi - km, -MAX_REL, MAX_REL) + MAX_REL
    return jnp.where(rel == idx, 1.0, 0.0).astype(BF16)


def bias_expand(name, sc, rel3, H):
    def body(sc_ref, rel_ref, o_ref):
        qi = pl.program_id(0)
        oh = _rel_onehot(qi, False)
        acc = jnp.zeros((H, BANDP), F32)
        for part in _split3(rel_ref[...]):
            acc = acc + jnp.dot(part, oh, preferred_element_type=F32)
        o_ref[...] = acc

    return _pcall(body, name=name, grid=(CHUNK,), sc=sc,
                  ins=[(rel3, _bs(rel3, (H, REL_PAD), lambda q, sc: (0, 0), L0))],
                  outs=[(_sds((CHUNK, H, BANDP), F32), pl.BlockSpec((None, H, BANDP), lambda q, sc: (q, 0, 0)))])[0]


def bias_reduce(name, sc, dbias_q, drel_buf, H):
    def body(sc_ref, d_ref, o_ref):
        qi = pl.program_id(0)
        oh = _rel_onehot(qi, True)
        acc = jnp.zeros((H, REL_PAD), F32)
        for part in _split3(d_ref[...]):
            acc = acc + jnp.dot(part, oh, preferred_element_type=F32)

        @pl.when(qi == 0)
        def _():
            o_ref[...] = acc

        @pl.when(qi > 0)
        def _():
            o_ref[...] += acc

    return _pcall(body, name=name, grid=(CHUNK,), sc=sc,
                  ins=[(dbias_q, pl.BlockSpec((None, H, BANDP), lambda q, sc: (q, 0, 0)))],
                  outs=[(drel_buf, _bs(drel_buf, (H, REL_PAD), lambda q, sc: (0, 0), L0))])[0]


def _head_blocks(a2):
    lo = lax.broadcasted_iota(jnp.int32, a2.shape, 1) < ATTN_HEAD_DIM
    zero = jnp.zeros_like(a2)
    return jnp.concatenate([jnp.where(lo, a2, zero), jnp.where(lo, zero, a2)], axis=0)


def _head_diag(r):
    lo = lax.broadcasted_iota(jnp.int32, (CHUNK, LANES), 1) < ATTN_HEAD_DIM
    return jnp.where(lo, r[0:CHUNK], r[CHUNK:2 * CHUNK])


def _attn_probs_t(q2, k2, bias_t, c, first, scale):
    qbd = _head_blocks(q2)
    s = lax.dot_general(k2, qbd, (((1,), (1,)), ((), ())), preferred_element_type=F32) * scale + bias_t
    km = lax.broadcasted_iota(jnp.int32, s.shape, 0)
    valid = (km < BAND) & (jnp.logical_not(first) | (km + c * CHUNK >= QBLK))
    s = jnp.where(valid, s, NEG_INF)
    p = jnp.exp(s - jnp.max(s, axis=0, keepdims=True))
    return qbd, p / jnp.sum(p, axis=0, keepdims=True)


def bias_to_lanes(bias_q, npair):
    t = jnp.transpose(bias_q, (1, 2, 0)).reshape(npair, 2, BANDP, CHUNK)
    return jnp.transpose(t, (0, 2, 1, 3)).reshape(npair, BANDP, 2 * CHUNK)


def bias_from_lanes(bias_t, npair):
    t = jnp.transpose(bias_t.reshape(npair, BANDP, 2, CHUNK), (3, 0, 2, 1))
    return t.reshape(CHUNK, 2 * npair, BANDP)


def _fill_window(win, prev_ref, cur_ref, A):
    win[0:QBLK, :] = prev_ref[...].astype(BF16)
    win[QBLK:2 * QBLK, :] = cur_ref[...].astype(BF16)
    win[2 * QBLK:2 * QBLK + CHUNK, :] = jnp.zeros((CHUNK, A), BF16)


def attn_fwd(name, sc, proj, bias_buf, ao_buf, A, qcol):
    S = proj.shape[1]
    nb = S // QBLK
    H = A // ATTN_HEAD_DIM
    npair = A // LANES
    scale = ATTN_HEAD_DIM ** -0.5

    def body(sc_ref, q_ref, kp_ref, kc_ref, vp_ref, vc_ref, b_ref, o_ref, kw, vw):
        first = pl.program_id(0) == 0
        _fill_window(kw, kp_ref, kc_ref, A)
        _fill_window(vw, vp_ref, vc_ref, A)

        def chunk(c, carry):
            r0 = pl.multiple_of(c * CHUNK, CHUNK)
            for hp in range(npair):
                cols = slice(hp * LANES, (hp + 1) * LANES)
                q2 = q_ref[pl.ds(r0, CHUNK), cols].astype(BF16)
                k2 = kw[pl.ds(r0, BANDP), cols]
                v2 = vw[pl.ds(r0, BANDP), cols]
                _, p = _attn_probs_t(q2, k2, b_ref[hp], c, first, scale)
                o = lax.dot_general(p.astype(BF16), v2, (((0,), (0,)), ((), ())), preferred_element_type=F32)
                o_ref[pl.ds(r0, CHUNK), cols] = _head_diag(o).astype(BF16)
            return carry

        lax.fori_loop(0, LEFT_CHUNKS, chunk, 0, unroll=2)

    prevb = lambda i: jnp.maximum(i - 1, 0)
    blk = lambda rowf, col: (proj, _bs(proj, (QBLK, A), lambda i, sc: (rowf(i), col), L0))
    same = lambda i: i
    return _pcall(body, name=name, grid=(nb,), sc=sc,
                  ins=[blk(same, qcol), blk(prevb, qcol + 1), blk(same, qcol + 1), blk(prevb, qcol + 2), blk(same, qcol + 2),
                       (bias_buf, pl.BlockSpec((None, npair, BANDP, LANES), lambda i, sc: (sc[0], 0, 0, 0)))],
                  outs=[(ao_buf, _bs(ao_buf, (QBLK, A), lambda i, sc: (i, 0), L0))],
                  scratch=[pltpu.VMEM((2 * QBLK + CHUNK, A), BF16), pltpu.VMEM((2 * QBLK + CHUNK, A), BF16)])


def attn_bwd(name, sc, proj, bias_buf, dao, A, qcol):
    S = proj.shape[1]
    nb = S // QBLK
    H = A // ATTN_HEAD_DIM
    npair = A // LANES
    scale = ATTN_HEAD_DIM ** -0.5
    WIN = 2 * QBLK + CHUNK

    def body(sc_ref, q_ref, kp_ref, kc_ref, vp_ref, vc_ref, b_ref, do_ref, dq_ref, dk_ref, dv_ref, db_ref, kw, vw, dkw, dvw):
        i = pl.program_id(0)
        first = i == 0

        @pl.when(first)
        def _():
            dkw[0:QBLK, :] = jnp.zeros((QBLK, A), F32)
            dvw[0:QBLK, :] = jnp.zeros((QBLK, A), F32)
            db_ref[...] = jnp.zeros_like(db_ref)

        @pl.when(i > 0)
        def _():
            dkw[0:QBLK, :] = dkw[QBLK:2 * QBLK, :]
            dvw[0:QBLK, :] = dvw[QBLK:2 * QBLK, :]

        dkw[QBLK:WIN, :] = jnp.zeros((WIN - QBLK, A), F32)
        dvw[QBLK:WIN, :] = jnp.zeros((WIN - QBLK, A), F32)

        @pl.when(i < nb)
        def _():
            _fill_window(kw, kp_ref, kc_ref, A)
            _fill_window(vw, vp_ref, vc_ref, A)

            def chunk(c, carry):
                r0 = pl.multiple_of(c * CHUNK, CHUNK)
                for hp in range(npair):
                    cols = slice(hp * LANES, (hp + 1) * LANES)
                    q2 = q_ref[pl.ds(r0, CHUNK), cols].astype(BF16)
                    k2 = kw[pl.ds(r0, BANDP), cols]
                    v2 = vw[pl.ds(r0, BANDP), cols]
                    dobd = _head_blocks(do_ref[pl.ds(r0, CHUNK), cols])
                    qbd, p = _attn_probs_t(q2, k2, b_ref[hp], c, first, scale)
                    dp = lax.dot_general(v2, dobd, (((1,), (1,)), ((), ())), preferred_element_type=F32)
                    ds = p * (dp - jnp.sum(dp * p, axis=0, keepdims=True))
                    db_ref[hp] += ds
                    dsb = ds.astype(BF16)
                    dq = lax.dot_general(dsb, k2, (((0,), (0,)), ((), ())), preferred_element_type=F32) * scale
                    dq_ref[pl.ds(r0, CHUNK), cols] = _head_diag(dq).astype(BF16)
                    dkw[pl.ds(r0, BANDP), cols] += jnp.dot(dsb, qbd, preferred_element_type=F32) * scale
                    dvw[pl.ds(r0, BANDP), cols] += jnp.dot(p.astype(BF16), dobd, preferred_element_type=F32)
                return carry

            lax.fori_loop(0, LEFT_CHUNKS, chunk, 0, unroll=2)

        dk_ref[...] = dkw[0:QBLK, :].astype(BF16)
        dv_ref[...] = dvw[0:QBLK, :].astype(BF16)

    cur = lambda i: jnp.minimum(i, nb - 1)
    prevb = lambda i: jnp.maximum(jnp.minimum(i, nb - 1) - 1, 0)
    done = lambda i: jnp.maximum(i - 1, 0)
    blk = lambda rowf, col: (proj, _bs(proj, (QBLK, A), lambda i, sc: (rowf(i), col), L0))
    o_sds = _sds((S, A), BF16)
    return _pcall(body, name=name, grid=(nb + 1,), sc=sc,
                  ins=[blk(cur, qcol), blk(prevb, qcol + 1), blk(cur, qcol + 1), blk(prevb, qcol + 2), blk(cur, qcol + 2),
                       (bias_buf, pl.BlockSpec((None, npair, BANDP, LANES), lambda i, sc: (sc[0], 0, 0, 0))),
                       (dao, pl.BlockSpec((QBLK, A), lambda i, sc: (cur(i), 0)))],
                  outs=[(o_sds, pl.BlockSpec((QBLK, A), lambda i, sc: (cur(i), 0))),
                        (o_sds, pl.BlockSpec((QBLK, A), lambda i, sc: (done(i), 0))),
                        (o_sds, pl.BlockSpec((QBLK, A), lambda i, sc: (done(i), 0))),
                        (_sds((npair, BANDP, LANES), F32), pl.BlockSpec((npair, BANDP, LANES), lambda i, sc: (0, 0, 0)))],
                  scratch=[pltpu.VMEM((WIN, A), BF16), pltpu.VMEM((WIN, A), BF16),
                           pltpu.VMEM((WIN, A), F32), pltpu.VMEM((WIN, A), F32)])


def mem_fwd(name, sc, proj, kv_buf, mo_buf, Dm, qcol):
    S = proj.shape[1]
    NM = kv_buf.shape[1]
    tr = _pick(S, 512, 8)
    hd = Dm // N_MEM_HEADS
    scale = hd ** -0.5

    def body(sc_ref, q_ref, kv_ref, o_ref):
        for h in range(N_MEM_HEADS):
            cols = slice(h * hd, (h + 1) * hd)
            q = q_ref[:, cols].astype(BF16)
            k = kv_ref[:, cols].astype(BF16)
            v = kv_ref[:, Dm + h * hd:Dm + (h + 1) * hd].astype(BF16)
            s = lax.dot_general(q, k, (((1,), (1,)), ((), ())), preferred_element_type=F32) * scale
            p = jnp.exp(s - jnp.max(s, axis=-1, keepdims=True))
            p = p / jnp.sum(p, axis=-1, keepdims=True)
            o_ref[:, cols] = jnp.dot(p.astype(BF16), v, preferred_element_type=F32).astype(BF16)

    return _pcall(body, name=name, grid=(S // tr,), sc=sc,
                  ins=[(proj, _bs(proj, (tr, Dm), lambda i, sc: (i, qcol), L0)),
                       (kv_buf, _bs(kv_buf, (NM, 2 * Dm), lambda i, sc: (0, 0), L0))],
                  outs=[(mo_buf, _bs(mo_buf, (tr, Dm), lambda i, sc: (i, 0), L0))])


def mem_bwd(name, sc, proj, kv_buf, dmo, Dm, qcol):
    S = proj.shape[1]
    NM = kv_buf.shape[1]
    tr = _pick(S, 512, 8)
    hd = Dm // N_MEM_HEADS
    scale = hd ** -0.5

    def body(sc_ref, q_ref, kv_ref, do_ref, dq_ref, dkv_ref):
        i = pl.program_id(0)

        @pl.when(i == 0)
        def _():
            dkv_ref[...] = jnp.zeros_like(dkv_ref)

        for h in range(N_MEM_HEADS):
            cols = slice(h * hd, (h + 1) * hd)
            vcols = slice(Dm + h * hd, Dm + (h + 1) * hd)
            q = q_ref[:, cols].astype(BF16)
            k = kv_ref[:, cols].astype(BF16)
            v = kv_ref[:, vcols].astype(BF16)
            do = do_ref[:, cols]
            s = lax.dot_general(q, k, (((1,), (1,)), ((), ())), preferred_element_type=F32) * scale
            p = jnp.exp(s - jnp.max(s, axis=-1, keepdims=True))
            p = p / jnp.sum(p, axis=-1, keepdims=True)
            dp = lax.dot_general(do, v, (((1,), (1,)), ((), ())), preferred_element_type=F32)
            ds = p * (dp - jnp.sum(dp * p, axis=-1, keepdims=True))
            dsb = ds.astype(BF16)
            dq_ref[:, cols] = (jnp.dot(dsb, k, preferred_element_type=F32) * scale).astype(BF16)
            dkv_ref[:, cols] += lax.dot_general(dsb, q, (((0,), (0,)), ((), ())), preferred_element_type=F32) * scale
            dkv_ref[:, vcols] += lax.dot_general(p.astype(BF16), do, (((0,), (0,)), ((), ())), preferred_element_type=F32)

    return _pcall(body, name=name, grid=(S // tr,), sc=sc,
                  ins=[(proj, _bs(proj, (tr, Dm), lambda i, sc: (i, qcol), L0)),
                       (kv_buf, _bs(kv_buf, (NM, 2 * Dm), lambda i, sc: (0, 0), L0)),
                       (dmo, pl.BlockSpec((tr, Dm), lambda i, sc: (i, 0)))],
                  outs=[(_sds((S, Dm), BF16), pl.BlockSpec((tr, Dm), lambda i, sc: (i, 0))),
                        (_sds((NM, 2 * Dm), F32), pl.BlockSpec((NM, 2 * Dm), lambda i, sc: (0, 0)))])


def _rows2d(a):
    return a.reshape(-1, a.shape[-1])


def _ew(name, fn, ins, out_dtypes):
    R, C = ins[0].shape
    tc = _pick(C, 2048)
    tr = _pick(R, max(8, (1 << 19) // tc), 8)
    n_in = len(ins)

    def body(sc_ref, *refs):
        outs = fn(*[r[...] for r in refs[:n_in]])
        for o_ref, o in zip(refs[n_in:], outs):
            o_ref[...] = o.astype(o_ref.dtype)

    plain = pl.BlockSpec((tr, tc), lambda i, j, sc: (i, j))
    return _pcall(body, name=name, grid=(R // tr, C // tc), sc=jnp.zeros((1,), jnp.int32), ins=[(a, plain) for a in ins],
                  outs=[(_sds((R, C), dt), plain) for dt in out_dtypes], sem=("parallel", "parallel"))


def _ew3(name, fn, sc, dims, ins, outs):
    G, R, C = dims
    tc = _pick(C, 2048)
    tr = _pick(R, max(16, (1 << 19) // tc), 16)
    n_in = len(ins)

    def body(sc_ref, *refs):
        res = fn(*[r[...] for r in refs[:n_in]])
        for o_ref, o in zip(refs[n_in:], res):
            o_ref[...] = o.astype(o_ref.dtype)

    def spec(arr, index):
        nlead = len(arr.shape) - 2

        def imap(g, i, j, s):
            lead, ro, co = index(g, s)
            return (*lead, i + ro // tr, j + co // tc)

        return pl.BlockSpec((None,) * nlead + (tr, tc), imap)

    return _pcall(body, name=name, grid=(G, R // tr, C // tc), sc=sc,
                  ins=[(a, spec(a, ix)) for a, ix in ins], outs=[(o, spec(o, ix)) for o, ix in outs],
                  sem=("parallel", "parallel", "parallel"))


def _at(lead_fn=None, ax=None, size=0):
    def index(g, s):
        lead = (g,) if lead_fn is None else lead_fn(g, s)
        off = s[0] * size
        return lead, (off if ax == 1 else 0), (off if ax == 2 else 0)
    return index


def _adamw_math(w, g, m, v):
    m = ADAM_B1 * m + (1.0 - ADAM_B1) * g
    v = ADAM_B2 * v + (1.0 - ADAM_B2) * (g * g)
    m_hat = m / (1.0 - ADAM_B1 ** ADAM_STEP)
    v_hat = v / (1.0 - ADAM_B2 ** ADAM_STEP)
    delta = -ADAM_LR * (m_hat / (jnp.sqrt(v_hat) + ADAM_EPS) + ADAM_WD * w)
    return delta, m, v


def adamw(name, w, g, m, v):
    shp = w.shape
    d, nm, nv = _ew(name, _adamw_math, [_rows2d(w), _rows2d(g), _rows2d(m), _rows2d(v)], (F32, F32, F32))
    return d.reshape(shp), nm.reshape(shp), nv.reshape(shp)


def _place():
    x, y, c = lax.axis_index("x"), lax.axis_index("y"), lax.axis_index("c")
    chips = [(1 - x, y), (x, 1 - y), (1 - x, 1 - y)]
    return x, y, c, chips


def _sub(ref, ax=None, s=None, n=None, half=None, lh=None):
    lay = slice(None) if half is None else pl.ds(half * lh, lh)
    if ax is None:
        return ref.at[lay]
    cut = pl.ds(pl.multiple_of(s * n, n), n)
    return ref.at[lay, cut, :] if ax == 1 else ref.at[lay, :, cut]


HBM_SPEC = pl.BlockSpec(memory_space=pl.ANY)


def _slab(ref, l, ax, s, half):
    K, N = ref.shape[1], ref.shape[2]
    if ax == 1:
        kh = K // 8
        return ref.at[l, pl.ds(pl.multiple_of((2 * s + half) * kh, kh), kh), :]
    kh, ns = K // 2, N // 4
    return ref.at[l, pl.ds(pl.multiple_of(half * kh, kh), kh), pl.ds(pl.multiple_of(s * ns, ns), ns)]


def _copy_to(view, sems_s, sems_r, k, j, to):
    return pltpu.make_async_remote_copy(src_ref=view, dst_ref=view, send_sem=sems_s.at[k, j], recv_sem=sems_r.at[k, j],
                                        device_id=to, device_id_type=MESH)


def _gather_start(n, view, sems):
    x, y, c, chips = _place()
    for k in range(n):
        for j, (px, py) in enumerate(chips):
            _copy_to(view(k, 2 * x + y, c), sems[0], sems[1], k, j, (px, py, c)).start()


def _gather_finish(n, view, sems):
    x, y, c, chips = _place()
    sibling = (x, y, 1 - c)
    for k in range(n):
        for j, (px, py) in enumerate(chips):
            theirs = view(k, 2 * px + py, c)
            _copy_to(theirs, sems[0], sems[1], k, j, (px, py, c)).wait_recv()
            _copy_to(theirs, sems[2], sems[3], k, j, sibling).start()
    for k in range(n):
        for j, (px, py) in enumerate(chips):
            _copy_to(view(k, 2 * px + py, 1 - c), sems[2], sems[3], k, j, sibling).wait_recv()
    for k in range(n):
        for j, (px, py) in enumerate(chips):
            _copy_to(view(k, 2 * x + y, c), sems[0], sems[1], k, j, (px, py, c)).wait_send()
            _copy_to(view(k, 2 * px + py, c), sems[2], sems[3], k, j, sibling).wait_send()


def _gather_layer_start(refs, axes, l, sems):
    _gather_start(len(refs), lambda k, s, half: _slab(refs[k], l, axes[k], s, half), sems)


def _gather_layer_finish(refs, axes, l, sems):
    _gather_finish(len(refs), lambda k, s, half: _slab(refs[k], l, axes[k], s, half), sems)


def gather_weights(bufs, axes, first_layer_only):
    n = len(bufs)
    lh = bufs[0].shape[0] // 2

    def body(*refs):
        dst = refs[n:2 * n]

        def view(k, s, half):
            if first_layer_only[k]:
                return _slab(dst[k], 0, axes[k], s, half)
            return _sub(dst[k], axes[k], s, dst[k].shape[axes[k]] // 4, half, lh)

        _gather_start(n, view, refs[2 * n:])
        _gather_finish(n, view, refs[2 * n:])

    return pl.pallas_call(
        body, name="gather_weights", out_shape=[_sds(b.shape, b.dtype) for b in bufs],
        in_specs=[HBM_SPEC] * n, out_specs=[HBM_SPEC] * n, input_output_aliases={k: k for k in range(n)},
        scratch_shapes=[pltpu.SemaphoreType.DMA((n, 3)) for _ in range(4)],
    )(*bufs)


def swap_halves(name, grads):
    n = len(grads)
    lh = grads[0].shape[0] // 2

    def body(*refs):
        src, dst = refs[:n], refs[n:2 * n]
        send, recv = refs[2 * n:]
        x, y, c, _ = _place()
        cps = []
        for k in range(n):
            cp = pltpu.make_async_remote_copy(
                src_ref=_sub(src[k], half=1 - c, lh=lh), dst_ref=dst[k], send_sem=send.at[k], recv_sem=recv.at[k],
                device_id=(x, y, 1 - c), device_id_type=MESH)
            cp.start()
            cps.append(cp)
        for cp in cps:
            cp.wait()

    return pl.pallas_call(
        body, name=name, out_shape=[_sds((lh,) + g.shape[1:], g.dtype) for g in grads],
        in_specs=[HBM_SPEC] * n, out_specs=[HBM_SPEC] * n,
        scratch_shapes=[pltpu.SemaphoreType.DMA((n,)), pltpu.SemaphoreType.DMA((n,))],
    )(*grads)


def exchange_chip_sums(name, sums_bf16, axes):
    n = len(sums_bf16)

    def body(*refs):
        sb, got = refs[:n], refs[n:2 * n]
        send, recv = refs[2 * n:]
        x, y, c, chips = _place()
        cps = []
        for k in range(n):
            ax = axes[k]
            ns = got[k].shape[1 + ax]
            for j, (px, py) in enumerate(chips):
                cp = pltpu.make_async_remote_copy(
                    src_ref=_sub(sb[k], ax, 2 * px + py, ns), dst_ref=got[k].at[j],
                    send_sem=send.at[k, j], recv_sem=recv.at[k, j], device_id=(px, py, c), device_id_type=MESH)
                cp.start()
                cps.append(cp)
        for cp in cps:
            cp.wait()

    out_shape = []
    for a, ax in zip(sums_bf16, axes):
        shp = list(a.shape)
        shp[ax] //= 4
        out_shape.append(_sds([3] + shp, BF16))
    return pl.pallas_call(
        body, name=name, out_shape=out_shape, in_specs=[HBM_SPEC] * n, out_specs=[HBM_SPEC] * n,
        scratch_shapes=[pltpu.SemaphoreType.DMA((n, 3)), pltpu.SemaphoreType.DMA((n, 3))],
    )(*sums_bf16)


def join_halves(name, bufs):
    n = len(bufs)
    lh = bufs[0].shape[0] // 2

    def body(*refs):
        dst = refs[n:2 * n]
        send, recv = refs[2 * n:]
        x, y, c, _ = _place()
        cps = []
        for k in range(n):
            mine = _sub(dst[k], half=c, lh=lh)
            cp = pltpu.make_async_remote_copy(
                src_ref=mine, dst_ref=mine, send_sem=send.at[k], recv_sem=recv.at[k],
                device_id=(x, y, 1 - c), device_id_type=MESH)
            cp.start()
            cps.append(cp)
        for k, cp in enumerate(cps):
            cp.wait_send()
            theirs = _sub(dst[k], half=1 - c, lh=lh)
            pltpu.make_async_remote_copy(
                src_ref=theirs, dst_ref=theirs, send_sem=send.at[k], recv_sem=recv.at[k],
                device_id=(x, y, 1 - c), device_id_type=MESH).wait_recv()

    return pl.pallas_call(
        body, name=name, out_shape=[_sds(b.shape, b.dtype) for b in bufs],
        in_specs=[HBM_SPEC] * n, out_specs=[HBM_SPEC] * n, input_output_aliases={k: k for k in range(n)},
        scratch_shapes=[pltpu.SemaphoreType.DMA((n,)), pltpu.SemaphoreType.DMA((n,))],
    )(*bufs)


def allreduce_small(name, packed):
    R = packed.shape[0]

    def body(x_ref, o_ref, buf, send, recv, loc):
        x, y, c, _ = _place()
        me = 4 * x + 2 * y + c
        cps = [pltpu.make_async_copy(x_ref, buf.at[me], loc)]
        cps[0].start()
        for d in range(1, 8):
            px, py, pc = x ^ (d >> 2), y ^ ((d >> 1) & 1), c ^ (d & 1)
            cp = pltpu.make_async_remote_copy(
                src_ref=x_ref, dst_ref=buf.at[me], send_sem=send.at[d - 1], recv_sem=recv.at[d - 1],
                device_id=(px, py, pc), device_id_type=MESH)
            cp.start()
            cps.append(cp)
        for cp in cps:
            cp.wait()
        acc = buf[0]
        for s in range(1, 8):
            acc = acc + buf[s]
        o_ref[...] = acc

    return pl.pallas_call(
        body, name=name, out_shape=_sds((R, LANES), F32),
        in_specs=[pl.BlockSpec(memory_space=pltpu.VMEM)], out_specs=pl.BlockSpec(memory_space=pltpu.VMEM),
        scratch_shapes=[pltpu.VMEM((8, R, LANES), F32), pltpu.SemaphoreType.DMA((7,)), pltpu.SemaphoreType.DMA((7,)),
                        pltpu.SemaphoreType.DMA],
        compiler_params=pltpu.CompilerParams(vmem_limit_bytes=VMEM_LIMIT),
    )(packed)


def _pack(arrs):
    flat = jnp.concatenate([a.reshape(-1) for a in arrs])
    pad = (-flat.shape[0]) % (8 * LANES)
    return jnp.pad(flat, (0, pad)).reshape(-1, LANES)


def _unpack(packed, like):
    flat = packed.reshape(-1)
    out, pos = [], 0
    for a in like:
        out.append(flat[pos:pos + a.size].reshape(a.shape))
        pos += a.size
    return out


def kernel(x, mem, mix_norm_g, mem_norm_g, w_in, gate_b, conv_w, conv_b, conv_ln_g, conv_ln_b, w_conv_out, rel_bias, w_attn_out, w_mem_kv, w_mem_out, w_o, ffn_norm_g, w_up, ffn_conv_w, ffn_conv_b, w_down, final_norm_g, loss_target, m_mix_norm_g, m_mem_norm_g, m_w_in, m_gate_b, m_conv_w, m_conv_b, m_conv_ln_g, m_conv_ln_b, m_w_conv_out, m_rel_bias, m_w_attn_out, m_w_mem_kv, m_w_mem_out, m_w_o, m_ffn_norm_g, m_w_up, m_ffn_conv_w, m_ffn_conv_b, m_w_down, m_final_norm_g, v_mix_norm_g, v_mem_norm_g, v_w_in, v_gate_b, v_conv_w, v_conv_b, v_conv_ln_g, v_conv_ln_b, v_w_conv_out, v_rel_bias, v_w_attn_out, v_w_mem_kv, v_w_mem_out, v_w_o, v_ffn_norm_g, v_w_up, v_ffn_conv_w, v_ffn_conv_b, v_w_down, v_final_norm_g):
    S, D = x.shape[1], x.shape[2]
    NM = mem.shape[1]
    L = w_in.shape[0]
    C = conv_b.shape[1]
    A = w_attn_out.shape[1]
    Dm = w_mem_out.shape[1]
    Fh = w_down.shape[1] * 4
    D_IN = w_in.shape[2] * 4
    CW = conv_w.shape[1]
    FW = ffn_conv_w.shape[1]
    H = A // ATTN_HEAD_DIM
    assert 2 * C == 2 * A == 2 * Dm == D and D_IN == 6 * D and S % QBLK == 0 and L % 2 == 0
    xi, yi, ci = lax.axis_index("x"), lax.axis_index("y"), lax.axis_index("c")
    chip = 2 * xi + yi
    place_sc = jnp.stack([chip, ci]).astype(jnp.int32)
    zero_sc = jnp.zeros((1,), jnp.int32)
    lh = L // 2

    big = [w_in, w_conv_out, w_attn_out, w_mem_kv, w_mem_out, w_o, w_up, w_down]
    big_m = [m_w_in, m_w_conv_out, m_w_attn_out, m_w_mem_kv, m_w_mem_out, m_w_o, m_w_up, m_w_down]
    big_v = [v_w_in, v_w_conv_out, v_w_attn_out, v_w_mem_kv, v_w_mem_out, v_w_o, v_w_up, v_w_down]
    big_ax = [2, 2, 2, 1, 2, 1, 2, 1]
    placed = []
    for k, (w, ax, dt) in enumerate(zip(big + [conv_w, ffn_conv_w], big_ax + [2, 2], [BF16] * len(big) + [F32, F32])):
        shp = list(w.shape)
        shp[ax] *= 4
        placed.append(_ew3(f"place_w{k}", lambda a: (a,), place_sc, w.shape, [(w, _at())],
                           [(_sds(shp, dt), _at(ax=ax, size=w.shape[ax]))])[0])
    gathered = gather_weights(placed, big_ax + [2, 2], [True] * len(big) + [False, False])
    wnames = ["in", "co", "ao", "kv", "mo", "o", "up", "dn"]
    Wd = dict(zip(wnames, gathered[:len(big)]))
    w_axis = dict(zip(wnames, big_ax))
    conv_w_f, ffn_conv_w_f = gathered[len(big):]
    hosted_by_in = ["up", "dn", "o"]
    hosted_by_up = ["in", "co", "ao", "kv", "mo"]

    r3 = lambda a: a.reshape(L, 1, a.shape[-1])
    mix_g3, mem_g3, ffn_g3, gb3 = r3(mix_norm_g), r3(mem_norm_g), r3(ffn_norm_g), r3(gate_b)
    cb3, lg3, lb3, fb3 = r3(conv_b), r3(conv_ln_g), r3(conv_ln_b), r3(ffn_conv_b)
    cw_p = jnp.pad(conv_w_f, ((0, 0), (0, CONV_HALO - CW), (0, 0)))
    fw_p = jnp.pad(ffn_conv_w_f, ((0, 0), (0, FFN_HALO - FW), (0, 0)))
    rel_p = jnp.pad(rel_bias, ((0, 0), (0, 0), (0, REL_PAD - rel_bias.shape[2])))

    x2, mem2, tgt2 = x[0], mem[0], loss_target[0]
    empty = lambda shape, dt: lax.empty(shape, dt)

    def layer_scalars(l):
        i32 = lambda *v: jnp.stack(v).astype(jnp.int32)
        return i32(l, 2 * l, 2 * l + 1), i32(l, 2 * l), i32(l, 2 * l + 1), i32(l, l)

    saved = dict(
        XN=empty((2 * L, S, D), BF16), HS=empty((2 * L, S, D), F32), PROJ=empty((L, S, D_IN), F32),
        HC=empty((L, S, C), F32), CACT=empty((L, S, C), BF16), AO=empty((L, S, A), BF16), MO=empty((L, S, Dm), BF16),
        Y=empty((L, 3, S, D), F32), MERGED=empty((L, S, D), BF16), UP=empty((L, S, 2 * Fh), F32),
        ACT=empty((L, S, Fh), BF16), MN=empty((L, NM, D), BF16), MEMS=empty((L, NM, D), F32),
        KV=empty((L, NM, 2 * Dm), F32), BIAS=empty((L, A // LANES, BANDP, LANES), F32))

    def fwd_layer(l, carry):
        h, sv, Wd = carry
        sv, Wd = dict(sv), dict(Wd)
        W_in, W_co, W_ao, W_kv, W_mo, W_o, W_up, W_dn = [Wd[nm] for nm in wnames]
        sc, sc_a, sc_f, sc_m = layer_scalars(l)
        sv["XN"], sv["HS"] = rms_fwd("rms_mix", sc_a, h, mix_g3, sv["XN"], sv["HS"])
        sv["PROJ"], *got = mm_nn("mm_in", sc_a, sv["XN"], L1, W_in, S, D, D_IN, sv["PROJ"], L0,
                                 gather=([Wd[nm] for nm in hosted_by_in], [w_axis[nm] for nm in hosted_by_in]))
        Wd.update(zip(hosted_by_in, got))
        W_o, W_dn = Wd["o"], Wd["dn"]
        sv["HC"], sv["CACT"] = conv_fwd("conv_fwd", sc, sv["PROJ"], cw_p, cb3, lg3, lb3, sv["HC"], sv["CACT"], C, CW)
        bias_q = bias_expand("bias_expand", sc, rel_p, H)
        sv["BIAS"] = lax.dynamic_update_slice(sv["BIAS"], bias_to_lanes(bias_q, A // LANES)[None], (l, 0, 0, 0))
        sv["AO"], = attn_fwd("attn_fwd", sc, sv["PROJ"], sv["BIAS"], sv["AO"], A, 2)
        sv["MN"], sv["MEMS"] = rms_fwd("rms_mem", sc_m, mem2, mem_g3, sv["MN"], sv["MEMS"])
        sv["KV"] = mm_nn("mm_kv", sc, sv["MN"], L0, W_kv, NM, D, 2 * Dm, sv["KV"], L0)
        sv["MO"], = mem_fwd("mem_fwd", sc, sv["PROJ"], sv["KV"], sv["MO"], Dm, 5)
        for b, (src, w, nm) in enumerate(((sv["CACT"], W_co, "mm_co"), (sv["AO"], W_ao, "mm_ao"), (sv["MO"], W_mo, "mm_mo"))):
            sv["Y"] = mm_nn(nm, sc, src, L0, w, S, C, D, sv["Y"], lambda s, b=b: (s[0], b))
        sv["MERGED"], = gate_fwd("gate_fwd", sc, sv["PROJ"], gb3, sv["Y"], sv["MERGED"], D)
        h2 = mm_nn("mm_o", sc, sv["MERGED"], L0, W_o, S, D, D, _sds((S, D), F32), add=h)
        sv["XN"], sv["HS"] = rms_fwd("rms_ffn", sc_f, h2, ffn_g3, sv["XN"], sv["HS"])
        sv["UP"], *got = mm_nn("mm_up", sc_f, sv["XN"], L1, Wd["up"], S, D, 2 * Fh, sv["UP"], L0,
                               gather=([Wd[nm] for nm in hosted_by_up], [w_axis[nm] for nm in hosted_by_up]))
        Wd.update(zip(hosted_by_up, got))
        sv["ACT"], = ffn_fwd("ffn_fwd", sc, sv["UP"], fw_p, fb3, sv["ACT"], Fh, FW)
        h3 = mm_nn("mm_down", sc, sv["ACT"], L0, W_dn, S, Fh, D, _sds((S, D), F32), add=h2)
        return h3, sv, Wd

    h_last, saved, Wd = lax.fori_loop(0, L, fwd_layer, (x2, saved, Wd))
    W_in, W_co, W_ao, W_kv, W_mo, W_o, W_up, W_dn = [Wd[nm] for nm in wnames]
    loss_t, dh, dhb, d_final_g = final_loss("final_loss", zero_sc, h_last, final_norm_g.reshape(1, D), tgt2)
    loss = lax.psum(loss_t[0, 0], ("x", "y", "c"))

    zeros = lambda shape: jnp.zeros(shape, F32)
    grads = dict(
        w_in=empty((L, D, D_IN), F32), w_conv_out=empty((L, C, D), F32), w_attn_out=empty((L, A, D), F32),
        w_mem_kv=empty((L, D, 2 * Dm), F32), w_mem_out=empty((L, Dm, D), F32), w_o=empty((L, D, D), F32),
        w_up=empty((L, D, 2 * Fh), F32), w_down=empty((L, Fh, D), F32),
        mix_g=zeros((L, 1, D)), mem_g=zeros((L, 1, D)), ffn_g=zeros((L, 1, D)), gate_b=zeros((L, 1, 3 * D)),
        conv_w=zeros((L, CONV_HALO, C)), conv_b=zeros((L, 1, C)), ln_g=zeros((L, 1, C)), ln_b=zeros((L, 1, C)),
        rel=zeros((L, H, REL_PAD)), fw_v=zeros((L, FFN_HALO, Fh)), fw_g=zeros((L, FFN_HALO, Fh)),
        fb_v=zeros((L, 1, Fh)), fb_g=zeros((L, 1, Fh)))
    zero_mem = jnp.zeros((NM, D), F32)

    def bwd_layer(it, carry):
        dh, dhb, g = carry
        g = dict(g)
        l = L - 1 - it
        sc, sc_a, sc_f, sc_m = layer_scalars(l)
        sv = saved
        dact = mm_nt("mm_down_dx", sc, [(dhb, None)], W_dn, S, Fh, _sds((S, Fh), F32))
        g["w_down"] = mm_tn("mm_down_dw", sc, sv["ACT"], L0, dhb, None, S, Fh, D, g["w_down"], L0)
        dupv, dupg, g["fw_v"], g["fw_g"], g["fb_v"], g["fb_g"] = ffn_bwd(
            "ffn_bwd", sc, sv["UP"], dact, fw_p, fb3, g["fw_v"], g["fw_g"], g["fb_v"], g["fb_g"], Fh, FW)
        dhn = mm_nt("mm_up_dx", sc, [(dupv, None), (dupg, None)], W_up, S, D, _sds((S, D), F32))
        g["w_up"] = mm_tn("mm_up_dw_v", sc_f, sv["XN"], L1, dupv, None, S, D, Fh, g["w_up"], L0)
        g["w_up"] = mm_tn("mm_up_dw_g", sc_f, sv["XN"], L1, dupg, None, S, D, Fh, g["w_up"], L0, out_joff=Fh)
        dh2, dh2b, g["ffn_g"] = rms_bwd("rms_ffn_bwd", sc_f, sv["HS"], ffn_g3, dhn, dh, g["ffn_g"])
        dmerged = mm_nt("mm_o_dx", sc, [(dh2b, None)], W_o, S, D, _sds((S, D), F32))
        g["w_o"] = mm_tn("mm_o_dw", sc, sv["MERGED"], L0, dh2b, None, S, D, D, g["w_o"], L0)
        dy, dgates, g["gate_b"] = gate_bwd("gate_bwd", sc, sv["PROJ"], gb3, sv["Y"], dmerged, g["gate_b"], D)
        dcact = mm_nt("mm_co_dx", sc, [(dy, lambda s: (0,))], W_co, S, C, _sds((S, C), F32))
        dao = mm_nt("mm_ao_dx", sc, [(dy, lambda s: (1,))], W_ao, S, A, _sds((S, A), BF16), out_dtype=BF16)
        dmo = mm_nt("mm_mo_dx", sc, [(dy, lambda s: (2,))], W_mo, S, Dm, _sds((S, Dm), BF16), out_dtype=BF16)
        g["w_conv_out"] = mm_tn("mm_co_dw", sc, sv["CACT"], L0, dy, lambda s: (0,), S, C, D, g["w_conv_out"], L0)
        g["w_attn_out"] = mm_tn("mm_ao_dw", sc, sv["AO"], L0, dy, lambda s: (1,), S, A, D, g["w_attn_out"], L0)
        g["w_mem_out"] = mm_tn("mm_mo_dw", sc, sv["MO"], L0, dy, lambda s: (2,), S, Dm, D, g["w_mem_out"], L0)
        dhc, g["ln_g"], g["ln_b"], g["conv_b"] = conv_bwd1(
            "conv_bwd1", sc, sv["HC"], dcact, lg3, lb3, g["ln_g"], g["ln_b"], g["conv_b"])
        dconv, g["conv_w"] = conv_bwd2("conv_bwd2", sc, sv["PROJ"], dhc, cw_p, g["conv_w"], C, CW)
        dq, dk, dv, dbias = attn_bwd("attn_bwd", sc, sv["PROJ"], sv["BIAS"], dao, A, 2)
        g["rel"] = bias_reduce("bias_reduce", sc, bias_from_lanes(dbias, A // LANES), g["rel"], H)
        dqm, dkv = mem_bwd("mem_bwd", sc, sv["PROJ"], sv["KV"], dmo, Dm, 5)
        g["w_mem_kv"] = mm_tn("mm_kv_dw", sc, sv["MN"], L0, dkv, None, NM, D, 2 * Dm, g["w_mem_kv"], L0)
        dmn = mm_nt("mm_kv_dx", sc, [(dkv, None)], W_kv, NM, D, _sds((NM, D), F32))
        _, _, g["mem_g"] = rms_bwd("rms_mem_bwd", sc_m, sv["MEMS"], mem_g3, dmn, zero_mem, g["mem_g"])
        pieces = [(dconv, None), (dq, None), (dk, None), (dv, None), (dqm, None), (dgates, None)]
        dxn = mm_nt("mm_in_dx", sc, pieces, W_in, S, D, _sds((S, D), F32), tk=1024)
        off = 0
        for nm, (p, _) in zip(("c", "q", "k", "v", "m", "g"), pieces):
            g["w_in"] = mm_tn("mm_in_dw_" + nm, sc_a, sv["XN"], L1, p, None, S, D, p.shape[1], g["w_in"], L0, out_joff=off)
            off += p.shape[1]
        dh0, dh0b, g["mix_g"] = rms_bwd("rms_mix_bwd", sc_a, sv["HS"], mix_g3, dxn, dh2, g["mix_g"])
        return dh0, dh0b, g

    grad_x2, _, grads = lax.fori_loop(0, L, bwd_layer, (dh, dhb, grads))

    names = ["w_in", "w_conv_out", "w_attn_out", "w_mem_kv", "w_mem_out", "w_o", "w_up", "w_down"]
    gl = [grads[nm] for nm in names]
    from_sib = swap_halves("swap_halves", gl)
    my_layer = lambda g_, s: (s[1] * lh + g_,)
    sums_f, sums_b = [], []
    for k, (g_, r_) in enumerate(zip(gl, from_sib)):
        s_f, s_b = _ew3(f"sum_sib{k}", lambda a, b: (a + b, a + b), place_sc, r_.shape,
                        [(g_, _at(my_layer)), (r_, _at())], [(_sds(r_.shape, F32), _at()), (_sds(r_.shape, BF16), _at())])
        sums_f.append(s_f)
        sums_b.append(s_b)
    got = exchange_chip_sums("exchange_chip_sums", sums_b, big_ax)
    halves = []
    for k, (s_f, r_, w, ax) in enumerate(zip(sums_f, got, big, big_ax)):
        fin, = _ew3(f"sum_chips{k}", lambda a, b0, b1, b2: (((a + b0.astype(F32)) + b1.astype(F32)) + b2.astype(F32),),
                    place_sc, (lh,) + w.shape[1:],
                    [(s_f, _at(ax=ax, size=w.shape[ax]))] + [(r_, _at(lambda g_, s, j=j: (j, g_))) for j in range(3)],
                    [(_sds(w.shape, F32), _at(my_layer))])
        halves.append(fin)
    big_g = join_halves("join_halves", halves)
    big_d, big_nm, big_nv = [], [], []
    for k in range(len(big)):
        d_, m_, v_ = adamw(f"adamw_big{k}", big[k], big_g[k], big_m[k], big_v[k])
        big_d.append(d_), big_nm.append(m_), big_nv.append(v_)

    g_small_full = [
        grads["mix_g"].reshape(L, D), grads["mem_g"].reshape(L, D), grads["gate_b"].reshape(L, 3 * D),
        grads["conv_w"][:, :CW, :], grads["conv_b"].reshape(L, C), grads["ln_g"].reshape(L, C), grads["ln_b"].reshape(L, C),
        grads["rel"][:, :, :rel_bias.shape[2]], grads["ffn_g"].reshape(L, D),
        jnp.concatenate([grads["fw_v"][:, :FW, :], grads["fw_g"][:, :FW, :]], axis=-1),
        jnp.concatenate([grads["fb_v"], grads["fb_g"]], axis=-1).reshape(L, 2 * Fh), d_final_g.reshape(D)]
    summed = _unpack(allreduce_small("allreduce_small", _pack(g_small_full)), g_small_full)
    cws, fws = conv_w.shape[2], ffn_conv_w.shape[2]
    summed[3] = lax.dynamic_slice_in_dim(summed[3], chip * cws, cws, axis=2)
    summed[9] = lax.dynamic_slice_in_dim(summed[9], chip * fws, fws, axis=2)
    small_w = [mix_norm_g, mem_norm_g, gate_b, conv_w, conv_b, conv_ln_g, conv_ln_b, rel_bias, ffn_norm_g, ffn_conv_w, ffn_conv_b, final_norm_g]
    small_m = [m_mix_norm_g, m_mem_norm_g, m_gate_b, m_conv_w, m_conv_b, m_conv_ln_g, m_conv_ln_b, m_rel_bias, m_ffn_norm_g, m_ffn_conv_w, m_ffn_conv_b, m_final_norm_g]
    small_v = [v_mix_norm_g, v_mem_norm_g, v_gate_b, v_conv_w, v_conv_b, v_conv_ln_g, v_conv_ln_b, v_rel_bias, v_ffn_norm_g, v_ffn_conv_w, v_ffn_conv_b, v_final_norm_g]
    sd, sm, sv_ = _ew("adamw_small", _adamw_math, [_pack(small_w), _pack(summed), _pack(small_m), _pack(small_v)], (F32, F32, F32))
    small_d, small_nm, small_nv = _unpack(sd, small_w), _unpack(sm, small_w), _unpack(sv_, small_w)

    order = ["mix_norm_g", "mem_norm_g", "w_in", "gate_b", "conv_w", "conv_b", "conv_ln_g", "conv_ln_b", "w_conv_out",
             "rel_bias", "w_attn_out", "w_mem_kv", "w_mem_out", "w_o", "ffn_norm_g", "w_up", "ffn_conv_w", "ffn_conv_b",
             "w_down", "final_norm_g"]
    small_names = ["mix_norm_g", "mem_norm_g", "gate_b", "conv_w", "conv_b", "conv_ln_g", "conv_ln_b", "rel_bias",
                   "ffn_norm_g", "ffn_conv_w", "ffn_conv_b", "final_norm_g"]

    def collect(bigs, smalls):
        table = dict(zip(names, bigs))
        table.update(zip(small_names, smalls))
        return [table[nm] for nm in order]

    return (loss, grad_x2[None], *collect(big_g, summed), *collect(big_d, small_d), *collect(big_nm, small_nm),
            *collect(big_nv, small_nv))
```

```python
import jax
import jax.numpy as jnp
from jax import lax
from jax.experimental import pallas as pl
from jax.experimental.pallas import tpu as pltpu

F32 = jnp.float32
BF16 = jnp.bfloat16
MESH = pl.DeviceIdType.MESH

CHUNK = 64
LEFT_CHUNKS = 8
BAND = (LEFT_CHUNKS + 1) * CHUNK
BANDP = BAND + CHUNK
MAX_REL = 256
REL_PAD = 640
ATTN_HEAD_DIM = 64
N_MEM_HEADS = 4
QBLK = LEFT_CHUNKS * CHUNK
EPS = 1e-6
NEG_INF = -1e30
LANES = 128
VMEM_LIMIT = 56 * 1024 * 1024

ADAM_LR = 0.001
ADAM_B1 = 0.9
ADAM_B2 = 0.999
ADAM_EPS = 1e-08
ADAM_WD = 0.01
ADAM_STEP = 10


def _pick(dim, target, unit=LANES):
    if dim <= target:
        return dim
    d = (target // unit) * unit
    while d >= unit:
        if dim % d == 0:
            return d
        d -= unit
    raise ValueError(f"no tile for {dim} under {target}")


def _sigmoid(x):
    return 1.0 / (1.0 + jnp.exp(-x))


def _pcall(body, *, name, grid, sc, ins, outs, scratch=(), sem=None):
    arrays = [a for a, _ in ins]
    in_specs = [s for _, s in ins]
    n_in = len(arrays)
    out_shape, out_specs, aliases = [], [], {}
    for k, (o, spec) in enumerate(outs):
        if isinstance(o, jax.ShapeDtypeStruct):
            out_shape.append(o)
        else:
            aliases[1 + len(arrays)] = k
            arrays.append(o)
            in_specs.append(pl.BlockSpec(memory_space=pl.ANY))
            out_shape.append(jax.ShapeDtypeStruct(o.shape, o.dtype))
        out_specs.append(spec)
    n_alias = len(arrays) - n_in

    def wrapped(sc_ref, *refs):
        body(sc_ref, *refs[:n_in], *refs[n_in + n_alias:])

    res = pl.pallas_call(
        wrapped,
        name=name,
        grid_spec=pltpu.PrefetchScalarGridSpec(
            num_scalar_prefetch=1, grid=grid, in_specs=in_specs, out_specs=out_specs, scratch_shapes=list(scratch)),
        out_shape=out_shape,
        input_output_aliases=aliases,
        compiler_params=pltpu.CompilerParams(
            dimension_semantics=sem or ("arbitrary",) * len(grid), vmem_limit_bytes=VMEM_LIMIT),
    )(sc, *arrays)
    return res


def _bs(arr, blk, rc, lead=None):
    nlead = len(arr.shape) - 2

    def imap(*ids):
        sc = ids[-1]
        r, c = rc(*ids)
        return (*(lead(sc) if nlead else ()), r, c)

    return pl.BlockSpec((None,) * nlead + tuple(blk), imap)


def _sds(shape, dtype):
    return jax.ShapeDtypeStruct(tuple(shape), dtype)


L0 = lambda sc: (sc[0],)
L1 = lambda sc: (sc[1],)


def _mm_core(name, grid, sc, a_items, b_item, add_item, out_item, dims, ranges, out_dtype, acc_shape, gather=None):
    nk = grid[2]
    na = len(a_items)
    has_add = add_item is not None
    ng = len(gather[0]) if gather else 0

    def body(sc_ref, *refs):
        a_refs = refs[:na]
        b_ref = refs[na]
        pos = na + 1
        add_ref = refs[pos] if has_add else None
        pos += int(has_add)
        o_ref = refs[pos]
        g_refs = refs[pos + 1:pos + 1 + ng]
        pos += 1 + ng
        acc = refs[pos] if nk > 1 else None
        k = pl.program_id(2)
        if ng:
            sems = refs[pos + int(nk > 1):]
            at = [pl.program_id(d) for d in range(3)]
            first = (at[0] == 0) & (at[1] == 0) & (at[2] == 0)
            last = (at[0] == grid[0] - 1) & (at[1] == grid[1] - 1) & (at[2] == grid[2] - 1)
            nxt = sc_ref[0] + 1
            more = nxt < g_refs[0].shape[0]

            @pl.when(first & more)
            def _():
                _gather_layer_start(g_refs, gather[1], nxt, sems)

        def contrib(ar):
            return lax.dot_general(ar[...].astype(BF16), b_ref[...].astype(BF16), dims, preferred_element_type=F32)

        def fin(val):
            if has_add:
                val = val + add_ref[...]
            o_ref[...] = val.astype(out_dtype)

        if nk == 1:
            fin(contrib(a_refs[0]))
        for p, (k0, k1) in enumerate(ranges if nk > 1 else []):
            @pl.when((k >= k0) & (k < k1))
            def _(p=p):
                part = contrib(a_refs[p])

                @pl.when(k == 0)
                def _():
                    acc[...] = part

                @pl.when((k > 0) & (k < nk - 1))
                def _():
                    acc[...] += part

                @pl.when(k == nk - 1)
                def _():
                    fin(acc[...] + part)

        if ng:
            @pl.when(last & more)
            def _():
                _gather_layer_finish(g_refs, gather[1], nxt, sems)

    ins = list(a_items) + [b_item] + ([add_item] if has_add else [])
    scratch = [pltpu.VMEM(acc_shape, F32)] if nk > 1 else []
    outs = [out_item]
    if ng:
        outs += [(b, HBM_SPEC) for b in gather[0]]
        scratch += [pltpu.SemaphoreType.DMA((ng, 3)) for _ in range(4)]
    res = _pcall(body, name=name, grid=grid, sc=sc, ins=ins, outs=outs, scratch=scratch,
                 sem=("arbitrary",) * 3 if ng else ("parallel", "parallel", "arbitrary"))
    return res if ng else res[0]


def mm_nn(name, sc, a, a_lead, w, M, K, N, out, out_lead=None, out_joff=0, add=None, out_dtype=F32,
          tm=1024, tn=1408, tk=2048, gather=None):
    tm, tn, tk = _pick(M, tm, 8), _pick(N, tn), _pick(K, tk)
    nk = K // tk
    a_item = (a, _bs(a, (tm, tk), lambda i, j, k, sc: (i, k), a_lead))
    b_item = (w, _bs(w, (tk, tn), lambda i, j, k, sc: (k, j), L0))
    add_item = None if add is None else (add, _bs(add, (tm, tn), lambda i, j, k, sc: (i, j)))
    out_spec = _bs(out, (tm, tn), lambda i, j, k, sc: (i, j + out_joff // tn), out_lead)
    return _mm_core(name, (M // tm, N // tn, nk), sc, [a_item], b_item, add_item, (out, out_spec),
                    (((1,), (0,)), ((), ())), [(0, nk)], out_dtype, (tm, tn), gather=gather)


def mm_nt(name, sc, pieces, w, M, N, out, w_koff=0, out_dtype=F32, tm=1024, tn=1408, tk=2048):
    tm, tn = _pick(M, tm, 8), _pick(N, tn)
    widths = [a.shape[-1] for a, _ in pieces]
    tk = _pick(widths[0], tk)
    offs = [w_koff]
    for wd in widths:
        offs.append(offs[-1] + wd)
    while any(x % tk for x in offs):
        tk = _pick(widths[0], tk - LANES)
    ranges, a_items, k0 = [], [], 0
    for (a, lead), wd in zip(pieces, widths):
        n = wd // tk
        ranges.append((k0, k0 + n))
        a_items.append((a, _bs(a, (tm, tk), lambda i, j, k, sc, k0=k0, n=n: (i, jnp.clip(k - k0, 0, n - 1)), lead)))
        k0 += n
    nk = k0
    b_item = (w, _bs(w, (tn, tk), lambda i, j, k, sc: (j, k + w_koff // tk), L0))
    out_spec = _bs(out, (tm, tn), lambda i, j, k, sc: (i, j))
    return _mm_core(name, (M // tm, N // tn, nk), sc, a_items, b_item, None, (out, out_spec),
                    (((1,), (1,)), ((), ())), ranges, out_dtype, (tm, tn))


def mm_tn(name, sc, a, a_lead, b, b_lead, S, K, N, out, out_lead, out_joff=0, tm=1408, tn=1408, tk=2048):
    tm, tn, tk = _pick(K, tm), _pick(N, tn), _pick(S, tk, 8)
    while out_joff % tn:
        tn = _pick(N, tn - LANES)
    nk = S // tk
    a_item = (a, _bs(a, (tk, tm), lambda i, j, k, sc: (k, i), a_lead))
    b_item = (b, _bs(b, (tk, tn), lambda i, j, k, sc: (k, j), b_lead))
    out_spec = _bs(out, (tm, tn), lambda i, j, k, sc: (i, j + out_joff // tn), out_lead)
    return _mm_core(name, (K // tm, N // tn, nk), sc, [a_item], b_item, None, (out, out_spec),
                    (((0,), (0,)), ((), ())), [(0, nk)], F32, (tm, tn))


def rms_fwd(name, sc, h, g3, xn_buf, hs_buf):
    S, D = h.shape
    tr = _pick(S, 256, 8)

    def body(sc_ref, h_ref, g_ref, xn_ref, hs_ref):
        x = h_ref[...]
        r = lax.rsqrt(jnp.mean(x * x, axis=-1, keepdims=True) + EPS)
        xn_ref[...] = (x * r * g_ref[...]).astype(BF16)
        hs_ref[...] = x

    return _pcall(body, name=name, grid=(S // tr,), sc=sc,
                  ins=[(h, _bs(h, (tr, D), lambda i, sc: (i, 0))), (g3, _bs(g3, (1, D), lambda i, sc: (0, 0), L0))],
                  outs=[(xn_buf, _bs(xn_buf, (tr, D), lambda i, sc: (i, 0), L1)),
                        (hs_buf, _bs(hs_buf, (tr, D), lambda i, sc: (i, 0), L1))])


def rms_bwd(name, sc, hs_buf, g3, dy, dres, dg_buf):
    S, D = dy.shape
    tr = _pick(S, 256, 8)

    def body(sc_ref, x_ref, g_ref, dy_ref, dres_ref, dx_ref, dxb_ref, dg_ref):
        i = pl.program_id(0)
        x = x_ref[...]
        r = lax.rsqrt(jnp.mean(x * x, axis=-1, keepdims=True) + EPS)
        xh = x * r
        dyv = dy_ref[...]
        dxh = dyv * g_ref[...]
        dx = r * (dxh - xh * jnp.mean(dxh * xh, axis=-1, keepdims=True)) + dres_ref[...]
        dx_ref[...] = dx
        dxb_ref[...] = dx.astype(BF16)
        part = jnp.sum(dyv * xh, axis=0, keepdims=True)

        @pl.when(i == 0)
        def _():
            dg_ref[...] = part

        @pl.when(i > 0)
        def _():
            dg_ref[...] += part

    return _pcall(body, name=name, grid=(S // tr,), sc=sc,
                  ins=[(hs_buf, _bs(hs_buf, (tr, D), lambda i, sc: (i, 0), L1)),
                       (g3, _bs(g3, (1, D), lambda i, sc: (0, 0), L0)),
                       (dy, _bs(dy, (tr, D), lambda i, sc: (i, 0))),
                       (dres, _bs(dres, (tr, D), lambda i, sc: (i, 0)))],
                  outs=[(_sds((S, D), F32), pl.BlockSpec((tr, D), lambda i, sc: (i, 0))),
                        (_sds((S, D), BF16), pl.BlockSpec((tr, D), lambda i, sc: (i, 0))),
                        (dg_buf, _bs(dg_buf, (1, D), lambda i, sc: (0, 0), L0))])


CONV_HALO = 32
CONV_RB = 32


def _rotations(buf, rots, n):
    for b in range(1, 8):
        rots[b - 1, 0:n, :] = buf[b:b + n, :]


def _shifted(buf, rots, s, r, n):
    b = s % 8
    if b == 0:
        return buf[r + s:r + s + n, :]
    return rots[b - 1, r + s - b:r + s - b + n, :]


def conv_fwd(name, sc, proj, cw, cb, lg, lb, hc_buf, cact_buf, C, W):
    S = proj.shape[1]
    T = _pick(S, 256, CONV_HALO)
    nh = T // CONV_HALO
    off = CONV_HALO - (W - 1)

    def body(sc_ref, ua_ref, ug_ref, pa_ref, pg_ref, w_ref, b_ref, lg_ref, lb_ref, hc_ref, ca_ref, buf, rots):
        i = pl.program_id(0)
        halo = pa_ref[...] * _sigmoid(pg_ref[...])
        buf[0:CONV_HALO, :] = jnp.where(i > 0, halo, 0.0)
        buf[CONV_HALO:CONV_HALO + T, :] = ua_ref[...] * _sigmoid(ug_ref[...])
        _rotations(buf, rots, T + CONV_HALO - 8)
        for r in range(0, T, CONV_RB):
            acc = jnp.zeros((CONV_RB, C), F32) + b_ref[...]
            for k in range(W):
                acc = acc + w_ref[k:k + 1, :] * _shifted(buf, rots, off + k, r, CONV_RB)
            hc_ref[r:r + CONV_RB, :] = acc
        hc = hc_ref[...]
        mu = jnp.mean(hc, axis=-1, keepdims=True)
        d = hc - mu
        rstd = lax.rsqrt(jnp.mean(d * d, axis=-1, keepdims=True) + EPS)
        y = d * rstd * lg_ref[...] + lb_ref[...]
        ca_ref[...] = (y * _sigmoid(y)).astype(BF16)

    vec = lambda a: (a, _bs(a, (1, C), lambda i, sc: (0, 0), L0))
    return _pcall(body, name=name, grid=(S // T,), sc=sc,
                  ins=[(proj, _bs(proj, (T, C), lambda i, sc: (i, 0), L0)),
                       (proj, _bs(proj, (T, C), lambda i, sc: (i, 1), L0)),
                       (proj, _bs(proj, (CONV_HALO, C), lambda i, sc: (jnp.maximum(i * nh - 1, 0), 0), L0)),
                       (proj, _bs(proj, (CONV_HALO, C), lambda i, sc: (jnp.maximum(i * nh - 1, 0), 1), L0)),
                       (cw, _bs(cw, (CONV_HALO, C), lambda i, sc: (0, 0), L0)), vec(cb), vec(lg), vec(lb)],
                  outs=[(hc_buf, _bs(hc_buf, (T, C), lambda i, sc: (i, 0), L0)),
                        (cact_buf, _bs(cact_buf, (T, C), lambda i, sc: (i, 0), L0))],
                  scratch=[pltpu.VMEM((T + CONV_HALO, C), F32), pltpu.VMEM((7, T + CONV_HALO, C), F32)])


def conv_bwd1(name, sc, hc_buf, dcact, lg, lb, dlg_buf, dlb_buf, dcb_buf):
    S, C = dcact.shape
    tr = _pick(S, 256, 8)

    def body(sc_ref, hc_ref, dc_ref, lg_ref, lb_ref, dhc_ref, dlg_ref, dlb_ref, dcb_ref):
        i = pl.program_id(0)
        hc = hc_ref[...]
        mu = jnp.mean(hc, axis=-1, keepdims=True)
        d = hc - mu
        rstd = lax.rsqrt(jnp.mean(d * d, axis=-1, keepdims=True) + EPS)
        yh = d * rstd
        y = yh * lg_ref[...] + lb_ref[...]
        sg = _sigmoid(y)
        dy = dc_ref[...] * (sg * (1.0 + y * (1.0 - sg)))
        dyh = dy * lg_ref[...]
        dhc = rstd * (dyh - jnp.mean(dyh, axis=-1, keepdims=True) - yh * jnp.mean(dyh * yh, axis=-1, keepdims=True))
        dhc_ref[...] = dhc
        parts = (jnp.sum(dy * yh, axis=0, keepdims=True), jnp.sum(dy, axis=0, keepdims=True),
                 jnp.sum(dhc, axis=0, keepdims=True))

        @pl.when(i == 0)
        def _():
            dlg_ref[...], dlb_ref[...], dcb_ref[...] = parts

        @pl.when(i > 0)
        def _():
            dlg_ref[...] += parts[0]
            dlb_ref[...] += parts[1]
            dcb_ref[...] += parts[2]

    vec = lambda a: (a, _bs(a, (1, C), lambda i, sc: (0, 0), L0))
    return _pcall(body, name=name, grid=(S // tr,), sc=sc,
                  ins=[(hc_buf, _bs(hc_buf, (tr, C), lambda i, sc: (i, 0), L0)),
                       (dcact, _bs(dcact, (tr, C), lambda i, sc: (i, 0))), vec(lg), vec(lb)],
                  outs=[(_sds((S, C), F32), pl.BlockSpec((tr, C), lambda i, sc: (i, 0))),
                        vec(dlg_buf), vec(dlb_buf), vec(dcb_buf)])


def conv_bwd2(name, sc, proj, dhc, cw, dcw_buf, C, W):
    S = proj.shape[1]
    T = _pick(S, 256, CONV_HALO)
    nh = T // CONV_HALO
    nt = S // T
    off = CONV_HALO - (W - 1)

    def body(sc_ref, ua_ref, ug_ref, pa_ref, pg_ref, d_ref, dn_ref, w_ref, o_ref, dw_ref, buf, dbuf, dhg, rots, drots):
        i = pl.program_id(0)
        halo = pa_ref[...] * _sigmoid(pg_ref[...])
        buf[0:CONV_HALO, :] = jnp.where(i > 0, halo, 0.0)
        sg = _sigmoid(ug_ref[...])
        ua = ua_ref[...]
        buf[CONV_HALO:CONV_HALO + T, :] = ua * sg
        dbuf[0:T, :] = d_ref[...]
        dbuf[T:T + CONV_HALO, :] = jnp.where(i < nt - 1, dn_ref[...], 0.0)
        _rotations(buf, rots, T + CONV_HALO - 8)
        _rotations(dbuf, drots, T + CONV_HALO - 8)

        @pl.when(i == 0)
        def _():
            dw_ref[...] = jnp.zeros_like(dw_ref)

        for r in range(0, T, CONV_RB):
            acc = jnp.zeros((CONV_RB, C), F32)
            for k in range(W):
                acc = acc + w_ref[k:k + 1, :] * _shifted(dbuf, drots, (W - 1) - k, r, CONV_RB)
            dhg[r:r + CONV_RB, :] = acc
        for k in range(W):
            dw_ref[k:k + 1, :] += jnp.sum(dbuf[0:T, :] * _shifted(buf, rots, off + k, 0, T), axis=0, keepdims=True)
        dh = dhg[...]
        o_ref[:, 0:C] = (dh * sg).astype(BF16)
        o_ref[:, C:2 * C] = (dh * ua * sg * (1.0 - sg)).astype(BF16)

    return _pcall(body, name=name, grid=(nt,), sc=sc,
                  ins=[(proj, _bs(proj, (T, C), lambda i, sc: (i, 0), L0)),
                       (proj, _bs(proj, (T, C), lambda i, sc: (i, 1), L0)),
                       (proj, _bs(proj, (CONV_HALO, C), lambda i, sc: (jnp.maximum(i * nh - 1, 0), 0), L0)),
                       (proj, _bs(proj, (CONV_HALO, C), lambda i, sc: (jnp.maximum(i * nh - 1, 0), 1), L0)),
                       (dhc, _bs(dhc, (T, C), lambda i, sc: (i, 0))),
                       (dhc, _bs(dhc, (CONV_HALO, C), lambda i, sc: (jnp.minimum((i + 1) * nh, S // CONV_HALO - 1), 0))),
                       (cw, _bs(cw, (CONV_HALO, C), lambda i, sc: (0, 0), L0))],
                  outs=[(_sds((S, 2 * C), BF16), pl.BlockSpec((T, 2 * C), lambda i, sc: (i, 0))),
                        (dcw_buf, _bs(dcw_buf, (CONV_HALO, C), lambda i, sc: (0, 0), L0))],
                  scratch=[pltpu.VMEM((T + CONV_HALO, C), F32), pltpu.VMEM((T + CONV_HALO, C), F32),
                           pltpu.VMEM((T, C), F32), pltpu.VMEM((7, T + CONV_HALO, C), F32),
                           pltpu.VMEM((7, T + CONV_HALO, C), F32)])


def gate_fwd(name, sc, proj, gb3, y_buf, merged_buf, D):
    S = proj.shape[1]
    tr, tc = _pick(S, 256, 8), _pick(D, 1024)
    nc = D // tc

    def body(sc_ref, g0, g1, g2, b0, b1, b2, y0, y1, y2, o_ref):
        acc = None
        for g, b, y in ((g0, b0, y0), (g1, b1, y1), (g2, b2, y2)):
            t = _sigmoid(g[...] + b[...]) * y[...]
            acc = t if acc is None else acc + t
        o_ref[...] = acc.astype(BF16)

    ins = [(proj, _bs(proj, (tr, tc), lambda i, j, sc, b=b: (i, (3 + b) * nc + j), L0)) for b in range(3)]
    ins += [(gb3, _bs(gb3, (1, tc), lambda i, j, sc, b=b: (0, b * nc + j), L0)) for b in range(3)]
    ins += [(y_buf, _bs(y_buf, (tr, tc), lambda i, j, sc: (i, j), lambda sc, b=b: (sc[0], b))) for b in range(3)]
    return _pcall(body, name=name, grid=(S // tr, nc), sc=sc, ins=ins,
                  outs=[(merged_buf, _bs(merged_buf, (tr, tc), lambda i, j, sc: (i, j), L0))])


def gate_bwd(name, sc, proj, gb3, y_buf, dmerged, dgb_buf, D):
    S = proj.shape[1]
    tr, tc = _pick(S, 256, 8), _pick(D, 1024)
    nc = D // tc

    def body(sc_ref, g_ref, b_ref, y_ref, dm_ref, dy_ref, dg_ref, dgb_ref):
        i = pl.program_id(2)
        g = _sigmoid(g_ref[...] + b_ref[...])
        dm = dm_ref[...]
        dy_ref[...] = (dm * g).astype(BF16)
        dgt = dm * y_ref[...] * g * (1.0 - g)
        dg_ref[...] = dgt.astype(BF16)
        part = jnp.sum(dgt, axis=0, keepdims=True)

        @pl.when(i == 0)
        def _():
            dgb_ref[...] = part

        @pl.when(i > 0)
        def _():
            dgb_ref[...] += part

    dy_sds = _sds((3, S, D), BF16)
    return _pcall(body, name=name, grid=(3, nc, S // tr), sc=sc,
                  ins=[(proj, _bs(proj, (tr, tc), lambda b, j, i, sc: (i, (3 + b) * nc + j), L0)),
                       (gb3, _bs(gb3, (1, tc), lambda b, j, i, sc: (0, b * nc + j), L0)),
                       (y_buf, pl.BlockSpec((None, None, tr, tc), lambda b, j, i, sc: (sc[0], b, i, j))),
                       (dmerged, _bs(dmerged, (tr, tc), lambda b, j, i, sc: (i, j)))],
                  outs=[(dy_sds, pl.BlockSpec((None, tr, tc), lambda b, j, i, sc: (b, i, j))),
                        (_sds((S, 3 * D), BF16), pl.BlockSpec((tr, tc), lambda b, j, i, sc: (i, b * nc + j))),
                        (dgb_buf, _bs(dgb_buf, (1, tc), lambda b, j, i, sc: (0, b * nc + j), L0))])


FFN_HALO = 8
FFN_RB = 16


def _taps(buf, w_ref, init, start, n, W, reverse=False):
    acc = init
    for k in range(W):
        s = start + ((W - 1 - k) if reverse else k)
        acc = acc + w_ref[k:k + 1, :] * buf[s:s + n, :]
    return acc


def ffn_fwd(name, sc, up, fw, fb3, act_buf, Fh, W):
    S = up.shape[1]
    T, tc = _pick(S, 512, FFN_RB), _pick(Fh, 512)
    nf = Fh // tc
    nh = T // FFN_HALO
    off = FFN_HALO - (W - 1)

    def body(sc_ref, v_ref, g_ref, pv_ref, pg_ref, wv_ref, wg_ref, bv_ref, bg_ref, o_ref, bufv, bufg):
        i = pl.program_id(1)
        for m_ref, p_ref, buf in ((v_ref, pv_ref, bufv), (g_ref, pg_ref, bufg)):
            buf[0:FFN_HALO, :] = jnp.where(i > 0, p_ref[...], 0.0)
            buf[FFN_HALO:FFN_HALO + T, :] = m_ref[...]
        for r in range(0, T, FFN_RB):
            val = _taps(bufv, wv_ref, bv_ref[...], r + off, FFN_RB, W)
            gt = _taps(bufg, wg_ref, bg_ref[...], r + off, FFN_RB, W)
            o_ref[r:r + FFN_RB, :] = (gt * _sigmoid(gt) * val).astype(BF16)

    prev = lambda i: jnp.maximum(i * nh - 1, 0)
    return _pcall(body, name=name, grid=(nf, S // T), sc=sc,
                  ins=[(up, _bs(up, (T, tc), lambda j, i, sc: (i, j), L0)),
                       (up, _bs(up, (T, tc), lambda j, i, sc: (i, j + nf), L0)),
                       (up, _bs(up, (FFN_HALO, tc), lambda j, i, sc: (prev(i), j), L0)),
                       (up, _bs(up, (FFN_HALO, tc), lambda j, i, sc: (prev(i), j + nf), L0)),
                       (fw, _bs(fw, (FFN_HALO, tc), lambda j, i, sc: (0, j), L0)),
                       (fw, _bs(fw, (FFN_HALO, tc), lambda j, i, sc: (0, j + nf), L0)),
                       (fb3, _bs(fb3, (1, tc), lambda j, i, sc: (0, j), L0)),
                       (fb3, _bs(fb3, (1, tc), lambda j, i, sc: (0, j + nf), L0))],
                  outs=[(act_buf, _bs(act_buf, (T, tc), lambda j, i, sc: (i, j), L0))],
                  scratch=[pltpu.VMEM((T + FFN_HALO, tc), F32), pltpu.VMEM((T + FFN_HALO, tc), F32)])


def ffn_bwd(name, sc, up, dact, fw, fb3, dfw_v_buf, dfw_g_buf, dfb_v_buf, dfb_g_buf, Fh, W):
    S = up.shape[1]
    T, tc = _pick(S, 512, FFN_RB), _pick(Fh, 512)
    nf = Fh // tc
    nh = T // FFN_HALO
    nt = S // T
    off = FFN_HALO - (W - 1)
    TE = T + FFN_HALO

    def body(sc_ref, v_ref, g_ref, pv_ref, pg_ref, nv_ref, ng_ref, d_ref, dn_ref, wv_ref, wg_ref, bv_ref, bg_ref,
             ov_ref, og_ref, dwv_ref, dwg_ref, dbv_ref, dbg_ref, bufv, bufg, dv, dg):
        i = pl.program_id(1)
        for m_ref, p_ref, n_ref, buf in ((v_ref, pv_ref, nv_ref, bufv), (g_ref, pg_ref, ng_ref, bufg)):
            buf[0:FFN_HALO, :] = jnp.where(i > 0, p_ref[...], 0.0)
            buf[FFN_HALO:FFN_HALO + T, :] = m_ref[...]
            buf[FFN_HALO + T:FFN_HALO + TE, :] = n_ref[...]

        def d_up(r, n, d):
            val = _taps(bufv, wv_ref, bv_ref[...], r + off, n, W)
            gt = _taps(bufg, wg_ref, bg_ref[...], r + off, n, W)
            sg = _sigmoid(gt)
            dv[r:r + n, :] = d * gt * sg
            dg[r:r + n, :] = d * val * (sg * (1.0 + gt * (1.0 - sg)))

        for r in range(0, T, FFN_RB):
            d_up(r, FFN_RB, d_ref[r:r + FFN_RB, :])
        d_up(T, FFN_HALO, jnp.where(i < nt - 1, dn_ref[...], 0.0))

        @pl.when(i == 0)
        def _():
            for r in (dwv_ref, dwg_ref, dbv_ref, dbg_ref):
                r[...] = jnp.zeros_like(r)

        fold = lambda a: a[0:8, :] + a[8:16, :]
        for dsrc, w_ref, buf, o_ref, dw_ref, db_ref in ((dv, wv_ref, bufv, ov_ref, dwv_ref, dbv_ref),
                                                      (dg, wg_ref, bufg, og_ref, dwg_ref, dbg_ref)):
            acc_w = [jnp.zeros((8, tc), F32) for _ in range(W)]
            acc_b = jnp.zeros((8, tc), F32)
            for r in range(0, T, FFN_RB):
                o_ref[r:r + FFN_RB, :] = _taps(dsrc, w_ref, 0.0, r, FFN_RB, W, reverse=True).astype(BF16)
                dm = dsrc[r:r + FFN_RB, :]
                for k in range(W):
                    acc_w[k] = acc_w[k] + fold(dm * buf[r + off + k:r + off + k + FFN_RB, :])
                acc_b = acc_b + fold(dm)
            for k in range(W):
                dw_ref[k:k + 1, :] += jnp.sum(acc_w[k], axis=0, keepdims=True)
            db_ref[...] += jnp.sum(acc_b, axis=0, keepdims=True)

    prev = lambda i: jnp.maximum(i * nh - 1, 0)
    nxt = lambda i: jnp.minimum((i + 1) * nh, S // FFN_HALO - 1)
    o_sds = _sds((S, Fh), BF16)
    return _pcall(body, name=name, grid=(nf, nt), sc=sc,
                  ins=[(up, _bs(up, (T, tc), lambda j, i, sc: (i, j), L0)),
                       (up, _bs(up, (T, tc), lambda j, i, sc: (i, j + nf), L0)),
                       (up, _bs(up, (FFN_HALO, tc), lambda j, i, sc: (prev(i), j), L0)),
                       (up, _bs(up, (FFN_HALO, tc), lambda j, i, sc: (prev(i), j + nf), L0)),
                       (up, _bs(up, (FFN_HALO, tc), lambda j, i, sc: (nxt(i), j), L0)),
                       (up, _bs(up, (FFN_HALO, tc), lambda j, i, sc: (nxt(i), j + nf), L0)),
                       (dact, _bs(dact, (T, tc), lambda j, i, sc: (i, j))),
                       (dact, _bs(dact, (FFN_HALO, tc), lambda j, i, sc: (nxt(i), j))),
                       (fw, _bs(fw, (FFN_HALO, tc), lambda j, i, sc: (0, j), L0)),
                       (fw, _bs(fw, (FFN_HALO, tc), lambda j, i, sc: (0, j + nf), L0)),
                       (fb3, _bs(fb3, (1, tc), lambda j, i, sc: (0, j), L0)),
                       (fb3, _bs(fb3, (1, tc), lambda j, i, sc: (0, j + nf), L0))],
                  outs=[(o_sds, pl.BlockSpec((T, tc), lambda j, i, sc: (i, j))),
                        (o_sds, pl.BlockSpec((T, tc), lambda j, i, sc: (i, j))),
                        (dfw_v_buf, _bs(dfw_v_buf, (FFN_HALO, tc), lambda j, i, sc: (0, j), L0)),
                        (dfw_g_buf, _bs(dfw_g_buf, (FFN_HALO, tc), lambda j, i, sc: (0, j), L0)),
                        (dfb_v_buf, _bs(dfb_v_buf, (1, tc), lambda j, i, sc: (0, j), L0)),
                        (dfb_g_buf, _bs(dfb_g_buf, (1, tc), lambda j, i, sc: (0, j), L0))],
                  scratch=[pltpu.VMEM((T + 2 * FFN_HALO, tc), F32), pltpu.VMEM((T + 2 * FFN_HALO, tc), F32),
                           pltpu.VMEM((T + 2 * FFN_HALO, tc), F32), pltpu.VMEM((T + 2 * FFN_HALO, tc), F32)],
                  sem=("parallel", "arbitrary"))


def final_loss(name, sc, h, g2, target):
    S, D = h.shape
    tr = _pick(S, 256, 8)

    def body(sc_ref, h_ref, g_ref, t_ref, loss_ref, dx_ref, dxb_ref, dg_ref):
        i = pl.program_id(0)
        x = h_ref[...]
        r = lax.rsqrt(jnp.mean(x * x, axis=-1, keepdims=True) + EPS)
        xh = x * r
        err = xh * g_ref[...] - t_ref[...]
        part_loss = 0.5 * jnp.sum(jnp.mean(err * err, axis=-1, keepdims=True), axis=0, keepdims=True)
        dy = err * (1.0 / D)
        dxh = dy * g_ref[...]
        dx = r * (dxh - xh * jnp.mean(dxh * xh, axis=-1, keepdims=True))
        dx_ref[...] = dx
        dxb_ref[...] = dx.astype(BF16)
        part_g = jnp.sum(dy * xh, axis=0, keepdims=True)

        @pl.when(i == 0)
        def _():
            loss_ref[...] = jnp.zeros_like(loss_ref) + part_loss
            dg_ref[...] = part_g

        @pl.when(i > 0)
        def _():
            loss_ref[...] += part_loss
            dg_ref[...] += part_g

    row = lambda a: (a, pl.BlockSpec((tr, D), lambda i, sc: (i, 0)))
    return _pcall(body, name=name, grid=(S // tr,), sc=sc,
                  ins=[row(h), (g2, pl.BlockSpec((1, D), lambda i, sc: (0, 0))), row(target)],
                  outs=[(_sds((8, LANES), F32), pl.BlockSpec((8, LANES), lambda i, sc: (0, 0))),
                        row(_sds((S, D), F32)), row(_sds((S, D), BF16)),
                        (_sds((1, D), F32), pl.BlockSpec((1, D), lambda i, sc: (0, 0)))])


def _split3(a):
    hi = a.astype(BF16)
    r1 = a - hi.astype(F32)
    mid = r1.astype(BF16)
    lo = (r1 - mid.astype(F32)).astype(BF16)
    return hi, mid, lo


def _rel_onehot(qi, rows_are_keys):
    shape = (BANDP, REL_PAD) if rows_are_keys else (REL_PAD, BANDP)
    km = lax.broadcasted_iota(jnp.int32, shape, 0 if rows_are_keys else 1)
    idx = lax.broadcasted_iota(jnp.int32, shape, 1 if rows_are_keys else 0)
    rel = jnp.clip(LEFT_CHUNKS * CHUNK + qi - km, -MAX_REL, MAX_REL) + MAX_REL
    return jnp.where(rel == idx, 1.0, 0.0).astype(BF16)


def bias_expand(name, sc, rel3, H):
    def body(sc_ref, rel_ref, o_ref):
        qi = pl.program_id(0)
        oh = _rel_onehot(qi, False)
        acc = jnp.zeros((H, BANDP), F32)
        for part in _split3(rel_ref[...]):
            acc = acc + jnp.dot(part, oh, preferred_element_type=F32)
        o_ref[...] = acc

    return _pcall(body, name=name, grid=(CHUNK,), sc=sc,
                  ins=[(rel3, _bs(rel3, (H, REL_PAD), lambda q, sc: (0, 0), L0))],
                  outs=[(_sds((CHUNK, H, BANDP), F32), pl.BlockSpec((None, H, BANDP), lambda q, sc: (q, 0, 0)))])[0]


def bias_reduce(name, sc, dbias_q, drel_buf, H):
    def body(sc_ref, d_ref, o_ref):
        qi = pl.program_id(0)
        oh = _rel_onehot(qi, True)
        acc = jnp.zeros((H, REL_PAD), F32)
        for part in _split3(d_ref[...]):
            acc = acc + jnp.dot(part, oh, preferred_element_type=F32)

        @pl.when(qi == 0)
        def _():
            o_ref[...] = acc

        @pl.when(qi > 0)
        def _():
            o_ref[...] += acc

    return _pcall(body, name=name, grid=(CHUNK,), sc=sc,
                  ins=[(dbias_q, pl.BlockSpec((None, H, BANDP), lambda q, sc: (q, 0, 0)))],
                  outs=[(drel_buf, _bs(drel_buf, (H, REL_PAD), lambda q, sc: (0, 0), L0))])[0]


def _head_blocks(a2):
    lo = lax.broadcasted_iota(jnp.int32, a2.shape, 1) < ATTN_HEAD_DIM
    zero = jnp.zeros_like(a2)
    return jnp.concatenate([jnp.where(lo, a2, zero), jnp.where(lo, zero, a2)], axis=0)


def _head_diag(r):
    lo = lax.broadcasted_iota(jnp.int32, (CHUNK, LANES), 1) < ATTN_HEAD_DIM
    return jnp.where(lo, r[0:CHUNK], r[CHUNK:2 * CHUNK])


def _attn_probs_t(q2, k2, bias_t, c, first, scale):
    qbd = _head_blocks(q2)
    s = lax.dot_general(k2, qbd, (((1,), (1,)), ((), ())), preferred_element_type=F32) * scale + bias_t
    km = lax.broadcasted_iota(jnp.int32, s.shape, 0)
    valid = (km < BAND) & (jnp.logical_not(first) | (km + c * CHUNK >= QBLK))
    s = jnp.where(valid, s, NEG_INF)
    p = jnp.exp(s - jnp.max(s, axis=0, keepdims=True))
    return qbd, p / jnp.sum(p, axis=0, keepdims=True)


def bias_to_lanes(bias_q, npair):
    t = jnp.transpose(bias_q, (1, 2, 0)).reshape(npair, 2, BANDP, CHUNK)
    return jnp.transpose(t, (0, 2, 1, 3)).reshape(npair, BANDP, 2 * CHUNK)


def bias_from_lanes(bias_t, npair):
    t = jnp.transpose(bias_t.reshape(npair, BANDP, 2, CHUNK), (3, 0, 2, 1))
    return t.reshape(CHUNK, 2 * npair, BANDP)


def _fill_window(win, prev_ref, cur_ref, A):
    win[0:QBLK, :] = prev_ref[...].astype(BF16)
    win[QBLK:2 * QBLK, :] = cur_ref[...].astype(BF16)
    win[2 * QBLK:2 * QBLK + CHUNK, :] = jnp.zeros((CHUNK, A), BF16)


def attn_fwd(name, sc, proj, bias_buf, ao_buf, A, qcol):
    S = proj.shape[1]
    nb = S // QBLK
    H = A // ATTN_HEAD_DIM
    npair = A // LANES
    scale = ATTN_HEAD_DIM ** -0.5

    def body(sc_ref, q_ref, kp_ref, kc_ref, vp_ref, vc_ref, b_ref, o_ref, kw, vw):
        first = pl.program_id(0) == 0
        _fill_window(kw, kp_ref, kc_ref, A)
        _fill_window(vw, vp_ref, vc_ref, A)

        def chunk(c, carry):
            r0 = pl.multiple_of(c * CHUNK, CHUNK)
            for hp in range(npair):
                cols = slice(hp * LANES, (hp + 1) * LANES)
                q2 = q_ref[pl.ds(r0, CHUNK), cols].astype(BF16)
                k2 = kw[pl.ds(r0, BANDP), cols]
                v2 = vw[pl.ds(r0, BANDP), cols]
                _, p = _attn_probs_t(q2, k2, b_ref[hp], c, first, scale)
                o = lax.dot_general(p.astype(BF16), v2, (((0,), (0,)), ((), ())), preferred_element_type=F32)
                o_ref[pl.ds(r0, CHUNK), cols] = _head_diag(o).astype(BF16)
            return carry

        lax.fori_loop(0, LEFT_CHUNKS, chunk, 0, unroll=2)

    prevb = lambda i: jnp.maximum(i - 1, 0)
    blk = lambda rowf, col: (proj, _bs(proj, (QBLK, A), lambda i, sc: (rowf(i), col), L0))
    same = lambda i: i
    return _pcall(body, name=name, grid=(nb,), sc=sc,
                  ins=[blk(same, qcol), blk(prevb, qcol + 1), blk(same, qcol + 1), blk(prevb, qcol + 2), blk(same, qcol + 2),
                       (bias_buf, pl.BlockSpec((None, npair, BANDP, LANES), lambda i, sc: (sc[0], 0, 0, 0)))],
                  outs=[(ao_buf, _bs(ao_buf, (QBLK, A), lambda i, sc: (i, 0), L0))],
                  scratch=[pltpu.VMEM((2 * QBLK + CHUNK, A), BF16), pltpu.VMEM((2 * QBLK + CHUNK, A), BF16)])


def attn_bwd(name, sc, proj, bias_buf, dao, A, qcol):
    S = proj.shape[1]
    nb = S // QBLK
    H = A // ATTN_HEAD_DIM
    npair = A // LANES
    scale = ATTN_HEAD_DIM ** -0.5
    WIN = 2 * QBLK + CHUNK

    def body(sc_ref, q_ref, kp_ref, kc_ref, vp_ref, vc_ref, b_ref, do_ref, dq_ref, dk_ref, dv_ref, db_ref, kw, vw, dkw, dvw):
        i = pl.program_id(0)
        first = i == 0

        @pl.when(first)
        def _():
            dkw[0:QBLK, :] = jnp.zeros((QBLK, A), F32)
            dvw[0:QBLK, :] = jnp.zeros((QBLK, A), F32)
            db_ref[...] = jnp.zeros_like(db_ref)

        @pl.when(i > 0)
        def _():
            dkw[0:QBLK, :] = dkw[QBLK:2 * QBLK, :]
            dvw[0:QBLK, :] = dvw[QBLK:2 * QBLK, :]

        dkw[QBLK:WIN, :] = jnp.zeros((WIN - QBLK, A), F32)
        dvw[QBLK:WIN, :] = jnp.zeros((WIN - QBLK, A), F32)

        @pl.when(i < nb)
        def _():
            _fill_window(kw, kp_ref, kc_ref, A)
            _fill_window(vw, vp_ref, vc_ref, A)

            def chunk(c, carry):
                r0 = pl.multiple_of(c * CHUNK, CHUNK)
                for hp in range(npair):
                    cols = slice(hp * LANES, (hp + 1) * LANES)
                    q2 = q_ref[pl.ds(r0, CHUNK), cols].astype(BF16)
                    k2 = kw[pl.ds(r0, BANDP), cols]
                    v2 = vw[pl.ds(r0, BANDP), cols]
                    dobd = _head_blocks(do_ref[pl.ds(r0, CHUNK), cols])
                    qbd, p = _attn_probs_t(q2, k2, b_ref[hp], c, first, scale)
                    dp = lax.dot_general(v2, dobd, (((1,), (1,)), ((), ())), preferred_element_type=F32)
                    ds = p * (dp - jnp.sum(dp * p, axis=0, keepdims=True))
                    db_ref[hp] += ds
                    dsb = ds.astype(BF16)
                    dq = lax.dot_general(dsb, k2, (((0,), (0,)), ((), ())), preferred_element_type=F32) * scale
                    dq_ref[pl.ds(r0, CHUNK), cols] = _head_diag(dq).astype(BF16)
                    dkw[pl.ds(r0, BANDP), cols] += jnp.dot(dsb, qbd, preferred_element_type=F32) * scale
                    dvw[pl.ds(r0, BANDP), cols] += jnp.dot(p.astype(BF16), dobd, preferred_element_type=F32)
                return carry

            lax.fori_loop(0, LEFT_CHUNKS, chunk, 0, unroll=2)

        dk_ref[...] = dkw[0:QBLK, :].astype(BF16)
        dv_ref[...] = dvw[0:QBLK, :].astype(BF16)

    cur = lambda i: jnp.minimum(i, nb - 1)
    prevb = lambda i: jnp.maximum(jnp.minimum(i, nb - 1) - 1, 0)
    done = lambda i: jnp.maximum(i - 1, 0)
    blk = lambda rowf, col: (proj, _bs(proj, (QBLK, A), lambda i, sc: (rowf(i), col), L0))
    o_sds = _sds((S, A), BF16)
    return _pcall(body, name=name, grid=(nb + 1,), sc=sc,
                  ins=[blk(cur, qcol), blk(prevb, qcol + 1), blk(cur, qcol + 1), blk(prevb, qcol + 2), blk(cur, qcol + 2),
                       (bias_buf, pl.BlockSpec((None, npair, BANDP, LANES), lambda i, sc: (sc[0], 0, 0, 0))),
                       (dao, pl.BlockSpec((QBLK, A), lambda i, sc: (cur(i), 0)))],
                  outs=[(o_sds, pl.BlockSpec((QBLK, A), lambda i, sc: (cur(i), 0))),
                        (o_sds, pl.BlockSpec((QBLK, A), lambda i, sc: (done(i), 0))),
                        (o_sds, pl.BlockSpec((QBLK, A), lambda i, sc: (done(i), 0))),
                        (_sds((npair, BANDP, LANES), F32), pl.BlockSpec((npair, BANDP, LANES), lambda i, sc: (0, 0, 0)))],
                  scratch=[pltpu.VMEM((WIN, A), BF16), pltpu.VMEM((WIN, A), BF16),
                           pltpu.VMEM((WIN, A), F32), pltpu.VMEM((WIN, A), F32)])


def mem_fwd(name, sc, proj, kv_buf, mo_buf, Dm, qcol):
    S = proj.shape[1]
    NM = kv_buf.shape[1]
    tr = _pick(S, 512, 8)
    hd = Dm // N_MEM_HEADS
    scale = hd ** -0.5

    def body(sc_ref, q_ref, kv_ref, o_ref):
        for h in range(N_MEM_HEADS):
            cols = slice(h * hd, (h + 1) * hd)
            q = q_ref[:, cols].astype(BF16)
            k = kv_ref[:, cols].astype(BF16)
            v = kv_ref[:, Dm + h * hd:Dm + (h + 1) * hd].astype(BF16)
            s = lax.dot_general(q, k, (((1,), (1,)), ((), ())), preferred_element_type=F32) * scale
            p = jnp.exp(s - jnp.max(s, axis=-1, keepdims=True))
            p = p / jnp.sum(p, axis=-1, keepdims=True)
            o_ref[:, cols] = jnp.dot(p.astype(BF16), v, preferred_element_type=F32).astype(BF16)

    return _pcall(body, name=name, grid=(S // tr,), sc=sc,
                  ins=[(proj, _bs(proj, (tr, Dm), lambda i, sc: (i, qcol), L0)),
                       (kv_buf, _bs(kv_buf, (NM, 2 * Dm), lambda i, sc: (0, 0), L0))],
                  outs=[(mo_buf, _bs(mo_buf, (tr, Dm), lambda i, sc: (i, 0), L0))])


def mem_bwd(name, sc, proj, kv_buf, dmo, Dm, qcol):
    S = proj.shape[1]
    NM = kv_buf.shape[1]
    tr = _pick(S, 512, 8)
    hd = Dm // N_MEM_HEADS
    scale = hd ** -0.5

    def body(sc_ref, q_ref, kv_ref, do_ref, dq_ref, dkv_ref):
        i = pl.program_id(0)

        @pl.when(i == 0)
        def _():
            dkv_ref[...] = jnp.zeros_like(dkv_ref)

        for h in range(N_MEM_HEADS):
            cols = slice(h * hd, (h + 1) * hd)
            vcols = slice(Dm + h * hd, Dm + (h + 1) * hd)
            q = q_ref[:, cols].astype(BF16)
            k = kv_ref[:, cols].astype(BF16)
            v = kv_ref[:, vcols].astype(BF16)
            do = do_ref[:, cols]
            s = lax.dot_general(q, k, (((1,), (1,)), ((), ())), preferred_element_type=F32) * scale
            p = jnp.exp(s - jnp.max(s, axis=-1, keepdims=True))
            p = p / jnp.sum(p, axis=-1, keepdims=True)
            dp = lax.dot_general(do, v, (((1,), (1,)), ((), ())), preferred_element_type=F32)
            ds = p * (dp - jnp.sum(dp * p, axis=-1, keepdims=True))
            dsb = ds.astype(BF16)
            dq_ref[:, cols] = (jnp.dot(dsb, k, preferred_element_type=F32) * scale).astype(BF16)
            dkv_ref[:, cols] += lax.dot_general(dsb, q, (((0,), (0,)), ((), ())), preferred_element_type=F32) * scale
            dkv_ref[:, vcols] += lax.dot_general(p.astype(BF16), do, (((0,), (0,)), ((), ())), preferred_element_type=F32)

    return _pcall(body, name=name, grid=(S // tr,), sc=sc,
                  ins=[(proj, _bs(proj, (tr, Dm), lambda i, sc: (i, qcol), L0)),
                       (kv_buf, _bs(kv_buf, (NM, 2 * Dm), lambda i, sc: (0, 0), L0)),
                       (dmo, pl.BlockSpec((tr, Dm), lambda i, sc: (i, 0)))],
                  outs=[(_sds((S, Dm), BF16), pl.BlockSpec((tr, Dm), lambda i, sc: (i, 0))),
                        (_sds((NM, 2 * Dm), F32), pl.BlockSpec((NM, 2 * Dm), lambda i, sc: (0, 0)))])


def _rows2d(a):
    return a.reshape(-1, a.shape[-1])


def _ew(name, fn, ins, out_dtypes):
    R, C = ins[0].shape
    tc = _pick(C, 2048)
    tr = _pick(R, max(8, (1 << 19) // tc), 8)
    n_in = len(ins)

    def body(sc_ref, *refs):
        outs = fn(*[r[...] for r in refs[:n_in]])
        for o_ref, o in zip(refs[n_in:], outs):
            o_ref[...] = o.astype(o_ref.dtype)

    plain = pl.BlockSpec((tr, tc), lambda i, j, sc: (i, j))
    return _pcall(body, name=name, grid=(R // tr, C // tc), sc=jnp.zeros((1,), jnp.int32), ins=[(a, plain) for a in ins],
                  outs=[(_sds((R, C), dt), plain) for dt in out_dtypes], sem=("parallel", "parallel"))


def _ew3(name, fn, sc, dims, ins, outs):
    G, R, C = dims
    tc = _pick(C, 2048)
    tr = _pick(R, max(16, (1 << 19) // tc), 16)
    n_in = len(ins)

    def body(sc_ref, *refs):
        res = fn(*[r[...] for r in refs[:n_in]])
        for o_ref, o in zip(refs[n_in:], res):
            o_ref[...] = o.astype(o_ref.dtype)

    def spec(arr, index):
        nlead = len(arr.shape) - 2

        def imap(g, i, j, s):
            lead, ro, co = index(g, s)
            return (*lead, i + ro // tr, j + co // tc)

        return pl.BlockSpec((None,) * nlead + (tr, tc), imap)

    return _pcall(body, name=name, grid=(G, R // tr, C // tc), sc=sc,
                  ins=[(a, spec(a, ix)) for a, ix in ins], outs=[(o, spec(o, ix)) for o, ix in outs],
                  sem=("parallel", "parallel", "parallel"))


def _at(lead_fn=None, ax=None, size=0):
    def index(g, s):
        lead = (g,) if lead_fn is None else lead_fn(g, s)
        off = s[0] * size
        return lead, (off if ax == 1 else 0), (off if ax == 2 else 0)
    return index


def _adamw_math(w, g, m, v):
    m = ADAM_B1 * m + (1.0 - ADAM_B1) * g
    v = ADAM_B2 * v + (1.0 - ADAM_B2) * (g * g)
    m_hat = m / (1.0 - ADAM_B1 ** ADAM_STEP)
    v_hat = v / (1.0 - ADAM_B2 ** ADAM_STEP)
    delta = -ADAM_LR * (m_hat / (jnp.sqrt(v_hat) + ADAM_EPS) + ADAM_WD * w)
    return delta, m, v


def adamw(name, w, g, m, v):
    shp = w.shape
    d, nm, nv = _ew(name, _adamw_math, [_rows2d(w), _rows2d(g), _rows2d(m), _rows2d(v)], (F32, F32, F32))
    return d.reshape(shp), nm.reshape(shp), nv.reshape(shp)


def _place():
    x, y, c = lax.axis_index("x"), lax.axis_index("y"), lax.axis_index("c")
    chips = [(1 - x, y), (x, 1 - y), (1 - x, 1 - y)]
    return x, y, c, chips


def _sub(ref, ax=None, s=None, n=None, half=None, lh=None):
    lay = slice(None) if half is None else pl.ds(half * lh, lh)
    if ax is None:
        return ref.at[lay]
    cut = pl.ds(pl.multiple_of(s * n, n), n)
    return ref.at[lay, cut, :] if ax == 1 else ref.at[lay, :, cut]


HBM_SPEC = pl.BlockSpec(memory_space=pl.ANY)


def _slab(ref, l, ax, s, half):
    K, N = ref.shape[1], ref.shape[2]
    if ax == 1:
        kh = K // 8
        return ref.at[l, pl.ds(pl.multiple_of((2 * s + half) * kh, kh), kh), :]
    kh, ns = K // 2, N // 4
    return ref.at[l, pl.ds(pl.multiple_of(half * kh, kh), kh), pl.ds(pl.multiple_of(s * ns, ns), ns)]


def _copy_to(view, sems_s, sems_r, k, j, to):
    return pltpu.make_async_remote_copy(src_ref=view, dst_ref=view, send_sem=sems_s.at[k, j], recv_sem=sems_r.at[k, j],
                                        device_id=to, device_id_type=MESH)


def _gather_start(n, view, sems):
    x, y, c, chips = _place()
    for k in range(n):
        for j, (px, py) in enumerate(chips):
            _copy_to(view(k, 2 * x + y, c), sems[0], sems[1], k, j, (px, py, c)).start()


def _gather_finish(n, view, sems):
    x, y, c, chips = _place()
    sibling = (x, y, 1 - c)
    for k in range(n):
        for j, (px, py) in enumerate(chips):
            theirs = view(k, 2 * px + py, c)
            _copy_to(theirs, sems[0], sems[1], k, j, (px, py, c)).wait_recv()
            _copy_to(theirs, sems[2], sems[3], k, j, sibling).start()
    for k in range(n):
        for j, (px, py) in enumerate(chips):
            _copy_to(view(k, 2 * px + py, 1 - c), sems[2], sems[3], k, j, sibling).wait_recv()
    for k in range(n):
        for j, (px, py) in enumerate(chips):
            _copy_to(view(k, 2 * x + y, c), sems[0], sems[1], k, j, (px, py, c)).wait_send()
            _copy_to(view(k, 2 * px + py, c), sems[2], sems[3], k, j, sibling).wait_send()


def _gather_layer_start(refs, axes, l, sems):
    _gather_start(len(refs), lambda k, s, half: _slab(refs[k], l, axes[k], s, half), sems)


def _gather_layer_finish(refs, axes, l, sems):
    _gather_finish(len(refs), lambda k, s, half: _slab(refs[k], l, axes[k], s, half), sems)


def gather_weights(bufs, axes, first_layer_only):
    n = len(bufs)
    lh = bufs[0].shape[0] // 2

    def body(*refs):
        dst = refs[n:2 * n]

        def view(k, s, half):
            if first_layer_only[k]:
                return _slab(dst[k], 0, axes[k], s, half)
            return _sub(dst[k], axes[k], s, dst[k].shape[axes[k]] // 4, half, lh)

        _gather_start(n, view, refs[2 * n:])
        _gather_finish(n, view, refs[2 * n:])

    return pl.pallas_call(
        body, name="gather_weights", out_shape=[_sds(b.shape, b.dtype) for b in bufs],
        in_specs=[HBM_SPEC] * n, out_specs=[HBM_SPEC] * n, input_output_aliases={k: k for k in range(n)},
        scratch_shapes=[pltpu.SemaphoreType.DMA((n, 3)) for _ in range(4)],
    )(*bufs)


def swap_halves(name, grads):
    n = len(grads)
    lh = grads[0].shape[0] // 2

    def body(*refs):
        src, dst = refs[:n], refs[n:2 * n]
        send, recv = refs[2 * n:]
        x, y, c, _ = _place()
        cps = []
        for k in range(n):
            cp = pltpu.make_async_remote_copy(
                src_ref=_sub(src[k], half=1 - c, lh=lh), dst_ref=dst[k], send_sem=send.at[k], recv_sem=recv.at[k],
                device_id=(x, y, 1 - c), device_id_type=MESH)
            cp.start()
            cps.append(cp)
        for cp in cps:
            cp.wait()

    return pl.pallas_call(
        body, name=name, out_shape=[_sds((lh,) + g.shape[1:], g.dtype) for g in grads],
        in_specs=[HBM_SPEC] * n, out_specs=[HBM_SPEC] * n,
        scratch_shapes=[pltpu.SemaphoreType.DMA((n,)), pltpu.SemaphoreType.DMA((n,))],
    )(*grads)


def exchange_chip_sums(name, sums_bf16, axes):
    n = len(sums_bf16)

    def body(*refs):
        sb, got = refs[:n], refs[n:2 * n]
        send, recv = refs[2 * n:]
        x, y, c, chips = _place()
        cps = []
        for k in range(n):
            ax = axes[k]
            ns = got[k].shape[1 + ax]
            for j, (px, py) in enumerate(chips):
                cp = pltpu.make_async_remote_copy(
                    src_ref=_sub(sb[k], ax, 2 * px + py, ns), dst_ref=got[k].at[j],
                    send_sem=send.at[k, j], recv_sem=recv.at[k, j], device_id=(px, py, c), device_id_type=MESH)
                cp.start()
                cps.append(cp)
        for cp in cps:
            cp.wait()

    out_shape = []
    for a, ax in zip(sums_bf16, axes):
        shp = list(a.shape)
        shp[ax] //= 4
        out_shape.append(_sds([3] + shp, BF16))
    return pl.pallas_call(
        body, name=name, out_shape=out_shape, in_specs=[HBM_SPEC] * n, out_specs=[HBM_SPEC] * n,
        scratch_shapes=[pltpu.SemaphoreType.DMA((n, 3)), pltpu.SemaphoreType.DMA((n, 3))],
    )(*sums_bf16)


def join_halves(name, bufs):
    n = len(bufs)
    lh = bufs[0].shape[0] // 2

    def body(*refs):
        dst = refs[n:2 * n]
        send, recv = refs[2 * n:]
        x, y, c, _ = _place()
        cps = []
        for k in range(n):
            mine = _sub(dst[k], half=c, lh=lh)
            cp = pltpu.make_async_remote_copy(
                src_ref=mine, dst_ref=mine, send_sem=send.at[k], recv_sem=recv.at[k],
                device_id=(x, y, 1 - c), device_id_type=MESH)
            cp.start()
            cps.append(cp)
        for k, cp in enumerate(cps):
            cp.wait_send()
            theirs = _sub(dst[k], half=1 - c, lh=lh)
            pltpu.make_async_remote_copy(
                src_ref=theirs, dst_ref=theirs, send_sem=send.at[k], recv_sem=recv.at[k],
                device_id=(x, y, 1 - c), device_id_type=MESH).wait_recv()

    return pl.pallas_call(
        body, name=name, out_shape=[_sds(b.shape, b.dtype) for b in bufs],
        in_specs=[HBM_SPEC] * n, out_specs=[HBM_SPEC] * n, input_output_aliases={k: k for k in range(n)},
        scratch_shapes=[pltpu.SemaphoreType.DMA((n,)), pltpu.SemaphoreType.DMA((n,))],
    )(*bufs)


def allreduce_small(name, packed):
    R = packed.shape[0]

    def body(x_ref, o_ref, buf, send, recv, loc):
        x, y, c, _ = _place()
        me = 4 * x + 2 * y + c
        cps = [pltpu.make_async_copy(x_ref, buf.at[me], loc)]
        cps[0].start()
        for d in range(1, 8):
            px, py, pc = x ^ (d >> 2), y ^ ((d >> 1) & 1), c ^ (d & 1)
            cp = pltpu.make_async_remote_copy(
                src_ref=x_ref, dst_ref=buf.at[me], send_sem=send.at[d - 1], recv_sem=recv.at[d - 1],
                device_id=(px, py, pc), device_id_type=MESH)
            cp.start()
            cps.append(cp)
        for cp in cps:
            cp.wait()
        acc = buf[0]
        for s in range(1, 8):
            acc = acc + buf[s]
        o_ref[...] = acc

    return pl.pallas_call(
        body, name=name, out_shape=_sds((R, LANES), F32),
        in_specs=[pl.BlockSpec(memory_space=pltpu.VMEM)], out_specs=pl.BlockSpec(memory_space=pltpu.VMEM),
        scratch_shapes=[pltpu.VMEM((8, R, LANES), F32), pltpu.SemaphoreType.DMA((7,)), pltpu.SemaphoreType.DMA((7,)),
                        pltpu.SemaphoreType.DMA],
        compiler_params=pltpu.CompilerParams(vmem_limit_bytes=VMEM_LIMIT),
    )(packed)


def _pack(arrs):
    flat = jnp.concatenate([a.reshape(-1) for a in arrs])
    pad = (-flat.shape[0]) % (8 * LANES)
    return jnp.pad(flat, (0, pad)).reshape(-1, LANES)


def _unpack(packed, like):
    flat = packed.reshape(-1)
    out, pos = [], 0
    for a in like:
        out.append(flat[pos:pos + a.size].reshape(a.shape))
        pos += a.size
    return out


def kernel(x, mem, mix_norm_g, mem_norm_g, w_in, gate_b, conv_w, conv_b, conv_ln_g, conv_ln_b, w_conv_out, rel_bias, w_attn_out, w_mem_kv, w_mem_out, w_o, ffn_norm_g, w_up, ffn_conv_w, ffn_conv_b, w_down, final_norm_g, loss_target, m_mix_norm_g, m_mem_norm_g, m_w_in, m_gate_b, m_conv_w, m_conv_b, m_conv_ln_g, m_conv_ln_b, m_w_conv_out, m_rel_bias, m_w_attn_out, m_w_mem_kv, m_w_mem_out, m_w_o, m_ffn_norm_g, m_w_up, m_ffn_conv_w, m_ffn_conv_b, m_w_down, m_final_norm_g, v_mix_norm_g, v_mem_norm_g, v_w_in, v_gate_b, v_conv_w, v_conv_b, v_conv_ln_g, v_conv_ln_b, v_w_conv_out, v_rel_bias, v_w_attn_out, v_w_mem_kv, v_w_mem_out, v_w_o, v_ffn_norm_g, v_w_up, v_ffn_conv_w, v_ffn_conv_b, v_w_down, v_final_norm_g):
    S, D = x.shape[1], x.shape[2]
    NM = mem.shape[1]
    L = w_in.shape[0]
    C = conv_b.shape[1]
    A = w_attn_out.shape[1]
    Dm = w_mem_out.shape[1]
    Fh = w_down.shape[1] * 4
    D_IN = w_in.shape[2] * 4
    CW = conv_w.shape[1]
    FW = ffn_conv_w.shape[1]
    H = A // ATTN_HEAD_DIM
    assert 2 * C == 2 * A == 2 * Dm == D and D_IN == 6 * D and S % QBLK == 0 and L % 2 == 0
    xi, yi, ci = lax.axis_index("x"), lax.axis_index("y"), lax.axis_index("c")
    chip = 2 * xi + yi
    place_sc = jnp.stack([chip, ci]).astype(jnp.int32)
    zero_sc = jnp.zeros((1,), jnp.int32)
    lh = L // 2

    big = [w_in, w_conv_out, w_attn_out, w_mem_kv, w_mem_out, w_o, w_up, w_down]
    big_m = [m_w_in, m_w_conv_out, m_w_attn_out, m_w_mem_kv, m_w_mem_out, m_w_o, m_w_up, m_w_down]
    big_v = [v_w_in, v_w_conv_out, v_w_attn_out, v_w_mem_kv, v_w_mem_out, v_w_o, v_w_up, v_w_down]
    big_ax = [2, 2, 2, 1, 2, 1, 2, 1]
    placed = []
    for k, (w, ax, dt) in enumerate(zip(big + [conv_w, ffn_conv_w], big_ax + [2, 2], [BF16] * len(big) + [F32, F32])):
        shp = list(w.shape)
        shp[ax] *= 4
        placed.append(_ew3(f"place_w{k}", lambda a: (a,), place_sc, w.shape, [(w, _at())],
                           [(_sds(shp, dt), _at(ax=ax, size=w.shape[ax]))])[0])
    gathered = gather_weights(placed, big_ax + [2, 2], [True] * len(big) + [False, False])
    wnames = ["in", "co", "ao", "kv", "mo", "o", "up", "dn"]
    Wd = dict(zip(wnames, gathered[:len(big)]))
    w_axis = dict(zip(wnames, big_ax))
    conv_w_f, ffn_conv_w_f = gathered[len(big):]
    hosted_by_in = ["up", "dn", "o"]
    hosted_by_up = ["in", "co", "ao", "kv", "mo"]

    r3 = lambda a: a.reshape(L, 1, a.shape[-1])
    mix_g3, mem_g3, ffn_g3, gb3 = r3(mix_norm_g), r3(mem_norm_g), r3(ffn_norm_g), r3(gate_b)
    cb3, lg3, lb3, fb3 = r3(conv_b), r3(conv_ln_g), r3(conv_ln_b), r3(ffn_conv_b)
    cw_p = jnp.pad(conv_w_f, ((0, 0), (0, CONV_HALO - CW), (0, 0)))
    fw_p = jnp.pad(ffn_conv_w_f, ((0, 0), (0, FFN_HALO - FW), (0, 0)))
    rel_p = jnp.pad(rel_bias, ((0, 0), (0, 0), (0, REL_PAD - rel_bias.shape[2])))

    x2, mem2, tgt2 = x[0], mem[0], loss_target[0]
    empty = lambda shape, dt: lax.empty(shape, dt)

    def layer_scalars(l):
        i32 = lambda *v: jnp.stack(v).astype(jnp.int32)
        return i32(l, 2 * l, 2 * l + 1), i32(l, 2 * l), i32(l, 2 * l + 1), i32(l, l)

    saved = dict(
        XN=empty((2 * L, S, D), BF16), HS=empty((2 * L, S, D), F32), PROJ=empty((L, S, D_IN), F32),
        HC=empty((L, S, C), F32), CACT=empty((L, S, C), BF16), AO=empty((L, S, A), BF16), MO=empty((L, S, Dm), BF16),
        Y=empty((L, 3, S, D), F32), MERGED=empty((L, S, D), BF16), UP=empty((L, S, 2 * Fh), F32),
        ACT=empty((L, S, Fh), BF16), MN=empty((L, NM, D), BF16), MEMS=empty((L, NM, D), F32),
        KV=empty((L, NM, 2 * Dm), F32), BIAS=empty((L, A // LANES, BANDP, LANES), F32))

    def fwd_layer(l, carry):
        h, sv, Wd = carry
        sv, Wd = dict(sv), dict(Wd)
        W_in, W_co, W_ao, W_kv, W_mo, W_o, W_up, W_dn = [Wd[nm] for nm in wnames]
        sc, sc_a, sc_f, sc_m = layer_scalars(l)
        sv["XN"], sv["HS"] = rms_fwd("rms_mix", sc_a, h, mix_g3, sv["XN"], sv["HS"])
        sv["PROJ"], *got = mm_nn("mm_in", sc_a, sv["XN"], L1, W_in, S, D, D_IN, sv["PROJ"], L0,
                                 gather=([Wd[nm] for nm in hosted_by_in], [w_axis[nm] for nm in hosted_by_in]))
        Wd.update(zip(hosted_by_in, got))
        W_o, W_dn = Wd["o"], Wd["dn"]
        sv["HC"], sv["CACT"] = conv_fwd("conv_fwd", sc, sv["PROJ"], cw_p, cb3, lg3, lb3, sv["HC"], sv["CACT"], C, CW)
        bias_q = bias_expand("bias_expand", sc, rel_p, H)
        sv["BIAS"] = lax.dynamic_update_slice(sv["BIAS"], bias_to_lanes(bias_q, A // LANES)[None], (l, 0, 0, 0))
        sv["AO"], = attn_fwd("attn_fwd", sc, sv["PROJ"], sv["BIAS"], sv["AO"], A, 2)
        sv["MN"], sv["MEMS"] = rms_fwd("rms_mem", sc_m, mem2, mem_g3, sv["MN"], sv["MEMS"])
        sv["KV"] = mm_nn("mm_kv", sc, sv["MN"], L0, W_kv, NM, D, 2 * Dm, sv["KV"], L0)
        sv["MO"], = mem_fwd("mem_fwd", sc, sv["PROJ"], sv["KV"], sv["MO"], Dm, 5)
        for b, (src, w, nm) in enumerate(((sv["CACT"], W_co, "mm_co"), (sv["AO"], W_ao, "mm_ao"), (sv["MO"], W_mo, "mm_mo"))):
            sv["Y"] = mm_nn(nm, sc, src, L0, w, S, C, D, sv["Y"], lambda s, b=b: (s[0], b))
        sv["MERGED"], = gate_fwd("gate_fwd", sc, sv["PROJ"], gb3, sv["Y"], sv["MERGED"], D)
        h2 = mm_nn("mm_o", sc, sv["MERGED"], L0, W_o, S, D, D, _sds((S, D), F32), add=h)
        sv["XN"], sv["HS"] = rms_fwd("rms_ffn", sc_f, h2, ffn_g3, sv["XN"], sv["HS"])
        sv["UP"], *got = mm_nn("mm_up", sc_f, sv["XN"], L1, Wd["up"], S, D, 2 * Fh, sv["UP"], L0,
                               gather=([Wd[nm] for nm in hosted_by_up], [w_axis[nm] for nm in hosted_by_up]))
        Wd.update(zip(hosted_by_up, got))
        sv["ACT"], = ffn_fwd("ffn_fwd", sc, sv["UP"], fw_p, fb3, sv["ACT"], Fh, FW)
        h3 = mm_nn("mm_down", sc, sv["ACT"], L0, W_dn, S, Fh, D, _sds((S, D), F32), add=h2)
        return h3, sv, Wd

    h_last, saved, Wd = lax.fori_loop(0, L, fwd_layer, (x2, saved, Wd))
    W_in, W_co, W_ao, W_kv, W_mo, W_o, W_up, W_dn = [Wd[nm] for nm in wnames]
    loss_t, dh, dhb, d_final_g = final_loss("final_loss", zero_sc, h_last, final_norm_g.reshape(1, D), tgt2)
    loss = lax.psum(loss_t[0, 0], ("x", "y", "c"))

    zeros = lambda shape: jnp.zeros(shape, F32)
    grads = dict(
        w_in=empty((L, D, D_IN), F32), w_conv_out=empty((L, C, D), F32), w_attn_out=empty((L, A, D), F32),
        w_mem_kv=empty((L, D, 2 * Dm), F32), w_mem_out=empty((L, Dm, D), F32), w_o=empty((L, D, D), F32),
        w_up=empty((L, D, 2 * Fh), F32), w_down=empty((L, Fh, D), F32),
        mix_g=zeros((L, 1, D)), mem_g=zeros((L, 1, D)), ffn_g=zeros((L, 1, D)), gate_b=zeros((L, 1, 3 * D)),
        conv_w=zeros((L, CONV_HALO, C)), conv_b=zeros((L, 1, C)), ln_g=zeros((L, 1, C)), ln_b=zeros((L, 1, C)),
        rel=zeros((L, H, REL_PAD)), fw_v=zeros((L, FFN_HALO, Fh)), fw_g=zeros((L, FFN_HALO, Fh)),
        fb_v=zeros((L, 1, Fh)), fb_g=zeros((L, 1, Fh)))
    zero_mem = jnp.zeros((NM, D), F32)

    def bwd_layer(it, carry):
        dh, dhb, g = carry
        g = dict(g)
        l = L - 1 - it
        sc, sc_a, sc_f, sc_m = layer_scalars(l)
        sv = saved
        dact = mm_nt("mm_down_dx", sc, [(dhb, None)], W_dn, S, Fh, _sds((S, Fh), F32))
        g["w_down"] = mm_tn("mm_down_dw", sc, sv["ACT"], L0, dhb, None, S, Fh, D, g["w_down"], L0)
        dupv, dupg, g["fw_v"], g["fw_g"], g["fb_v"], g["fb_g"] = ffn_bwd(
            "ffn_bwd", sc, sv["UP"], dact, fw_p, fb3, g["fw_v"], g["fw_g"], g["fb_v"], g["fb_g"], Fh, FW)
        dhn = mm_nt("mm_up_dx", sc, [(dupv, None), (dupg, None)], W_up, S, D, _sds((S, D), F32), tk=2816)
        g["w_up"] = mm_tn("mm_up_dw_v", sc_f, sv["XN"], L1, dupv, None, S, D, Fh, g["w_up"], L0)
        g["w_up"] = mm_tn("mm_up_dw_g", sc_f, sv["XN"], L1, dupg, None, S, D, Fh, g["w_up"], L0, out_joff=Fh)
        dh2, dh2b, g["ffn_g"] = rms_bwd("rms_ffn_bwd", sc_f, sv["HS"], ffn_g3, dhn, dh, g["ffn_g"])
        dmerged = mm_nt("mm_o_dx", sc, [(dh2b, None)], W_o, S, D, _sds((S, D), F32))
        g["w_o"] = mm_tn("mm_o_dw", sc, sv["MERGED"], L0, dh2b, None, S, D, D, g["w_o"], L0)
        dy, dgates, g["gate_b"] = gate_bwd("gate_bwd", sc, sv["PROJ"], gb3, sv["Y"], dmerged, g["gate_b"], D)
        dcact = mm_nt("mm_co_dx", sc, [(dy, lambda s: (0,))], W_co, S, C, _sds((S, C), F32))
        dao = mm_nt("mm_ao_dx", sc, [(dy, lambda s: (1,))], W_ao, S, A, _sds((S, A), BF16), out_dtype=BF16)
        dmo = mm_nt("mm_mo_dx", sc, [(dy, lambda s: (2,))], W_mo, S, Dm, _sds((S, Dm), BF16), out_dtype=BF16)
        g["w_conv_out"] = mm_tn("mm_co_dw", sc, sv["CACT"], L0, dy, lambda s: (0,), S, C, D, g["w_conv_out"], L0)
        g["w_attn_out"] = mm_tn("mm_ao_dw", sc, sv["AO"], L0, dy, lambda s: (1,), S, A, D, g["w_attn_out"], L0)
        g["w_mem_out"] = mm_tn("mm_mo_dw", sc, sv["MO"], L0, dy, lambda s: (2,), S, Dm, D, g["w_mem_out"], L0)
        dhc, g["ln_g"], g["ln_b"], g["conv_b"] = conv_bwd1(
            "conv_bwd1", sc, sv["HC"], dcact, lg3, lb3, g["ln_g"], g["ln_b"], g["conv_b"])
        dconv, g["conv_w"] = conv_bwd2("conv_bwd2", sc, sv["PROJ"], dhc, cw_p, g["conv_w"], C, CW)
        dq, dk, dv, dbias = attn_bwd("attn_bwd", sc, sv["PROJ"], sv["BIAS"], dao, A, 2)
        g["rel"] = bias_reduce("bias_reduce", sc, bias_from_lanes(dbias, A // LANES), g["rel"], H)
        dqm, dkv = mem_bwd("mem_bwd", sc, sv["PROJ"], sv["KV"], dmo, Dm, 5)
        g["w_mem_kv"] = mm_tn("mm_kv_dw", sc, sv["MN"], L0, dkv, None, NM, D, 2 * Dm, g["w_mem_kv"], L0)
        dmn = mm_nt("mm_kv_dx", sc, [(dkv, None)], W_kv, NM, D, _sds((NM, D), F32))
        _, _, g["mem_g"] = rms_bwd("rms_mem_bwd", sc_m, sv["MEMS"], mem_g3, dmn, zero_mem, g["mem_g"])
        pieces = [(dconv, None), (dq, None), (dk, None), (dv, None), (dqm, None), (dgates, None)]
        dxn = mm_nt("mm_in_dx", sc, pieces, W_in, S, D, _sds((S, D), F32), tk=1024)
        off = 0
        for nm, (p, _) in zip(("c", "q", "k", "v", "m", "g"), pieces):
            g["w_in"] = mm_tn("mm_in_dw_" + nm, sc_a, sv["XN"], L1, p, None, S, D, p.shape[1], g["w_in"], L0, out_joff=off)
            off += p.shape[1]
        dh0, dh0b, g["mix_g"] = rms_bwd("rms_mix_bwd", sc_a, sv["HS"], mix_g3, dxn, dh2, g["mix_g"])
        return dh0, dh0b, g

    grad_x2, _, grads = lax.fori_loop(0, L, bwd_layer, (dh, dhb, grads))

    names = ["w_in", "w_conv_out", "w_attn_out", "w_mem_kv", "w_mem_out", "w_o", "w_up", "w_down"]
    gl = [grads[nm] for nm in names]
    from_sib = swap_halves("swap_halves", gl)
    my_layer = lambda g_, s: (s[1] * lh + g_,)
    sums_f, sums_b = [], []
    for k, (g_, r_) in enumerate(zip(gl, from_sib)):
        s_f, s_b = _ew3(f"sum_sib{k}", lambda a, b: (a + b, a + b), place_sc, r_.shape,
                        [(g_, _at(my_layer)), (r_, _at())], [(_sds(r_.shape, F32), _at()), (_sds(r_.shape, BF16), _at())])
        sums_f.append(s_f)
        sums_b.append(s_b)
    got = exchange_chip_sums("exchange_chip_sums", sums_b, big_ax)
    halves = []
    for k, (s_f, r_, w, ax) in enumerate(zip(sums_f, got, big, big_ax)):
        fin, = _ew3(f"sum_chips{k}", lambda a, b0, b1, b2: (((a + b0.astype(F32)) + b1.astype(F32)) + b2.astype(F32),),
                    place_sc, (lh,) + w.shape[1:],
                    [(s_f, _at(ax=ax, size=w.shape[ax]))] + [(r_, _at(lambda g_, s, j=j: (j, g_))) for j in range(3)],
                    [(_sds(w.shape, F32), _at(my_layer))])
        halves.append(fin)
    big_g = join_halves("join_halves", halves)
    big_d, big_nm, big_nv = [], [], []
    for k in range(len(big)):
        d_, m_, v_ = adamw(f"adamw_big{k}", big[k], big_g[k], big_m[k], big_v[k])
        big_d.append(d_), big_nm.append(m_), big_nv.append(v_)

    g_small_full = [
        grads["mix_g"].reshape(L, D), grads["mem_g"].reshape(L, D), grads["gate_b"].reshape(L, 3 * D),
        grads["conv_w"][:, :CW, :], grads["conv_b"].reshape(L, C), grads["ln_g"].reshape(L, C), grads["ln_b"].reshape(L, C),
        grads["rel"][:, :, :rel_bias.shape[2]], grads["ffn_g"].reshape(L, D),
        jnp.concatenate([grads["fw_v"][:, :FW, :], grads["fw_g"][:, :FW, :]], axis=-1),
        jnp.concatenate([grads["fb_v"], grads["fb_g"]], axis=-1).reshape(L, 2 * Fh), d_final_g.reshape(D)]
    summed = _unpack(allreduce_small("allreduce_small", _pack(g_small_full)), g_small_full)
    cws, fws = conv_w.shape[2], ffn_conv_w.shape[2]
    summed[3] = lax.dynamic_slice_in_dim(summed[3], chip * cws, cws, axis=2)
    summed[9] = lax.dynamic_slice_in_dim(summed[9], chip * fws, fws, axis=2)
    small_w = [mix_norm_g, mem_norm_g, gate_b, conv_w, conv_b, conv_ln_g, conv_ln_b, rel_bias, ffn_norm_g, ffn_conv_w, ffn_conv_b, final_norm_g]
    small_m = [m_mix_norm_g, m_mem_norm_g, m_gate_b, m_conv_w, m_conv_b, m_conv_ln_g, m_conv_ln_b, m_rel_bias, m_ffn_norm_g, m_ffn_conv_w, m_ffn_conv_b, m_final_norm_g]
    small_v = [v_mix_norm_g, v_mem_norm_g, v_gate_b, v_conv_w, v_conv_b, v_conv_ln_g, v_conv_ln_b, v_rel_bias, v_ffn_norm_g, v_ffn_conv_w, v_ffn_conv_b, v_final_norm_g]
    sd, sm, sv_ = _ew("adamw_small", _adamw_math, [_pack(small_w), _pack(summed), _pack(small_m), _pack(small_v)], (F32, F32, F32))
    small_d, small_nm, small_nv = _unpack(sd, small_w), _unpack(sm, small_w), _unpack(sv_, small_w)

    order = ["mix_norm_g", "mem_norm_g", "w_in", "gate_b", "conv_w", "conv_b", "conv_ln_g", "conv_ln_b", "w_conv_out",
             "rel_bias", "w_attn_out", "w_mem_kv", "w_mem_out", "w_o", "ffn_norm_g", "w_up", "ffn_conv_w", "ffn_conv_b",
             "w_down", "final_norm_g"]
    small_names = ["mix_norm_g", "mem_norm_g", "gate_b", "conv_w", "conv_b", "conv_ln_g", "conv_ln_b", "rel_bias",
                   "ffn_norm_g", "ffn_conv_w", "ffn_conv_b", "final_norm_g"]

    def collect(bigs, smalls):
        table = dict(zip(names, bigs))
        table.update(zip(small_names, smalls))
        return [table[nm] for nm in order]

    return (loss, grad_x2[None], *collect(big_g, summed), *collect(big_d, small_d), *collect(big_nm, small_nm),
            *collect(big_nv, small_nv))
```

```python
import jax
import jax.numpy as jnp
from jax import lax
from jax.experimental import pallas as pl
from jax.experimental.pallas import tpu as pltpu

F32 = jnp.float32
BF16 = jnp.bfloat16
MESH = pl.DeviceIdType.MESH

CHUNK = 64
LEFT_CHUNKS = 8
BAND = (LEFT_CHUNKS + 1) * CHUNK
BANDP = BAND + CHUNK
MAX_REL = 256
REL_PAD = 640
ATTN_HEAD_DIM = 64
N_MEM_HEADS = 4
QBLK = LEFT_CHUNKS * CHUNK
EPS = 1e-6
NEG_INF = -1e30
LANES = 128
VMEM_LIMIT = 56 * 1024 * 1024

ADAM_LR = 0.001
ADAM_B1 = 0.9
ADAM_B2 = 0.999
ADAM_EPS = 1e-08
ADAM_WD = 0.01
ADAM_STEP = 10


def _pick(dim, target, unit=LANES):
    if dim <= target:
        return dim
    d = (target // unit) * unit
    while d >= unit:
        if dim % d == 0:
            return d
        d -= unit
    raise ValueError(f"no tile for {dim} under {target}")


def _sigmoid(x):
    return 1.0 / (1.0 + jnp.exp(-x))


def _pcall(body, *, name, grid, sc, ins, outs, scratch=(), sem=None):
    arrays = [a for a, _ in ins]
    in_specs = [s for _, s in ins]
    n_in = len(arrays)
    out_shape, out_specs, aliases = [], [], {}
    for k, (o, spec) in enumerate(outs):
        if isinstance(o, jax.ShapeDtypeStruct):
            out_shape.append(o)
        else:
            aliases[1 + len(arrays)] = k
            arrays.append(o)
            in_specs.append(pl.BlockSpec(memory_space=pl.ANY))
            out_shape.append(jax.ShapeDtypeStruct(o.shape, o.dtype))
        out_specs.append(spec)
    n_alias = len(arrays) - n_in

    def wrapped(sc_ref, *refs):
        body(sc_ref, *refs[:n_in], *refs[n_in + n_alias:])

    res = pl.pallas_call(
        wrapped,
        name=name,
        grid_spec=pltpu.PrefetchScalarGridSpec(
            num_scalar_prefetch=1, grid=grid, in_specs=in_specs, out_specs=out_specs, scratch_shapes=list(scratch)),
        out_shape=out_shape,
        input_output_aliases=aliases,
        compiler_params=pltpu.CompilerParams(
            dimension_semantics=sem or ("arbitrary",) * len(grid), vmem_limit_bytes=VMEM_LIMIT),
    )(sc, *arrays)
    return res


def _bs(arr, blk, rc, lead=None):
    nlead = len(arr.shape) - 2

    def imap(*ids):
        sc = ids[-1]
        r, c = rc(*ids)
        return (*(lead(sc) if nlead else ()), r, c)

    return pl.BlockSpec((None,) * nlead + tuple(blk), imap)


def _sds(shape, dtype):
    return jax.ShapeDtypeStruct(tuple(shape), dtype)


L0 = lambda sc: (sc[0],)
L1 = lambda sc: (sc[1],)


def _mm_core(name, grid, sc, a_items, b_item, add_item, out_item, dims, ranges, out_dtype, acc_shape, gather=None):
    nk = grid[2]
    na = len(a_items)
    has_add = add_item is not None
    ng = len(gather[0]) if gather else 0

    def body(sc_ref, *refs):
        a_refs = refs[:na]
        b_ref = refs[na]
        pos = na + 1
        add_ref = refs[pos] if has_add else None
        pos += int(has_add)
        o_ref = refs[pos]
        g_refs = refs[pos + 1:pos + 1 + ng]
        pos += 1 + ng
        acc = refs[pos] if nk > 1 else None
        k = pl.program_id(2)
        if ng:
            sems = refs[pos + int(nk > 1):]
            at = [pl.program_id(d) for d in range(3)]
            first = (at[0] == 0) & (at[1] == 0) & (at[2] == 0)
            last = (at[0] == grid[0] - 1) & (at[1] == grid[1] - 1) & (at[2] == grid[2] - 1)
            nxt = sc_ref[0] + 1
            more = nxt < g_refs[0].shape[0]

            @pl.when(first & more)
            def _():
                _gather_layer_start(g_refs, gather[1], nxt, sems)

        def contrib(ar):
            return lax.dot_general(ar[...].astype(BF16), b_ref[...].astype(BF16), dims, preferred_element_type=F32)

        def fin(val):
            if has_add:
                val = val + add_ref[...]
            o_ref[...] = val.astype(out_dtype)

        if nk == 1:
            fin(contrib(a_refs[0]))
        for p, (k0, k1) in enumerate(ranges if nk > 1 else []):
            @pl.when((k >= k0) & (k < k1))
            def _(p=p):
                part = contrib(a_refs[p])

                @pl.when(k == 0)
                def _():
                    acc[...] = part

                @pl.when((k > 0) & (k < nk - 1))
                def _():
                    acc[...] += part

                @pl.when(k == nk - 1)
                def _():
                    fin(acc[...] + part)

        if ng:
            @pl.when(last & more)
            def _():
                _gather_layer_finish(g_refs, gather[1], nxt, sems)

    ins = list(a_items) + [b_item] + ([add_item] if has_add else [])
    scratch = [pltpu.VMEM(acc_shape, F32)] if nk > 1 else []
    outs = [out_item]
    if ng:
        outs += [(b, HBM_SPEC) for b in gather[0]]
        scratch += [pltpu.SemaphoreType.DMA((ng, 3)) for _ in range(4)]
    res = _pcall(body, name=name, grid=grid, sc=sc, ins=ins, outs=outs, scratch=scratch,
                 sem=("arbitrary",) * 3 if ng else ("parallel", "parallel", "arbitrary"))
    return res if ng else res[0]


def mm_nn(name, sc, a, a_lead, w, M, K, N, out, out_lead=None, out_joff=0, add=None, out_dtype=F32,
          tm=1024, tn=1408, tk=2048, gather=None):
    tm, tn, tk = _pick(M, tm, 8), _pick(N, tn), _pick(K, tk)
    nk = K // tk
    a_item = (a, _bs(a, (tm, tk), lambda i, j, k, sc: (i, k), a_lead))
    b_item = (w, _bs(w, (tk, tn), lambda i, j, k, sc: (k, j), L0))
    add_item = None if add is None else (add, _bs(add, (tm, tn), lambda i, j, k, sc: (i, j)))
    out_spec = _bs(out, (tm, tn), lambda i, j, k, sc: (i, j + out_joff // tn), out_lead)
    return _mm_core(name, (M // tm, N // tn, nk), sc, [a_item], b_item, add_item, (out, out_spec),
                    (((1,), (0,)), ((), ())), [(0, nk)], out_dtype, (tm, tn), gather=gather)


def mm_nt(name, sc, pieces, w, M, N, out, w_koff=0, out_dtype=F32, tm=1024, tn=1408, tk=2048):
    tm, tn = _pick(M, tm, 8), _pick(N, tn)
    widths = [a.shape[-1] for a, _ in pieces]
    tk = _pick(widths[0], tk)
    offs = [w_koff]
    for wd in widths:
        offs.append(offs[-1] + wd)
    while any(x % tk for x in offs):
        tk = _pick(widths[0], tk - LANES)
    ranges, a_items, k0 = [], [], 0
    for (a, lead), wd in zip(pieces, widths):
        n = wd // tk
        ranges.append((k0, k0 + n))
        a_items.append((a, _bs(a, (tm, tk), lambda i, j, k, sc, k0=k0, n=n: (i, jnp.clip(k - k0, 0, n - 1)), lead)))
        k0 += n
    nk = k0
    b_item = (w, _bs(w, (tn, tk), lambda i, j, k, sc: (j, k + w_koff // tk), L0))
    out_spec = _bs(out, (tm, tn), lambda i, j, k, sc: (i, j))
    return _mm_core(name, (M // tm, N // tn, nk), sc, a_items, b_item, None, (out, out_spec),
                    (((1,), (1,)), ((), ())), ranges, out_dtype, (tm, tn))


def mm_tn(name, sc, a, a_lead, b, b_lead, S, K, N, out, out_lead, out_joff=0, tm=1408, tn=1408, tk=2048):
    tm, tn, tk = _pick(K, tm), _pick(N, tn), _pick(S, tk, 8)
    while out_joff % tn:
        tn = _pick(N, tn - LANES)
    nk = S // tk
    a_item = (a, _bs(a, (tk, tm), lambda i, j, k, sc: (k, i), a_lead))
    b_item = (b, _bs(b, (tk, tn), lambda i, j, k, sc: (k, j), b_lead))
    out_spec = _bs(out, (tm, tn), lambda i, j, k, sc: (i, j + out_joff // tn), out_lead)
    return _mm_core(name, (K // tm, N // tn, nk), sc, [a_item], b_item, None, (out, out_spec),
                    (((0,), (0,)), ((), ())), [(0, nk)], F32, (tm, tn))


def rms_fwd(name, sc, h, g3, xn_buf, hs_buf):
    S, D = h.shape
    tr = _pick(S, 256, 8)

    def body(sc_ref, h_ref, g_ref, xn_ref, hs_ref):
        x = h_ref[...]
        r = lax.rsqrt(jnp.mean(x * x, axis=-1, keepdims=True) + EPS)
        xn_ref[...] = (x * r * g_ref[...]).astype(BF16)
        hs_ref[...] = x

    return _pcall(body, name=name, grid=(S // tr,), sc=sc,
                  ins=[(h, _bs(h, (tr, D), lambda i, sc: (i, 0))), (g3, _bs(g3, (1, D), lambda i, sc: (0, 0), L0))],
                  outs=[(xn_buf, _bs(xn_buf, (tr, D), lambda i, sc: (i, 0), L1)),
                        (hs_buf, _bs(hs_buf, (tr, D), lambda i, sc: (i, 0), L1))])


def rms_bwd(name, sc, hs_buf, g3, dy, dres, dg_buf):
    S, D = dy.shape
    tr = _pick(S, 256, 8)

    def body(sc_ref, x_ref, g_ref, dy_ref, dres_ref, dx_ref, dxb_ref, dg_ref):
        i = pl.program_id(0)
        x = x_ref[...]
        r = lax.rsqrt(jnp.mean(x * x, axis=-1, keepdims=True) + EPS)
        xh = x * r
        dyv = dy_ref[...]
        dxh = dyv * g_ref[...]
        dx = r * (dxh - xh * jnp.mean(dxh * xh, axis=-1, keepdims=True)) + dres_ref[...]
        dx_ref[...] = dx
        dxb_ref[...] = dx.astype(BF16)
        part = jnp.sum(dyv * xh, axis=0, keepdims=True)

        @pl.when(i == 0)
        def _():
            dg_ref[...] = part

        @pl.when(i > 0)
        def _():
            dg_ref[...] += part

    return _pcall(body, name=name, grid=(S // tr,), sc=sc,
                  ins=[(hs_buf, _bs(hs_buf, (tr, D), lambda i, sc: (i, 0), L1)),
                       (g3, _bs(g3, (1, D), lambda i, sc: (0, 0), L0)),
                       (dy, _bs(dy, (tr, D), lambda i, sc: (i, 0))),
                       (dres, _bs(dres, (tr, D), lambda i, sc: (i, 0)))],
                  outs=[(_sds((S, D), F32), pl.BlockSpec((tr, D), lambda i, sc: (i, 0))),
                        (_sds((S, D), BF16), pl.BlockSpec((tr, D), lambda i, sc: (i, 0))),
                        (dg_buf, _bs(dg_buf, (1, D), lambda i, sc: (0, 0), L0))])


CONV_HALO = 32
CONV_RB = 32


def _rotations(buf, rots, n):
    for b in range(1, 8):
        rots[b - 1, 0:n, :] = buf[b:b + n, :]


def _shifted(buf, rots, s, r, n):
    b = s % 8
    if b == 0:
        return buf[r + s:r + s + n, :]
    return rots[b - 1, r + s - b:r + s - b + n, :]


def conv_fwd(name, sc, proj, cw, cb, lg, lb, hc_buf, cact_buf, C, W):
    S = proj.shape[1]
    T = _pick(S, 256, CONV_HALO)
    nh = T // CONV_HALO
    off = CONV_HALO - (W - 1)

    def body(sc_ref, ua_ref, ug_ref, pa_ref, pg_ref, w_ref, b_ref, lg_ref, lb_ref, hc_ref, ca_ref, buf, rots):
        i = pl.program_id(0)
        halo = pa_ref[...].astype(F32) * _sigmoid(pg_ref[...].astype(F32))
        buf[0:CONV_HALO, :] = jnp.where(i > 0, halo, 0.0)
        buf[CONV_HALO:CONV_HALO + T, :] = ua_ref[...].astype(F32) * _sigmoid(ug_ref[...].astype(F32))
        _rotations(buf, rots, T + CONV_HALO - 8)
        for r in range(0, T, CONV_RB):
            acc = jnp.zeros((CONV_RB, C), F32) + b_ref[...]
            for k in range(W):
                acc = acc + w_ref[k:k + 1, :] * _shifted(buf, rots, off + k, r, CONV_RB)
            hc_ref[r:r + CONV_RB, :] = acc
        hc = hc_ref[...]
        mu = jnp.mean(hc, axis=-1, keepdims=True)
        d = hc - mu
        rstd = lax.rsqrt(jnp.mean(d * d, axis=-1, keepdims=True) + EPS)
        y = d * rstd * lg_ref[...] + lb_ref[...]
        ca_ref[...] = (y * _sigmoid(y)).astype(BF16)

    vec = lambda a: (a, _bs(a, (1, C), lambda i, sc: (0, 0), L0))
    return _pcall(body, name=name, grid=(S // T,), sc=sc,
                  ins=[(proj, _bs(proj, (T, C), lambda i, sc: (i, 0), L0)),
                       (proj, _bs(proj, (T, C), lambda i, sc: (i, 1), L0)),
                       (proj, _bs(proj, (CONV_HALO, C), lambda i, sc: (jnp.maximum(i * nh - 1, 0), 0), L0)),
                       (proj, _bs(proj, (CONV_HALO, C), lambda i, sc: (jnp.maximum(i * nh - 1, 0), 1), L0)),
                       (cw, _bs(cw, (CONV_HALO, C), lambda i, sc: (0, 0), L0)), vec(cb), vec(lg), vec(lb)],
                  outs=[(hc_buf, _bs(hc_buf, (T, C), lambda i, sc: (i, 0), L0)),
                        (cact_buf, _bs(cact_buf, (T, C), lambda i, sc: (i, 0), L0))],
                  scratch=[pltpu.VMEM((T + CONV_HALO, C), F32), pltpu.VMEM((7, T + CONV_HALO, C), F32)])


def conv_bwd1(name, sc, hc_buf, dcact, lg, lb, dlg_buf, dlb_buf, dcb_buf):
    S, C = dcact.shape
    tr = _pick(S, 256, 8)

    def body(sc_ref, hc_ref, dc_ref, lg_ref, lb_ref, dhc_ref, dlg_ref, dlb_ref, dcb_ref):
        i = pl.program_id(0)
        hc = hc_ref[...]
        mu = jnp.mean(hc, axis=-1, keepdims=True)
        d = hc - mu
        rstd = lax.rsqrt(jnp.mean(d * d, axis=-1, keepdims=True) + EPS)
        yh = d * rstd
        y = yh * lg_ref[...] + lb_ref[...]
        sg = _sigmoid(y)
        dy = dc_ref[...] * (sg * (1.0 + y * (1.0 - sg)))
        dyh = dy * lg_ref[...]
        dhc = rstd * (dyh - jnp.mean(dyh, axis=-1, keepdims=True) - yh * jnp.mean(dyh * yh, axis=-1, keepdims=True))
        dhc_ref[...] = dhc
        parts = (jnp.sum(dy * yh, axis=0, keepdims=True), jnp.sum(dy, axis=0, keepdims=True),
                 jnp.sum(dhc, axis=0, keepdims=True))

        @pl.when(i == 0)
        def _():
            dlg_ref[...], dlb_ref[...], dcb_ref[...] = parts

        @pl.when(i > 0)
        def _():
            dlg_ref[...] += parts[0]
            dlb_ref[...] += parts[1]
            dcb_ref[...] += parts[2]

    vec = lambda a: (a, _bs(a, (1, C), lambda i, sc: (0, 0), L0))
    return _pcall(body, name=name, grid=(S // tr,), sc=sc,
                  ins=[(hc_buf, _bs(hc_buf, (tr, C), lambda i, sc: (i, 0), L0)),
                       (dcact, _bs(dcact, (tr, C), lambda i, sc: (i, 0))), vec(lg), vec(lb)],
                  outs=[(_sds((S, C), F32), pl.BlockSpec((tr, C), lambda i, sc: (i, 0))),
                        vec(dlg_buf), vec(dlb_buf), vec(dcb_buf)])


def conv_bwd2(name, sc, proj, dhc, cw, dcw_buf, C, W):
    S = proj.shape[1]
    T = _pick(S, 256, CONV_HALO)
    nh = T // CONV_HALO
    nt = S // T
    off = CONV_HALO - (W - 1)

    def body(sc_ref, ua_ref, ug_ref, pa_ref, pg_ref, d_ref, dn_ref, w_ref, o_ref, dw_ref, buf, dbuf, dhg, rots, drots):
        i = pl.program_id(0)
        halo = pa_ref[...].astype(F32) * _sigmoid(pg_ref[...].astype(F32))
        buf[0:CONV_HALO, :] = jnp.where(i > 0, halo, 0.0)
        sg = _sigmoid(ug_ref[...].astype(F32))
        ua = ua_ref[...].astype(F32)
        buf[CONV_HALO:CONV_HALO + T, :] = ua * sg
        dbuf[0:T, :] = d_ref[...]
        dbuf[T:T + CONV_HALO, :] = jnp.where(i < nt - 1, dn_ref[...], 0.0)
        _rotations(buf, rots, T + CONV_HALO - 8)
        _rotations(dbuf, drots, T + CONV_HALO - 8)

        @pl.when(i == 0)
        def _():
            dw_ref[...] = jnp.zeros_like(dw_ref)

        for r in range(0, T, CONV_RB):
            acc = jnp.zeros((CONV_RB, C), F32)
            for k in range(W):
                acc = acc + w_ref[k:k + 1, :] * _shifted(dbuf, drots, (W - 1) - k, r, CONV_RB)
            dhg[r:r + CONV_RB, :] = acc
        for k in range(W):
            dw_ref[k:k + 1, :] += jnp.sum(dbuf[0:T, :] * _shifted(buf, rots, off + k, 0, T), axis=0, keepdims=True)
        dh = dhg[...]
        o_ref[:, 0:C] = (dh * sg).astype(BF16)
        o_ref[:, C:2 * C] = (dh * ua * sg * (1.0 - sg)).astype(BF16)

    return _pcall(body, name=name, grid=(nt,), sc=sc,
                  ins=[(proj, _bs(proj, (T, C), lambda i, sc: (i, 0), L0)),
                       (proj, _bs(proj, (T, C), lambda i, sc: (i, 1), L0)),
                       (proj, _bs(proj, (CONV_HALO, C), lambda i, sc: (jnp.maximum(i * nh - 1, 0), 0), L0)),
                       (proj, _bs(proj, (CONV_HALO, C), lambda i, sc: (jnp.maximum(i * nh - 1, 0), 1), L0)),
                       (dhc, _bs(dhc, (T, C), lambda i, sc: (i, 0))),
                       (dhc, _bs(dhc, (CONV_HALO, C), lambda i, sc: (jnp.minimum((i + 1) * nh, S // CONV_HALO - 1), 0))),
                       (cw, _bs(cw, (CONV_HALO, C), lambda i, sc: (0, 0), L0))],
                  outs=[(_sds((S, 2 * C), BF16), pl.BlockSpec((T, 2 * C), lambda i, sc: (i, 0))),
                        (dcw_buf, _bs(dcw_buf, (CONV_HALO, C), lambda i, sc: (0, 0), L0))],
                  scratch=[pltpu.VMEM((T + CONV_HALO, C), F32), pltpu.VMEM((T + CONV_HALO, C), F32),
                           pltpu.VMEM((T, C), F32), pltpu.VMEM((7, T + CONV_HALO, C), F32),
                           pltpu.VMEM((7, T + CONV_HALO, C), F32)])


def gate_fwd(name, sc, proj, gb3, y_buf, merged_buf, D):
    S = proj.shape[1]
    tr, tc = _pick(S, 256, 8), _pick(D, 1024)
    nc = D // tc

    def body(sc_ref, g0, g1, g2, b0, b1, b2, y0, y1, y2, o_ref):
        acc = None
        for g, b, y in ((g0, b0, y0), (g1, b1, y1), (g2, b2, y2)):
            t = _sigmoid(g[...].astype(F32) + b[...]) * y[...].astype(F32)
            acc = t if acc is None else acc + t
        o_ref[...] = acc.astype(BF16)

    ins = [(proj, _bs(proj, (tr, tc), lambda i, j, sc, b=b: (i, (3 + b) * nc + j), L0)) for b in range(3)]
    ins += [(gb3, _bs(gb3, (1, tc), lambda i, j, sc, b=b: (0, b * nc + j), L0)) for b in range(3)]
    ins += [(y_buf, _bs(y_buf, (tr, tc), lambda i, j, sc: (i, j), lambda sc, b=b: (sc[0], b))) for b in range(3)]
    return _pcall(body, name=name, grid=(S // tr, nc), sc=sc, ins=ins,
                  outs=[(merged_buf, _bs(merged_buf, (tr, tc), lambda i, j, sc: (i, j), L0))])


def gate_bwd(name, sc, proj, gb3, y_buf, dmerged, dgb_buf, D):
    S = proj.shape[1]
    tr, tc = _pick(S, 256, 8), _pick(D, 1024)
    nc = D // tc

    def body(sc_ref, g_ref, b_ref, y_ref, dm_ref, dy_ref, dg_ref, dgb_ref):
        i = pl.program_id(2)
        g = _sigmoid(g_ref[...].astype(F32) + b_ref[...])
        dm = dm_ref[...]
        dy_ref[...] = (dm * g).astype(BF16)
        dgt = dm * y_ref[...].astype(F32) * g * (1.0 - g)
        dg_ref[...] = dgt.astype(BF16)
        part = jnp.sum(dgt, axis=0, keepdims=True)

        @pl.when(i == 0)
        def _():
            dgb_ref[...] = part

        @pl.when(i > 0)
        def _():
            dgb_ref[...] += part

    dy_sds = _sds((3, S, D), BF16)
    return _pcall(body, name=name, grid=(3, nc, S // tr), sc=sc,
                  ins=[(proj, _bs(proj, (tr, tc), lambda b, j, i, sc: (i, (3 + b) * nc + j), L0)),
                       (gb3, _bs(gb3, (1, tc), lambda b, j, i, sc: (0, b * nc + j), L0)),
                       (y_buf, pl.BlockSpec((None, None, tr, tc), lambda b, j, i, sc: (sc[0], b, i, j))),
                       (dmerged, _bs(dmerged, (tr, tc), lambda b, j, i, sc: (i, j)))],
                  outs=[(dy_sds, pl.BlockSpec((None, tr, tc), lambda b, j, i, sc: (b, i, j))),
                        (_sds((S, 3 * D), BF16), pl.BlockSpec((tr, tc), lambda b, j, i, sc: (i, b * nc + j))),
                        (dgb_buf, _bs(dgb_buf, (1, tc), lambda b, j, i, sc: (0, b * nc + j), L0))])


FFN_HALO = 8
FFN_RB = 16


def _taps(buf, w_ref, init, start, n, W, reverse=False):
    acc = init
    for k in range(W):
        s = start + ((W - 1 - k) if reverse else k)
        acc = acc + w_ref[k:k + 1, :] * buf[s:s + n, :]
    return acc


def ffn_fwd(name, sc, up, fw, fb3, act_buf, Fh, W):
    S = up.shape[1]
    T, tc = _pick(S, 512, FFN_RB), _pick(Fh, 512)
    nf = Fh // tc
    nh = T // FFN_HALO
    off = FFN_HALO - (W - 1)

    def body(sc_ref, v_ref, g_ref, pv_ref, pg_ref, wv_ref, wg_ref, bv_ref, bg_ref, o_ref, bufv, bufg):
        i = pl.program_id(1)
        for m_ref, p_ref, buf in ((v_ref, pv_ref, bufv), (g_ref, pg_ref, bufg)):
            buf[0:FFN_HALO, :] = jnp.where(i > 0, p_ref[...], 0.0)
            buf[FFN_HALO:FFN_HALO + T, :] = m_ref[...]
        for r in range(0, T, FFN_RB):
            val = _taps(bufv, wv_ref, bv_ref[...], r + off, FFN_RB, W)
            gt = _taps(bufg, wg_ref, bg_ref[...], r + off, FFN_RB, W)
            o_ref[r:r + FFN_RB, :] = (gt * _sigmoid(gt) * val).astype(BF16)

    prev = lambda i: jnp.maximum(i * nh - 1, 0)
    return _pcall(body, name=name, grid=(nf, S // T), sc=sc,
                  ins=[(up, _bs(up, (T, tc), lambda j, i, sc: (i, j), L0)),
                       (up, _bs(up, (T, tc), lambda j, i, sc: (i, j + nf), L0)),
                       (up, _bs(up, (FFN_HALO, tc), lambda j, i, sc: (prev(i), j), L0)),
                       (up, _bs(up, (FFN_HALO, tc), lambda j, i, sc: (prev(i), j + nf), L0)),
                       (fw, _bs(fw, (FFN_HALO, tc), lambda j, i, sc: (0, j), L0)),
                       (fw, _bs(fw, (FFN_HALO, tc), lambda j, i, sc: (0, j + nf), L0)),
                       (fb3, _bs(fb3, (1, tc), lambda j, i, sc: (0, j), L0)),
                       (fb3, _bs(fb3, (1, tc), lambda j, i, sc: (0, j + nf), L0))],
                  outs=[(act_buf, _bs(act_buf, (T, tc), lambda j, i, sc: (i, j), L0))],
                  scratch=[pltpu.VMEM((T + FFN_HALO, tc), F32), pltpu.VMEM((T + FFN_HALO, tc), F32)])


def ffn_bwd(name, sc, up, dact, fw, fb3, dfw_v_buf, dfw_g_buf, dfb_v_buf, dfb_g_buf, Fh, W):
    S = up.shape[1]
    T, tc = _pick(S, 512, FFN_RB), _pick(Fh, 512)
    nf = Fh // tc
    nh = T // FFN_HALO
    nt = S // T
    off = FFN_HALO - (W - 1)
    TE = T + FFN_HALO

    def body(sc_ref, v_ref, g_ref, pv_ref, pg_ref, nv_ref, ng_ref, d_ref, dn_ref, wv_ref, wg_ref, bv_ref, bg_ref,
             ov_ref, og_ref, dwv_ref, dwg_ref, dbv_ref, dbg_ref, bufv, bufg, dv, dg):
        i = pl.program_id(1)
        for m_ref, p_ref, n_ref, buf in ((v_ref, pv_ref, nv_ref, bufv), (g_ref, pg_ref, ng_ref, bufg)):
            buf[0:FFN_HALO, :] = jnp.where(i > 0, p_ref[...], 0.0)
            buf[FFN_HALO:FFN_HALO + T, :] = m_ref[...]
            buf[FFN_HALO + T:FFN_HALO + TE, :] = n_ref[...]

        def d_up(r, n, d):
            val = _taps(bufv, wv_ref, bv_ref[...], r + off, n, W)
            gt = _taps(bufg, wg_ref, bg_ref[...], r + off, n, W)
            sg = _sigmoid(gt)
            dv[r:r + n, :] = d * gt * sg
            dg[r:r + n, :] = d * val * (sg * (1.0 + gt * (1.0 - sg)))

        for r in range(0, T, FFN_RB):
            d_up(r, FFN_RB, d_ref[r:r + FFN_RB, :])
        d_up(T, FFN_HALO, jnp.where(i < nt - 1, dn_ref[...], 0.0))

        @pl.when(i == 0)
        def _():
            for r in (dwv_ref, dwg_ref, dbv_ref, dbg_ref):
                r[...] = jnp.zeros_like(r)

        fold = lambda a: a[0:8, :] + a[8:16, :]
        for dsrc, w_ref, buf, o_ref, dw_ref, db_ref in ((dv, wv_ref, bufv, ov_ref, dwv_ref, dbv_ref),
                                                      (dg, wg_ref, bufg, og_ref, dwg_ref, dbg_ref)):
            acc_w = [jnp.zeros((8, tc), F32) for _ in range(W)]
            acc_b = jnp.zeros((8, tc), F32)
            for r in range(0, T, FFN_RB):
                o_ref[r:r + FFN_RB, :] = _taps(dsrc, w_ref, 0.0, r, FFN_RB, W, reverse=True).astype(BF16)
                dm = dsrc[r:r + FFN_RB, :]
                for k in range(W):
                    acc_w[k] = acc_w[k] + fold(dm * buf[r + off + k:r + off + k + FFN_RB, :])
                acc_b = acc_b + fold(dm)
            for k in range(W):
                dw_ref[k:k + 1, :] += jnp.sum(acc_w[k], axis=0, keepdims=True)
            db_ref[...] += jnp.sum(acc_b, axis=0, keepdims=True)

    prev = lambda i: jnp.maximum(i * nh - 1, 0)
    nxt = lambda i: jnp.minimum((i + 1) * nh, S // FFN_HALO - 1)
    o_sds = _sds((S, Fh), BF16)
    return _pcall(body, name=name, grid=(nf, nt), sc=sc,
                  ins=[(up, _bs(up, (T, tc), lambda j, i, sc: (i, j), L0)),
                       (up, _bs(up, (T, tc), lambda j, i, sc: (i, j + nf), L0)),
                       (up, _bs(up, (FFN_HALO, tc), lambda j, i, sc: (prev(i), j), L0)),
                       (up, _bs(up, (FFN_HALO, tc), lambda j, i, sc: (prev(i), j + nf), L0)),
                       (up, _bs(up, (FFN_HALO, tc), lambda j, i, sc: (nxt(i), j), L0)),
                       (up, _bs(up, (FFN_HALO, tc), lambda j, i, sc: (nxt(i), j + nf), L0)),
                       (dact, _bs(dact, (T, tc), lambda j, i, sc: (i, j))),
                       (dact, _bs(dact, (FFN_HALO, tc), lambda j, i, sc: (nxt(i), j))),
                       (fw, _bs(fw, (FFN_HALO, tc), lambda j, i, sc: (0, j), L0)),
                       (fw, _bs(fw, (FFN_HALO, tc), lambda j, i, sc: (0, j + nf), L0)),
                       (fb3, _bs(fb3, (1, tc), lambda j, i, sc: (0, j), L0)),
                       (fb3, _bs(fb3, (1, tc), lambda j, i, sc: (0, j + nf), L0))],
                  outs=[(o_sds, pl.BlockSpec((T, tc), lambda j, i, sc: (i, j))),
                        (o_sds, pl.BlockSpec((T, tc), lambda j, i, sc: (i, j))),
                        (dfw_v_buf, _bs(dfw_v_buf, (FFN_HALO, tc), lambda j, i, sc: (0, j), L0)),
                        (dfw_g_buf, _bs(dfw_g_buf, (FFN_HALO, tc), lambda j, i, sc: (0, j), L0)),
                        (dfb_v_buf, _bs(dfb_v_buf, (1, tc), lambda j, i, sc: (0, j), L0)),
                        (dfb_g_buf, _bs(dfb_g_buf, (1, tc), lambda j, i, sc: (0, j), L0))],
                  scratch=[pltpu.VMEM((T + 2 * FFN_HALO, tc), F32), pltpu.VMEM((T + 2 * FFN_HALO, tc), F32),
                           pltpu.VMEM((T + 2 * FFN_HALO, tc), F32), pltpu.VMEM((T + 2 * FFN_HALO, tc), F32)],
                  sem=("parallel", "arbitrary"))


def final_loss(name, sc, h, g2, target):
    S, D = h.shape
    tr = _pick(S, 256, 8)

    def body(sc_ref, h_ref, g_ref, t_ref, loss_ref, dx_ref, dxb_ref, dg_ref):
        i = pl.program_id(0)
        x = h_ref[...]
        r = lax.rsqrt(jnp.mean(x * x, axis=-1, keepdims=True) + EPS)
        xh = x * r
        err = xh * g_ref[...] - t_ref[...]
        part_loss = 0.5 * jnp.sum(jnp.mean(err * err, axis=-1, keepdims=True), axis=0, keepdims=True)
        dy = err * (1.0 / D)
        dxh = dy * g_ref[...]
        dx = r * (dxh - xh * jnp.mean(dxh * xh, axis=-1, keepdims=True))
        dx_ref[...] = dx
        dxb_ref[...] = dx.astype(BF16)
        part_g = jnp.sum(dy * xh, axis=0, keepdims=True)

        @pl.when(i == 0)
        def _():
            loss_ref[...] = jnp.zeros_like(loss_ref) + part_loss
            dg_ref[...] = part_g

        @pl.when(i > 0)
        def _():
            loss_ref[...] += part_loss
            dg_ref[...] += part_g

    row = lambda a: (a, pl.BlockSpec((tr, D), lambda i, sc: (i, 0)))
    return _pcall(body, name=name, grid=(S // tr,), sc=sc,
                  ins=[row(h), (g2, pl.BlockSpec((1, D), lambda i, sc: (0, 0))), row(target)],
                  outs=[(_sds((8, LANES), F32), pl.BlockSpec((8, LANES), lambda i, sc: (0, 0))),
                        row(_sds((S, D), F32)), row(_sds((S, D), BF16)),
                        (_sds((1, D), F32), pl.BlockSpec((1, D), lambda i, sc: (0, 0)))])


def _split3(a):
    hi = a.astype(BF16)
    r1 = a - hi.astype(F32)
    mid = r1.astype(BF16)
    lo = (r1 - mid.astype(F32)).astype(BF16)
    return hi, mid, lo


def _rel_onehot(qi, rows_are_keys):
    shape = (BANDP, REL_PAD) if rows_are_keys else (REL_PAD, BANDP)
    km = lax.broadcasted_iota(jnp.int32, shape, 0 if rows_are_keys else 1)
    idx = lax.broadcasted_iota(jnp.int32, shape, 1 if rows_are_keys else 0)
    rel = jnp.clip(LEFT_CHUNKS * CHUNK + qi - km, -MAX_REL, MAX_REL) + MAX_REL
    return jnp.where(rel == idx, 1.0, 0.0).astype(BF16)


def bias_expand(name, sc, rel3, H):
    def body(sc_ref, rel_ref, o_ref):
        qi = pl.program_id(0)
        oh = _rel_onehot(qi, False)
        acc = jnp.zeros((H, BANDP), F32)
        for part in _split3(rel_ref[...]):
            acc = acc + jnp.dot(part, oh, preferred_element_type=F32)
        o_ref[...] = acc

    return _pcall(body, name=name, grid=(CHUNK,), sc=sc,
                  ins=[(rel3, _bs(rel3, (H, REL_PAD), lambda q, sc: (0, 0), L0))],
                  outs=[(_sds((CHUNK, H, BANDP), F32), pl.BlockSpec((None, H, BANDP), lambda q, sc: (q, 0, 0)))])[0]


def bias_reduce(name, sc, dbias_q, drel_buf, H):
    def body(sc_ref, d_ref, o_ref):
        qi = pl.program_id(0)
        oh = _rel_onehot(qi, True)
        acc = jnp.zeros((H, REL_PAD), F32)
        for part in _split3(d_ref[...]):
            acc = acc + jnp.dot(part, oh, preferred_element_type=F32)

        @pl.when(qi == 0)
        def _():
            o_ref[...] = acc

        @pl.when(qi > 0)
        def _():
            o_ref[...] += acc

    return _pcall(body, name=name, grid=(CHUNK,), sc=sc,
                  ins=[(dbias_q, pl.BlockSpec((None, H, BANDP), lambda q, sc: (q, 0, 0)))],
                  outs=[(drel_buf, _bs(drel_buf, (H, REL_PAD), lambda q, sc: (0, 0), L0))])[0]


def _head_blocks(a2):
    lo = lax.broadcasted_iota(jnp.int32, a2.shape, 1) < ATTN_HEAD_DIM
    zero = jnp.zeros_like(a2)
    return jnp.concatenate([jnp.where(lo, a2, zero), jnp.where(lo, zero, a2)], axis=0)


def _head_diag(r):
    lo = lax.broadcasted_iota(jnp.int32, (CHUNK, LANES), 1) < ATTN_HEAD_DIM
    return jnp.where(lo, r[0:CHUNK], r[CHUNK:2 * CHUNK])


def _attn_probs_t(q2, k2, bias_t, c, first, scale):
    qbd = _head_blocks(q2)
    s = lax.dot_general(k2, qbd, (((1,), (1,)), ((), ())), preferred_element_type=F32) * scale + bias_t
    km = lax.broadcasted_iota(jnp.int32, s.shape, 0)
    valid = (km < BAND) & (jnp.logical_not(first) | (km + c * CHUNK >= QBLK))
    s = jnp.where(valid, s, NEG_INF)
    p = jnp.exp(s - jnp.max(s, axis=0, keepdims=True))
    return qbd, p / jnp.sum(p, axis=0, keepdims=True)


def bias_to_lanes(bias_q, npair):
    t = jnp.transpose(bias_q, (1, 2, 0)).reshape(npair, 2, BANDP, CHUNK)
    return jnp.transpose(t, (0, 2, 1, 3)).reshape(npair, BANDP, 2 * CHUNK)


def bias_from_lanes(bias_t, npair):
    t = jnp.transpose(bias_t.reshape(npair, BANDP, 2, CHUNK), (3, 0, 2, 1))
    return t.reshape(CHUNK, 2 * npair, BANDP)


def _fill_window(win, prev_ref, cur_ref, A):
    win[0:QBLK, :] = prev_ref[...].astype(BF16)
    win[QBLK:2 * QBLK, :] = cur_ref[...].astype(BF16)
    win[2 * QBLK:2 * QBLK + CHUNK, :] = jnp.zeros((CHUNK, A), BF16)


def attn_fwd(name, sc, proj, bias_buf, ao_buf, A, qcol):
    S = proj.shape[1]
    nb = S // QBLK
    H = A // ATTN_HEAD_DIM
    npair = A // LANES
    scale = ATTN_HEAD_DIM ** -0.5

    def body(sc_ref, q_ref, kp_ref, kc_ref, vp_ref, vc_ref, b_ref, o_ref, kw, vw):
        first = pl.program_id(0) == 0
        _fill_window(kw, kp_ref, kc_ref, A)
        _fill_window(vw, vp_ref, vc_ref, A)

        def chunk(c, carry):
            r0 = pl.multiple_of(c * CHUNK, CHUNK)
            for hp in range(npair):
                cols = slice(hp * LANES, (hp + 1) * LANES)
                q2 = q_ref[pl.ds(r0, CHUNK), cols].astype(BF16)
                k2 = kw[pl.ds(r0, BANDP), cols]
                v2 = vw[pl.ds(r0, BANDP), cols]
                _, p = _attn_probs_t(q2, k2, b_ref[hp], c, first, scale)
                o = lax.dot_general(p.astype(BF16), v2, (((0,), (0,)), ((), ())), preferred_element_type=F32)
                o_ref[pl.ds(r0, CHUNK), cols] = _head_diag(o).astype(BF16)
            return carry

        lax.fori_loop(0, LEFT_CHUNKS, chunk, 0, unroll=2)

    prevb = lambda i: jnp.maximum(i - 1, 0)
    blk = lambda rowf, col: (proj, _bs(proj, (QBLK, A), lambda i, sc: (rowf(i), col), L0))
    same = lambda i: i
    return _pcall(body, name=name, grid=(nb,), sc=sc,
                  ins=[blk(same, qcol), blk(prevb, qcol + 1), blk(same, qcol + 1), blk(prevb, qcol + 2), blk(same, qcol + 2),
                       (bias_buf, pl.BlockSpec((None, npair, BANDP, LANES), lambda i, sc: (sc[0], 0, 0, 0)))],
                  outs=[(ao_buf, _bs(ao_buf, (QBLK, A), lambda i, sc: (i, 0), L0))],
                  scratch=[pltpu.VMEM((2 * QBLK + CHUNK, A), BF16), pltpu.VMEM((2 * QBLK + CHUNK, A), BF16)])


def attn_bwd(name, sc, proj, bias_buf, dao, A, qcol):
    S = proj.shape[1]
    nb = S // QBLK
    H = A // ATTN_HEAD_DIM
    npair = A // LANES
    scale = ATTN_HEAD_DIM ** -0.5
    WIN = 2 * QBLK + CHUNK

    def body(sc_ref, q_ref, kp_ref, kc_ref, vp_ref, vc_ref, b_ref, do_ref, dq_ref, dk_ref, dv_ref, db_ref, kw, vw, dkw, dvw):
        i = pl.program_id(0)
        first = i == 0

        @pl.when(first)
        def _():
            dkw[0:QBLK, :] = jnp.zeros((QBLK, A), F32)
            dvw[0:QBLK, :] = jnp.zeros((QBLK, A), F32)
            db_ref[...] = jnp.zeros_like(db_ref)

        @pl.when(i > 0)
        def _():
            dkw[0:QBLK, :] = dkw[QBLK:2 * QBLK, :]
            dvw[0:QBLK, :] = dvw[QBLK:2 * QBLK, :]

        dkw[QBLK:WIN, :] = jnp.zeros((WIN - QBLK, A), F32)
        dvw[QBLK:WIN, :] = jnp.zeros((WIN - QBLK, A), F32)

        @pl.when(i < nb)
        def _():
            _fill_window(kw, kp_ref, kc_ref, A)
            _fill_window(vw, vp_ref, vc_ref, A)

            def chunk(c, carry):
                r0 = pl.multiple_of(c * CHUNK, CHUNK)
                for hp in range(npair):
                    cols = slice(hp * LANES, (hp + 1) * LANES)
                    q2 = q_ref[pl.ds(r0, CHUNK), cols].astype(BF16)
                    k2 = kw[pl.ds(r0, BANDP), cols]
                    v2 = vw[pl.ds(r0, BANDP), cols]
                    dobd = _head_blocks(do_ref[pl.ds(r0, CHUNK), cols])
                    qbd, p = _attn_probs_t(q2, k2, b_ref[hp], c, first, scale)
                    dp = lax.dot_general(v2, dobd, (((1,), (1,)), ((), ())), preferred_element_type=F32)
                    ds = p * (dp - jnp.sum(dp * p, axis=0, keepdims=True))
                    db_ref[hp] += ds
                    dsb = ds.astype(BF16)
                    dq = lax.dot_general(dsb, k2, (((0,), (0,)), ((), ())), preferred_element_type=F32) * scale
                    dq_ref[pl.ds(r0, CHUNK), cols] = _head_diag(dq).astype(BF16)
                    dkw[pl.ds(r0, BANDP), cols] += jnp.dot(dsb, qbd, preferred_element_type=F32) * scale
                    dvw[pl.ds(r0, BANDP), cols] += jnp.dot(p.astype(BF16), dobd, preferred_element_type=F32)
                return carry

            lax.fori_loop(0, LEFT_CHUNKS, chunk, 0, unroll=2)

        dk_ref[...] = dkw[0:QBLK, :].astype(BF16)
        dv_ref[...] = dvw[0:QBLK, :].astype(BF16)

    cur = lambda i: jnp.minimum(i, nb - 1)
    prevb = lambda i: jnp.maximum(jnp.minimum(i, nb - 1) - 1, 0)
    done = lambda i: jnp.maximum(i - 1, 0)
    blk = lambda rowf, col: (proj, _bs(proj, (QBLK, A), lambda i, sc: (rowf(i), col), L0))
    o_sds = _sds((S, A), BF16)
    return _pcall(body, name=name, grid=(nb + 1,), sc=sc,
                  ins=[blk(cur, qcol), blk(prevb, qcol + 1), blk(cur, qcol + 1), blk(prevb, qcol + 2), blk(cur, qcol + 2),
                       (bias_buf, pl.BlockSpec((None, npair, BANDP, LANES), lambda i, sc: (sc[0], 0, 0, 0))),
                       (dao, pl.BlockSpec((QBLK, A), lambda i, sc: (cur(i), 0)))],
                  outs=[(o_sds, pl.BlockSpec((QBLK, A), lambda i, sc: (cur(i), 0))),
                        (o_sds, pl.BlockSpec((QBLK, A), lambda i, sc: (done(i), 0))),
                        (o_sds, pl.BlockSpec((QBLK, A), lambda i, sc: (done(i), 0))),
                        (_sds((npair, BANDP, LANES), F32), pl.BlockSpec((npair, BANDP, LANES), lambda i, sc: (0, 0, 0)))],
                  scratch=[pltpu.VMEM((WIN, A), BF16), pltpu.VMEM((WIN, A), BF16),
                           pltpu.VMEM((WIN, A), F32), pltpu.VMEM((WIN, A), F32)])


def mem_fwd(name, sc, proj, kv_buf, mo_buf, Dm, qcol):
    S = proj.shape[1]
    NM = kv_buf.shape[1]
    tr = _pick(S, 512, 8)
    hd = Dm // N_MEM_HEADS
    scale = hd ** -0.5

    def body(sc_ref, q_ref, kv_ref, o_ref):
        for h in range(N_MEM_HEADS):
            cols = slice(h * hd, (h + 1) * hd)
            q = q_ref[:, cols].astype(BF16)
            k = kv_ref[:, cols].astype(BF16)
            v = kv_ref[:, Dm + h * hd:Dm + (h + 1) * hd].astype(BF16)
            s = lax.dot_general(q, k, (((1,), (1,)), ((), ())), preferred_element_type=F32) * scale
            p = jnp.exp(s - jnp.max(s, axis=-1, keepdims=True))
            p = p / jnp.sum(p, axis=-1, keepdims=True)
            o_ref[:, cols] = jnp.dot(p.astype(BF16), v, preferred_element_type=F32).astype(BF16)

    return _pcall(body, name=name, grid=(S // tr,), sc=sc,
                  ins=[(proj, _bs(proj, (tr, Dm), lambda i, sc: (i, qcol), L0)),
                       (kv_buf, _bs(kv_buf, (NM, 2 * Dm), lambda i, sc: (0, 0), L0))],
                  outs=[(mo_buf, _bs(mo_buf, (tr, Dm), lambda i, sc: (i, 0), L0))])


def mem_bwd(name, sc, proj, kv_buf, dmo, Dm, qcol):
    S = proj.shape[1]
    NM = kv_buf.shape[1]
    tr = _pick(S, 512, 8)
    hd = Dm // N_MEM_HEADS
    scale = hd ** -0.5

    def body(sc_ref, q_ref, kv_ref, do_ref, dq_ref, dkv_ref):
        i = pl.program_id(0)

        @pl.when(i == 0)
        def _():
            dkv_ref[...] = jnp.zeros_like(dkv_ref)

        for h in range(N_MEM_HEADS):
            cols = slice(h * hd, (h + 1) * hd)
            vcols = slice(Dm + h * hd, Dm + (h + 1) * hd)
            q = q_ref[:, cols].astype(BF16)
            k = kv_ref[:, cols].astype(BF16)
            v = kv_ref[:, vcols].astype(BF16)
            do = do_ref[:, cols]
            s = lax.dot_general(q, k, (((1,), (1,)), ((), ())), preferred_element_type=F32) * scale
            p = jnp.exp(s - jnp.max(s, axis=-1, keepdims=True))
            p = p / jnp.sum(p, axis=-1, keepdims=True)
            dp = lax.dot_general(do, v, (((1,), (1,)), ((), ())), preferred_element_type=F32)
            ds = p * (dp - jnp.sum(dp * p, axis=-1, keepdims=True))
            dsb = ds.astype(BF16)
            dq_ref[:, cols] = (jnp.dot(dsb, k, preferred_element_type=F32) * scale).astype(BF16)
            dkv_ref[:, cols] += lax.dot_general(dsb, q, (((0,), (0,)), ((), ())), preferred_element_type=F32) * scale
            dkv_ref[:, vcols] += lax.dot_general(p.astype(BF16), do, (((0,), (0,)), ((), ())), preferred_element_type=F32)

    return _pcall(body, name=name, grid=(S // tr,), sc=sc,
                  ins=[(proj, _bs(proj, (tr, Dm), lambda i, sc: (i, qcol), L0)),
                       (kv_buf, _bs(kv_buf, (NM, 2 * Dm), lambda i, sc: (0, 0), L0)),
                       (dmo, pl.BlockSpec((tr, Dm), lambda i, sc: (i, 0)))],
                  outs=[(_sds((S, Dm), BF16), pl.BlockSpec((tr, Dm), lambda i, sc: (i, 0))),
                        (_sds((NM, 2 * Dm), F32), pl.BlockSpec((NM, 2 * Dm), lambda i, sc: (0, 0)))])


def _rows2d(a):
    return a.reshape(-1, a.shape[-1])


def _ew(name, fn, ins, out_dtypes):
    R, C = ins[0].shape
    tc = _pick(C, 2048)
    tr = _pick(R, max(8, (1 << 19) // tc), 8)
    n_in = len(ins)

    def body(sc_ref, *refs):
        outs = fn(*[r[...] for r in refs[:n_in]])
        for o_ref, o in zip(refs[n_in:], outs):
            o_ref[...] = o.astype(o_ref.dtype)

    plain = pl.BlockSpec((tr, tc), lambda i, j, sc: (i, j))
    return _pcall(body, name=name, grid=(R // tr, C // tc), sc=jnp.zeros((1,), jnp.int32), ins=[(a, plain) for a in ins],
                  outs=[(_sds((R, C), dt), plain) for dt in out_dtypes], sem=("parallel", "parallel"))


def _ew3(name, fn, sc, dims, ins, outs):
    G, R, C = dims
    tc = _pick(C, 2048)
    tr = _pick(R, max(16, (1 << 19) // tc), 16)
    n_in = len(ins)

    def body(sc_ref, *refs):
        res = fn(*[r[...] for r in refs[:n_in]])
        for o_ref, o in zip(refs[n_in:], res):
            o_ref[...] = o.astype(o_ref.dtype)

    def spec(arr, index):
        nlead = len(arr.shape) - 2

        def imap(g, i, j, s):
            lead, ro, co = index(g, s)
            return (*lead, i + ro // tr, j + co // tc)

        return pl.BlockSpec((None,) * nlead + (tr, tc), imap)

    return _pcall(body, name=name, grid=(G, R // tr, C // tc), sc=sc,
                  ins=[(a, spec(a, ix)) for a, ix in ins], outs=[(o, spec(o, ix)) for o, ix in outs],
                  sem=("parallel", "parallel", "parallel"))


def _at(lead_fn=None, ax=None, size=0):
    def index(g, s):
        lead = (g,) if lead_fn is None else lead_fn(g, s)
        off = s[0] * size
        return lead, (off if ax == 1 else 0), (off if ax == 2 else 0)
    return index


def _adamw_math(w, g, m, v):
    m = ADAM_B1 * m + (1.0 - ADAM_B1) * g
    v = ADAM_B2 * v + (1.0 - ADAM_B2) * (g * g)
    m_hat = m / (1.0 - ADAM_B1 ** ADAM_STEP)
    v_hat = v / (1.0 - ADAM_B2 ** ADAM_STEP)
    delta = -ADAM_LR * (m_hat / (jnp.sqrt(v_hat) + ADAM_EPS) + ADAM_WD * w)
    return delta, m, v


def adamw(name, w, g, m, v):
    shp = w.shape
    d, nm, nv = _ew(name, _adamw_math, [_rows2d(w), _rows2d(g), _rows2d(m), _rows2d(v)], (F32, F32, F32))
    return d.reshape(shp), nm.reshape(shp), nv.reshape(shp)


def _place():
    x, y, c = lax.axis_index("x"), lax.axis_index("y"), lax.axis_index("c")
    chips = [(1 - x, y), (x, 1 - y), (1 - x, 1 - y)]
    return x, y, c, chips


def _sub(ref, ax=None, s=None, n=None, half=None, lh=None):
    lay = slice(None) if half is None else pl.ds(half * lh, lh)
    if ax is None:
        return ref.at[lay]
    cut = pl.ds(pl.multiple_of(s * n, n), n)
    return ref.at[lay, cut, :] if ax == 1 else ref.at[lay, :, cut]


HBM_SPEC = pl.BlockSpec(memory_space=pl.ANY)


def _slab(ref, l, ax, s, half):
    K, N = ref.shape[1], ref.shape[2]
    if ax == 1:
        kh = K // 8
        return ref.at[l, pl.ds(pl.multiple_of((2 * s + half) * kh, kh), kh), :]
    kh, ns = K // 2, N // 4
    return ref.at[l, pl.ds(pl.multiple_of(half * kh, kh), kh), pl.ds(pl.multiple_of(s * ns, ns), ns)]


def _copy_to(view, sems_s, sems_r, k, j, to):
    return pltpu.make_async_remote_copy(src_ref=view, dst_ref=view, send_sem=sems_s.at[k, j], recv_sem=sems_r.at[k, j],
                                        device_id=to, device_id_type=MESH)


def _gather_start(n, view, sems):
    x, y, c, chips = _place()
    for k in range(n):
        for j, (px, py) in enumerate(chips):
            _copy_to(view(k, 2 * x + y, c), sems[0], sems[1], k, j, (px, py, c)).start()


def _gather_finish(n, view, sems):
    x, y, c, chips = _place()
    sibling = (x, y, 1 - c)
    for k in range(n):
        for j, (px, py) in enumerate(chips):
            theirs = view(k, 2 * px + py, c)
            _copy_to(theirs, sems[0], sems[1], k, j, (px, py, c)).wait_recv()
            _copy_to(theirs, sems[2], sems[3], k, j, sibling).start()
    for k in range(n):
        for j, (px, py) in enumerate(chips):
            _copy_to(view(k, 2 * px + py, 1 - c), sems[2], sems[3], k, j, sibling).wait_recv()
    for k in range(n):
        for j, (px, py) in enumerate(chips):
            _copy_to(view(k, 2 * x + y, c), sems[0], sems[1], k, j, (px, py, c)).wait_send()
            _copy_to(view(k, 2 * px + py, c), sems[2], sems[3], k, j, sibling).wait_send()


def _gather_layer_start(refs, axes, l, sems):
    _gather_start(len(refs), lambda k, s, half: _slab(refs[k], l, axes[k], s, half), sems)


def _gather_layer_finish(refs, axes, l, sems):
    _gather_finish(len(refs), lambda k, s, half: _slab(refs[k], l, axes[k], s, half), sems)


def gather_weights(bufs, axes, first_layer_only):
    n = len(bufs)
    lh = bufs[0].shape[0] // 2

    def body(*refs):
        dst = refs[n:2 * n]

        def view(k, s, half):
            if first_layer_only[k]:
                return _slab(dst[k], 0, axes[k], s, half)
            return _sub(dst[k], axes[k], s, dst[k].shape[axes[k]] // 4, half, lh)

        _gather_start(n, view, refs[2 * n:])
        _gather_finish(n, view, refs[2 * n:])

    return pl.pallas_call(
        body, name="gather_weights", out_shape=[_sds(b.shape, b.dtype) for b in bufs],
        in_specs=[HBM_SPEC] * n, out_specs=[HBM_SPEC] * n, input_output_aliases={k: k for k in range(n)},
        scratch_shapes=[pltpu.SemaphoreType.DMA((n, 3)) for _ in range(4)],
    )(*bufs)


def swap_halves(name, grads):
    n = len(grads)
    lh = grads[0].shape[0] // 2

    def body(*refs):
        src, dst = refs[:n], refs[n:2 * n]
        send, recv = refs[2 * n:]
        x, y, c, _ = _place()
        cps = []
        for k in range(n):
            cp = pltpu.make_async_remote_copy(
                src_ref=_sub(src[k], half=1 - c, lh=lh), dst_ref=dst[k], send_sem=send.at[k], recv_sem=recv.at[k],
                device_id=(x, y, 1 - c), device_id_type=MESH)
            cp.start()
            cps.append(cp)
        for cp in cps:
            cp.wait()

    return pl.pallas_call(
        body, name=name, out_shape=[_sds((lh,) + g.shape[1:], g.dtype) for g in grads],
        in_specs=[HBM_SPEC] * n, out_specs=[HBM_SPEC] * n,
        scratch_shapes=[pltpu.SemaphoreType.DMA((n,)), pltpu.SemaphoreType.DMA((n,))],
    )(*grads)


def exchange_chip_sums(name, sums_bf16, axes):
    n = len(sums_bf16)

    def body(*refs):
        sb, got = refs[:n], refs[n:2 * n]
        send, recv = refs[2 * n:]
        x, y, c, chips = _place()
        cps = []
        for k in range(n):
            ax = axes[k]
            ns = got[k].shape[1 + ax]
            for j, (px, py) in enumerate(chips):
                cp = pltpu.make_async_remote_copy(
                    src_ref=_sub(sb[k], ax, 2 * px + py, ns), dst_ref=got[k].at[j],
                    send_sem=send.at[k, j], recv_sem=recv.at[k, j], device_id=(px, py, c), device_id_type=MESH)
                cp.start()
                cps.append(cp)
        for cp in cps:
            cp.wait()

    out_shape = []
    for a, ax in zip(sums_bf16, axes):
        shp = list(a.shape)
        shp[ax] //= 4
        out_shape.append(_sds([3] + shp, BF16))
    return pl.pallas_call(
        body, name=name, out_shape=out_shape, in_specs=[HBM_SPEC] * n, out_specs=[HBM_SPEC] * n,
        scratch_shapes=[pltpu.SemaphoreType.DMA((n, 3)), pltpu.SemaphoreType.DMA((n, 3))],
    )(*sums_bf16)


def join_halves(name, bufs):
    n = len(bufs)
    lh = bufs[0].shape[0] // 2

    def body(*refs):
        dst = refs[n:2 * n]
        send, recv = refs[2 * n:]
        x, y, c, _ = _place()
        cps = []
        for k in range(n):
            mine = _sub(dst[k], half=c, lh=lh)
            cp = pltpu.make_async_remote_copy(
                src_ref=mine, dst_ref=mine, send_sem=send.at[k], recv_sem=recv.at[k],
                device_id=(x, y, 1 - c), device_id_type=MESH)
            cp.start()
            cps.append(cp)
        for k, cp in enumerate(cps):
            cp.wait_send()
            theirs = _sub(dst[k], half=1 - c, lh=lh)
            pltpu.make_async_remote_copy(
                src_ref=theirs, dst_ref=theirs, send_sem=send.at[k], recv_sem=recv.at[k],
                device_id=(x, y, 1 - c), device_id_type=MESH).wait_recv()

    return pl.pallas_call(
        body, name=name, out_shape=[_sds(b.shape, b.dtype) for b in bufs],
        in_specs=[HBM_SPEC] * n, out_specs=[HBM_SPEC] * n, input_output_aliases={k: k for k in range(n)},
        scratch_shapes=[pltpu.SemaphoreType.DMA((n,)), pltpu.SemaphoreType.DMA((n,))],
    )(*bufs)


def allreduce_small(name, packed):
    R = packed.shape[0]

    def body(x_ref, o_ref, buf, send, recv, loc):
        x, y, c, _ = _place()
        me = 4 * x + 2 * y + c
        cps = [pltpu.make_async_copy(x_ref, buf.at[me], loc)]
        cps[0].start()
        for d in range(1, 8):
            px, py, pc = x ^ (d >> 2), y ^ ((d >> 1) & 1), c ^ (d & 1)
            cp = pltpu.make_async_remote_copy(
                src_ref=x_ref, dst_ref=buf.at[me], send_sem=send.at[d - 1], recv_sem=recv.at[d - 1],
                device_id=(px, py, pc), device_id_type=MESH)
            cp.start()
            cps.append(cp)
        for cp in cps:
            cp.wait()
        acc = buf[0]
        for s in range(1, 8):
            acc = acc + buf[s]
        o_ref[...] = acc

    return pl.pallas_call(
        body, name=name, out_shape=_sds((R, LANES), F32),
        in_specs=[pl.BlockSpec(memory_space=pltpu.VMEM)], out_specs=pl.BlockSpec(memory_space=pltpu.VMEM),
        scratch_shapes=[pltpu.VMEM((8, R, LANES), F32), pltpu.SemaphoreType.DMA((7,)), pltpu.SemaphoreType.DMA((7,)),
                        pltpu.SemaphoreType.DMA],
        compiler_params=pltpu.CompilerParams(vmem_limit_bytes=VMEM_LIMIT),
    )(packed)


def _pack(arrs):
    flat = jnp.concatenate([a.reshape(-1) for a in arrs])
    pad = (-flat.shape[0]) % (8 * LANES)
    return jnp.pad(flat, (0, pad)).reshape(-1, LANES)


def _unpack(packed, like):
    flat = packed.reshape(-1)
    out, pos = [], 0
    for a in like:
        out.append(flat[pos:pos + a.size].reshape(a.shape))
        pos += a.size
    return out


def kernel(x, mem, mix_norm_g, mem_norm_g, w_in, gate_b, conv_w, conv_b, conv_ln_g, conv_ln_b, w_conv_out, rel_bias, w_attn_out, w_mem_kv, w_mem_out, w_o, ffn_norm_g, w_up, ffn_conv_w, ffn_conv_b, w_down, final_norm_g, loss_target, m_mix_norm_g, m_mem_norm_g, m_w_in, m_gate_b, m_conv_w, m_conv_b, m_conv_ln_g, m_conv_ln_b, m_w_conv_out, m_rel_bias, m_w_attn_out, m_w_mem_kv, m_w_mem_out, m_w_o, m_ffn_norm_g, m_w_up, m_ffn_conv_w, m_ffn_conv_b, m_w_down, m_final_norm_g, v_mix_norm_g, v_mem_norm_g, v_w_in, v_gate_b, v_conv_w, v_conv_b, v_conv_ln_g, v_conv_ln_b, v_w_conv_out, v_rel_bias, v_w_attn_out, v_w_mem_kv, v_w_mem_out, v_w_o, v_ffn_norm_g, v_w_up, v_ffn_conv_w, v_ffn_conv_b, v_w_down, v_final_norm_g):
    S, D = x.shape[1], x.shape[2]
    NM = mem.shape[1]
    L = w_in.shape[0]
    C = conv_b.shape[1]
    A = w_attn_out.shape[1]
    Dm = w_mem_out.shape[1]
    Fh = w_down.shape[1] * 4
    D_IN = w_in.shape[2] * 4
    CW = conv_w.shape[1]
    FW = ffn_conv_w.shape[1]
    H = A // ATTN_HEAD_DIM
    assert 2 * C == 2 * A == 2 * Dm == D and D_IN == 6 * D and S % QBLK == 0 and L % 2 == 0
    xi, yi, ci = lax.axis_index("x"), lax.axis_index("y"), lax.axis_index("c")
    chip = 2 * xi + yi
    place_sc = jnp.stack([chip, ci]).astype(jnp.int32)
    zero_sc = jnp.zeros((1,), jnp.int32)
    lh = L // 2

    big = [w_in, w_conv_out, w_attn_out, w_mem_kv, w_mem_out, w_o, w_up, w_down]
    big_m = [m_w_in, m_w_conv_out, m_w_attn_out, m_w_mem_kv, m_w_mem_out, m_w_o, m_w_up, m_w_down]
    big_v = [v_w_in, v_w_conv_out, v_w_attn_out, v_w_mem_kv, v_w_mem_out, v_w_o, v_w_up, v_w_down]
    big_ax = [2, 2, 2, 1, 2, 1, 2, 1]
    placed = []
    for k, (w, ax, dt) in enumerate(zip(big + [conv_w, ffn_conv_w], big_ax + [2, 2], [BF16] * len(big) + [F32, F32])):
        shp = list(w.shape)
        shp[ax] *= 4
        placed.append(_ew3(f"place_w{k}", lambda a: (a,), place_sc, w.shape, [(w, _at())],
                           [(_sds(shp, dt), _at(ax=ax, size=w.shape[ax]))])[0])
    gathered = gather_weights(placed, big_ax + [2, 2], [True] * len(big) + [False, False])
    wnames = ["in", "co", "ao", "kv", "mo", "o", "up", "dn"]
    Wd = dict(zip(wnames, gathered[:len(big)]))
    w_axis = dict(zip(wnames, big_ax))
    conv_w_f, ffn_conv_w_f = gathered[len(big):]
    hosted_by_in = ["up", "dn", "o"]
    hosted_by_up = ["in", "co", "ao", "kv", "mo"]

    r3 = lambda a: a.reshape(L, 1, a.shape[-1])
    mix_g3, mem_g3, ffn_g3, gb3 = r3(mix_norm_g), r3(mem_norm_g), r3(ffn_norm_g), r3(gate_b)
    cb3, lg3, lb3, fb3 = r3(conv_b), r3(conv_ln_g), r3(conv_ln_b), r3(ffn_conv_b)
    cw_p = jnp.pad(conv_w_f, ((0, 0), (0, CONV_HALO - CW), (0, 0)))
    fw_p = jnp.pad(ffn_conv_w_f, ((0, 0), (0, FFN_HALO - FW), (0, 0)))
    rel_p = jnp.pad(rel_bias, ((0, 0), (0, 0), (0, REL_PAD - rel_bias.shape[2])))

    x2, mem2, tgt2 = x[0], mem[0], loss_target[0]
    empty = lambda shape, dt: lax.empty(shape, dt)

    def layer_scalars(l):
        i32 = lambda *v: jnp.stack(v).astype(jnp.int32)
        return i32(l, 2 * l, 2 * l + 1), i32(l, 2 * l), i32(l, 2 * l + 1), i32(l, l)

    saved = dict(
        XN=empty((2 * L, S, D), BF16), HS=empty((2 * L, S, D), F32), PROJ=empty((L, S, D_IN), BF16),
        HC=empty((L, S, C), F32), CACT=empty((L, S, C), BF16), AO=empty((L, S, A), BF16), MO=empty((L, S, Dm), BF16),
        Y=empty((L, 3, S, D), BF16), MERGED=empty((L, S, D), BF16), UP=empty((L, S, 2 * Fh), F32),
        ACT=empty((L, S, Fh), BF16), MN=empty((L, NM, D), BF16), MEMS=empty((L, NM, D), F32),
        KV=empty((L, NM, 2 * Dm), F32), BIAS=empty((L, A // LANES, BANDP, LANES), F32))

    def fwd_layer(l, carry):
        h, sv, Wd = carry
        sv, Wd = dict(sv), dict(Wd)
        W_in, W_co, W_ao, W_kv, W_mo, W_o, W_up, W_dn = [Wd[nm] for nm in wnames]
        sc, sc_a, sc_f, sc_m = layer_scalars(l)
        sv["XN"], sv["HS"] = rms_fwd("rms_mix", sc_a, h, mix_g3, sv["XN"], sv["HS"])
        sv["PROJ"], *got = mm_nn("mm_in", sc_a, sv["XN"], L1, W_in, S, D, D_IN, sv["PROJ"], L0, out_dtype=BF16,
                                 gather=([Wd[nm] for nm in hosted_by_in], [w_axis[nm] for nm in hosted_by_in]))
        Wd.update(zip(hosted_by_in, got))
        W_o, W_dn = Wd["o"], Wd["dn"]
        sv["HC"], sv["CACT"] = conv_fwd("conv_fwd", sc, sv["PROJ"], cw_p, cb3, lg3, lb3, sv["HC"], sv["CACT"], C, CW)
        bias_q = bias_expand("bias_expand", sc, rel_p, H)
        sv["BIAS"] = lax.dynamic_update_slice(sv["BIAS"], bias_to_lanes(bias_q, A // LANES)[None], (l, 0, 0, 0))
        sv["AO"], = attn_fwd("attn_fwd", sc, sv["PROJ"], sv["BIAS"], sv["AO"], A, 2)
        sv["MN"], sv["MEMS"] = rms_fwd("rms_mem", sc_m, mem2, mem_g3, sv["MN"], sv["MEMS"])
        sv["KV"] = mm_nn("mm_kv", sc, sv["MN"], L0, W_kv, NM, D, 2 * Dm, sv["KV"], L0)
        sv["MO"], = mem_fwd("mem_fwd", sc, sv["PROJ"], sv["KV"], sv["MO"], Dm, 5)
        for b, (src, w, nm) in enumerate(((sv["CACT"], W_co, "mm_co"), (sv["AO"], W_ao, "mm_ao"), (sv["MO"], W_mo, "mm_mo"))):
            sv["Y"] = mm_nn(nm, sc, src, L0, w, S, C, D, sv["Y"], lambda s, b=b: (s[0], b), out_dtype=BF16)
        sv["MERGED"], = gate_fwd("gate_fwd", sc, sv["PROJ"], gb3, sv["Y"], sv["MERGED"], D)
        h2 = mm_nn("mm_o", sc, sv["MERGED"], L0, W_o, S, D, D, _sds((S, D), F32), add=h)
        sv["XN"], sv["HS"] = rms_fwd("rms_ffn", sc_f, h2, ffn_g3, sv["XN"], sv["HS"])
        sv["UP"], *got = mm_nn("mm_up", sc_f, sv["XN"], L1, Wd["up"], S, D, 2 * Fh, sv["UP"], L0,
                               gather=([Wd[nm] for nm in hosted_by_up], [w_axis[nm] for nm in hosted_by_up]))
        Wd.update(zip(hosted_by_up, got))
        sv["ACT"], = ffn_fwd("ffn_fwd", sc, sv["UP"], fw_p, fb3, sv["ACT"], Fh, FW)
        h3 = mm_nn("mm_down", sc, sv["ACT"], L0, W_dn, S, Fh, D, _sds((S, D), F32), add=h2)
        return h3, sv, Wd

    h_last, saved, Wd = lax.fori_loop(0, L, fwd_layer, (x2, saved, Wd))
    W_in, W_co, W_ao, W_kv, W_mo, W_o, W_up, W_dn = [Wd[nm] for nm in wnames]
    loss_t, dh, dhb, d_final_g = final_loss("final_loss", zero_sc, h_last, final_norm_g.reshape(1, D), tgt2)
    loss = lax.psum(loss_t[0, 0], ("x", "y", "c"))

    zeros = lambda shape: jnp.zeros(shape, F32)
    grads = dict(
        w_in=empty((L, D, D_IN), F32), w_conv_out=empty((L, C, D), F32), w_attn_out=empty((L, A, D), F32),
        w_mem_kv=empty((L, D, 2 * Dm), F32), w_mem_out=empty((L, Dm, D), F32), w_o=empty((L, D, D), F32),
        w_up=empty((L, D, 2 * Fh), F32), w_down=empty((L, Fh, D), F32),
        mix_g=zeros((L, 1, D)), mem_g=zeros((L, 1, D)), ffn_g=zeros((L, 1, D)), gate_b=zeros((L, 1, 3 * D)),
        conv_w=zeros((L, CONV_HALO, C)), conv_b=zeros((L, 1, C)), ln_g=zeros((L, 1, C)), ln_b=zeros((L, 1, C)),
        rel=zeros((L, H, REL_PAD)), fw_v=zeros((L, FFN_HALO, Fh)), fw_g=zeros((L, FFN_HALO, Fh)),
        fb_v=zeros((L, 1, Fh)), fb_g=zeros((L, 1, Fh)))
    zero_mem = jnp.zeros((NM, D), F32)

    def bwd_layer(it, carry):
        dh, dhb, g = carry
        g = dict(g)
        l = L - 1 - it
        sc, sc_a, sc_f, sc_m = layer_scalars(l)
        sv = saved
        dact = mm_nt("mm_down_dx", sc, [(dhb, None)], W_dn, S, Fh, _sds((S, Fh), F32))
        g["w_down"] = mm_tn("mm_down_dw", sc, sv["ACT"], L0, dhb, None, S, Fh, D, g["w_down"], L0)
        dupv, dupg, g["fw_v"], g["fw_g"], g["fb_v"], g["fb_g"] = ffn_bwd(
            "ffn_bwd", sc, sv["UP"], dact, fw_p, fb3, g["fw_v"], g["fw_g"], g["fb_v"], g["fb_g"], Fh, FW)
        dhn = mm_nt("mm_up_dx", sc, [(dupv, None), (dupg, None)], W_up, S, D, _sds((S, D), F32), tk=2816)
        g["w_up"] = mm_tn("mm_up_dw_v", sc_f, sv["XN"], L1, dupv, None, S, D, Fh, g["w_up"], L0)
        g["w_up"] = mm_tn("mm_up_dw_g", sc_f, sv["XN"], L1, dupg, None, S, D, Fh, g["w_up"], L0, out_joff=Fh)
        dh2, dh2b, g["ffn_g"] = rms_bwd("rms_ffn_bwd", sc_f, sv["HS"], ffn_g3, dhn, dh, g["ffn_g"])
        dmerged = mm_nt("mm_o_dx", sc, [(dh2b, None)], W_o, S, D, _sds((S, D), F32))
        g["w_o"] = mm_tn("mm_o_dw", sc, sv["MERGED"], L0, dh2b, None, S, D, D, g["w_o"], L0)
        dy, dgates, g["gate_b"] = gate_bwd("gate_bwd", sc, sv["PROJ"], gb3, sv["Y"], dmerged, g["gate_b"], D)
        dcact = mm_nt("mm_co_dx", sc, [(dy, lambda s: (0,))], W_co, S, C, _sds((S, C), F32))
        dao = mm_nt("mm_ao_dx", sc, [(dy, lambda s: (1,))], W_ao, S, A, _sds((S, A), BF16), out_dtype=BF16)
        dmo = mm_nt("mm_mo_dx", sc, [(dy, lambda s: (2,))], W_mo, S, Dm, _sds((S, Dm), BF16), out_dtype=BF16)
        g["w_conv_out"] = mm_tn("mm_co_dw", sc, sv["CACT"], L0, dy, lambda s: (0,), S, C, D, g["w_conv_out"], L0)
        g["w_attn_out"] = mm_tn("mm_ao_dw", sc, sv["AO"], L0, dy, lambda s: (1,), S, A, D, g["w_attn_out"], L0)
        g["w_mem_out"] = mm_tn("mm_mo_dw", sc, sv["MO"], L0, dy, lambda s: (2,), S, Dm, D, g["w_mem_out"], L0)
        dhc, g["ln_g"], g["ln_b"], g["conv_b"] = conv_bwd1(
            "conv_bwd1", sc, sv["HC"], dcact, lg3, lb3, g["ln_g"], g["ln_b"], g["conv_b"])
        dconv, g["conv_w"] = conv_bwd2("conv_bwd2", sc, sv["PROJ"], dhc, cw_p, g["conv_w"], C, CW)
        dq, dk, dv, dbias = attn_bwd("attn_bwd", sc, sv["PROJ"], sv["BIAS"], dao, A, 2)
        g["rel"] = bias_reduce("bias_reduce", sc, bias_from_lanes(dbias, A // LANES), g["rel"], H)
        dqm, dkv = mem_bwd("mem_bwd", sc, sv["PROJ"], sv["KV"], dmo, Dm, 5)
        g["w_mem_kv"] = mm_tn("mm_kv_dw", sc, sv["MN"], L0, dkv, None, NM, D, 2 * Dm, g["w_mem_kv"], L0)
        dmn = mm_nt("mm_kv_dx", sc, [(dkv, None)], W_kv, NM, D, _sds((NM, D), F32))
        _, _, g["mem_g"] = rms_bwd("rms_mem_bwd", sc_m, sv["MEMS"], mem_g3, dmn, zero_mem, g["mem_g"])
        pieces = [(dconv, None), (dq, None), (dk, None), (dv, None), (dqm, None), (dgates, None)]
        dxn = mm_nt("mm_in_dx", sc, pieces, W_in, S, D, _sds((S, D), F32), tk=1024)
        off = 0
        for nm, (p, _) in zip(("c", "q", "k", "v", "m", "g"), pieces):
            g["w_in"] = mm_tn("mm_in_dw_" + nm, sc_a, sv["XN"], L1, p, None, S, D, p.shape[1], g["w_in"], L0, out_joff=off)
            off += p.shape[1]
        dh0, dh0b, g["mix_g"] = rms_bwd("rms_mix_bwd", sc_a, sv["HS"], mix_g3, dxn, dh2, g["mix_g"])
        return dh0, dh0b, g

    grad_x2, _, grads = lax.fori_loop(0, L, bwd_layer, (dh, dhb, grads))

    names = ["w_in", "w_conv_out", "w_attn_out", "w_mem_kv", "w_mem_out", "w_o", "w_up", "w_down"]
    gl = [grads[nm] for nm in names]
    from_sib = swap_halves("swap_halves", gl)
    my_layer = lambda g_, s: (s[1] * lh + g_,)
    sums_f, sums_b = [], []
    for k, (g_, r_) in enumerate(zip(gl, from_sib)):
        s_f, s_b = _ew3(f"sum_sib{k}", lambda a, b: (a + b, a + b), place_sc, r_.shape,
                        [(g_, _at(my_layer)), (r_, _at())], [(_sds(r_.shape, F32), _at()), (_sds(r_.shape, BF16), _at())])
        sums_f.append(s_f)
        sums_b.append(s_b)
    got = exchange_chip_sums("exchange_chip_sums", sums_b, big_ax)
    halves = []
    for k, (s_f, r_, w, ax) in enumerate(zip(sums_f, got, big, big_ax)):
        fin, = _ew3(f"sum_chips{k}", lambda a, b0, b1, b2: (((a + b0.astype(F32)) + b1.astype(F32)) + b2.astype(F32),),
                    place_sc, (lh,) + w.shape[1:],
                    [(s_f, _at(ax=ax, size=w.shape[ax]))] + [(r_, _at(lambda g_, s, j=j: (j, g_))) for j in range(3)],
                    [(_sds(w.shape, F32), _at(my_layer))])
        halves.append(fin)
    big_g = join_halves("join_halves", halves)
    big_d, big_nm, big_nv = [], [], []
    for k in range(len(big)):
        d_, m_, v_ = adamw(f"adamw_big{k}", big[k], big_g[k], big_m[k], big_v[k])
        big_d.append(d_), big_nm.append(m_), big_nv.append(v_)

    g_small_full = [
        grads["mix_g"].reshape(L, D), grads["mem_g"].reshape(L, D), grads["gate_b"].reshape(L, 3 * D),
        grads["conv_w"][:, :CW, :], grads["conv_b"].reshape(L, C), grads["ln_g"].reshape(L, C), grads["ln_b"].reshape(L, C),
        grads["rel"][:, :, :rel_bias.shape[2]], grads["ffn_g"].reshape(L, D),
        jnp.concatenate([grads["fw_v"][:, :FW, :], grads["fw_g"][:, :FW, :]], axis=-1),
        jnp.concatenate([grads["fb_v"], grads["fb_g"]], axis=-1).reshape(L, 2 * Fh), d_final_g.reshape(D)]
    summed = _unpack(allreduce_small("allreduce_small", _pack(g_small_full)), g_small_full)
    cws, fws = conv_w.shape[2], ffn_conv_w.shape[2]
    summed[3] = lax.dynamic_slice_in_dim(summed[3], chip * cws, cws, axis=2)
    summed[9] = lax.dynamic_slice_in_dim(summed[9], chip * fws, fws, axis=2)
    small_w = [mix_norm_g, mem_norm_g, gate_b, conv_w, conv_b, conv_ln_g, conv_ln_b, rel_bias, ffn_norm_g, ffn_conv_w, ffn_conv_b, final_norm_g]
    small_m = [m_mix_norm_g, m_mem_norm_g, m_gate_b, m_conv_w, m_conv_b, m_conv_ln_g, m_conv_ln_b, m_rel_bias, m_ffn_norm_g, m_ffn_conv_w, m_ffn_conv_b, m_final_norm_g]
    small_v = [v_mix_norm_g, v_mem_norm_g, v_gate_b, v_conv_w, v_conv_b, v_conv_ln_g, v_conv_ln_b, v_rel_bias, v_ffn_norm_g, v_ffn_conv_w, v_ffn_conv_b, v_final_norm_g]
    sd, sm, sv_ = _ew("adamw_small", _adamw_math, [_pack(small_w), _pack(summed), _pack(small_m), _pack(small_v)], (F32, F32, F32))
    small_d, small_nm, small_nv = _unpack(sd, small_w), _unpack(sm, small_w), _unpack(sv_, small_w)

    order = ["mix_norm_g", "mem_norm_g", "w_in", "gate_b", "conv_w", "conv_b", "conv_ln_g", "conv_ln_b", "w_conv_out",
             "rel_bias", "w_attn_out", "w_mem_kv", "w_mem_out", "w_o", "ffn_norm_g", "w_up", "ffn_conv_w", "ffn_conv_b",
             "w_down", "final_norm_g"]
    small_names = ["mix_norm_g", "mem_norm_g", "gate_b", "conv_w", "conv_b", "conv_ln_g", "conv_ln_b", "rel_bias",
                   "ffn_norm_g", "ffn_conv_w", "ffn_conv_b", "final_norm_g"]

    def collect(bigs, smalls):
        table = dict(zip(names, bigs))
        table.update(zip(small_names, smalls))
        return [table[nm] for nm in order]

    return (loss, grad_x2[None], *collect(big_g, summed), *collect(big_d, small_d), *collect(big_nm, small_nm),
            *collect(big_nv, small_nv))
```

```python
import jax
import jax.numpy as jnp
from jax import lax
from jax.experimental import pallas as pl
from jax.experimental.pallas import tpu as pltpu

F32 = jnp.float32
BF16 = jnp.bfloat16
MESH = pl.DeviceIdType.MESH

CHUNK = 64
LEFT_CHUNKS = 8
BAND = (LEFT_CHUNKS + 1) * CHUNK
BANDP = BAND + CHUNK
MAX_REL = 256
REL_PAD = 640
ATTN_HEAD_DIM = 64
N_MEM_HEADS = 4
QBLK = LEFT_CHUNKS * CHUNK
EPS = 1e-6
NEG_INF = -1e30
LANES = 128
VMEM_LIMIT = 56 * 1024 * 1024

ADAM_LR = 0.001
ADAM_B1 = 0.9
ADAM_B2 = 0.999
ADAM_EPS = 1e-08
ADAM_WD = 0.01
ADAM_STEP = 10


def _pick(dim, target, unit=LANES):
    if dim <= target:
        return dim
    d = (target // unit) * unit
    while d >= unit:
        if dim % d == 0:
            return d
        d -= unit
    raise ValueError(f"no tile for {dim} under {target}")


def _sigmoid(x):
    return 1.0 / (1.0 + jnp.exp(-x))


def _pcall(body, *, name, grid, sc, ins, outs, scratch=(), sem=None):
    arrays = [a for a, _ in ins]
    in_specs = [s for _, s in ins]
    n_in = len(arrays)
    out_shape, out_specs, aliases = [], [], {}
    for k, (o, spec) in enumerate(outs):
        if isinstance(o, jax.ShapeDtypeStruct):
            out_shape.append(o)
        else:
            aliases[1 + len(arrays)] = k
            arrays.append(o)
            in_specs.append(pl.BlockSpec(memory_space=pl.ANY))
            out_shape.append(jax.ShapeDtypeStruct(o.shape, o.dtype))
        out_specs.append(spec)
    n_alias = len(arrays) - n_in

    def wrapped(sc_ref, *refs):
        body(sc_ref, *refs[:n_in], *refs[n_in + n_alias:])

    res = pl.pallas_call(
        wrapped,
        name=name,
        grid_spec=pltpu.PrefetchScalarGridSpec(
            num_scalar_prefetch=1, grid=grid, in_specs=in_specs, out_specs=out_specs, scratch_shapes=list(scratch)),
        out_shape=out_shape,
        input_output_aliases=aliases,
        compiler_params=pltpu.CompilerParams(
            dimension_semantics=sem or ("arbitrary",) * len(grid), vmem_limit_bytes=VMEM_LIMIT),
    )(sc, *arrays)
    return res


def _bs(arr, blk, rc, lead=None):
    nlead = len(arr.shape) - 2

    def imap(*ids):
        sc = ids[-1]
        r, c = rc(*ids)
        return (*(lead(sc) if nlead else ()), r, c)

    return pl.BlockSpec((None,) * nlead + tuple(blk), imap)


def _sds(shape, dtype):
    return jax.ShapeDtypeStruct(tuple(shape), dtype)


L0 = lambda sc: (sc[0],)
L1 = lambda sc: (sc[1],)


def _mm_core(name, grid, sc, a_items, b_item, add_item, out_item, dims, ranges, out_dtype, acc_shape, gather=None):
    nk = grid[2]
    na = len(a_items)
    has_add = add_item is not None
    ng = len(gather[0]) if gather else 0

    def body(sc_ref, *refs):
        a_refs = refs[:na]
        b_ref = refs[na]
        pos = na + 1
        add_ref = refs[pos] if has_add else None
        pos += int(has_add)
        o_ref = refs[pos]
        g_refs = refs[pos + 1:pos + 1 + ng]
        pos += 1 + ng
        acc = refs[pos] if nk > 1 else None
        k = pl.program_id(2)
        if ng:
            sems = refs[pos + int(nk > 1):]
            at = [pl.program_id(d) for d in range(3)]
            first = (at[0] == 0) & (at[1] == 0) & (at[2] == 0)
            last = (at[0] == grid[0] - 1) & (at[1] == grid[1] - 1) & (at[2] == grid[2] - 1)
            nxt = sc_ref[0] + 1
            more = nxt < g_refs[0].shape[0]

            @pl.when(first & more)
            def _():
                _gather_layer_start(g_refs, gather[1], nxt, sems)

        def contrib(ar):
            return lax.dot_general(ar[...].astype(BF16), b_ref[...].astype(BF16), dims, preferred_element_type=F32)

        def fin(val):
            if has_add:
                val = val + add_ref[...]
            o_ref[...] = val.astype(out_dtype)

        if nk == 1:
            fin(contrib(a_refs[0]))
        for p, (k0, k1) in enumerate(ranges if nk > 1 else []):
            @pl.when((k >= k0) & (k < k1))
            def _(p=p):
                part = contrib(a_refs[p])

                @pl.when(k == 0)
                def _():
                    acc[...] = part

                @pl.when((k > 0) & (k < nk - 1))
                def _():
                    acc[...] += part

                @pl.when(k == nk - 1)
                def _():
                    fin(acc[...] + part)

        if ng:
            @pl.when(last & more)
            def _():
                _gather_layer_finish(g_refs, gather[1], nxt, sems)

    ins = list(a_items) + [b_item] + ([add_item] if has_add else [])
    scratch = [pltpu.VMEM(acc_shape, F32)] if nk > 1 else []
    outs = [out_item]
    if ng:
        outs += [(b, HBM_SPEC) for b in gather[0]]
        scratch += [pltpu.SemaphoreType.DMA((ng, 3)) for _ in range(4)]
    res = _pcall(body, name=name, grid=grid, sc=sc, ins=ins, outs=outs, scratch=scratch,
                 sem=("arbitrary",) * 3 if ng else ("parallel", "parallel", "arbitrary"))
    return res if ng else res[0]


def mm_nn(name, sc, a, a_lead, w, M, K, N, out, out_lead=None, out_joff=0, add=None, out_dtype=F32,
          tm=1024, tn=1408, tk=2048, gather=None):
    tm, tn, tk = _pick(M, tm, 8), _pick(N, tn), _pick(K, tk)
    nk = K // tk
    a_item = (a, _bs(a, (tm, tk), lambda i, j, k, sc: (i, k), a_lead))
    b_item = (w, _bs(w, (tk, tn), lambda i, j, k, sc: (k, j), L0))
    add_item = None if add is None else (add, _bs(add, (tm, tn), lambda i, j, k, sc: (i, j)))
    out_spec = _bs(out, (tm, tn), lambda i, j, k, sc: (i, j + out_joff // tn), out_lead)
    return _mm_core(name, (M // tm, N // tn, nk), sc, [a_item], b_item, add_item, (out, out_spec),
                    (((1,), (0,)), ((), ())), [(0, nk)], out_dtype, (tm, tn), gather=gather)


def mm_nt(name, sc, pieces, w, M, N, out, w_koff=0, out_dtype=F32, tm=1024, tn=1408, tk=2048):
    tm, tn = _pick(M, tm, 8), _pick(N, tn)
    widths = [a.shape[-1] for a, _ in pieces]
    tk = _pick(widths[0], tk)
    offs = [w_koff]
    for wd in widths:
        offs.append(offs[-1] + wd)
    while any(x % tk for x in offs):
        tk = _pick(widths[0], tk - LANES)
    ranges, a_items, k0 = [], [], 0
    for (a, lead), wd in zip(pieces, widths):
        n = wd // tk
        ranges.append((k0, k0 + n))
        a_items.append((a, _bs(a, (tm, tk), lambda i, j, k, sc, k0=k0, n=n: (i, jnp.clip(k - k0, 0, n - 1)), lead)))
        k0 += n
    nk = k0
    b_item = (w, _bs(w, (tn, tk), lambda i, j, k, sc: (j, k + w_koff // tk), L0))
    out_spec = _bs(out, (tm, tn), lambda i, j, k, sc: (i, j))
    return _mm_core(name, (M // tm, N // tn, nk), sc, a_items, b_item, None, (out, out_spec),
                    (((1,), (1,)), ((), ())), ranges, out_dtype, (tm, tn))


def mm_tn(name, sc, a, a_lead, b, b_lead, S, K, N, out, out_lead, out_joff=0, tm=1408, tn=1408, tk=2048):
    tm, tn, tk = _pick(K, tm), _pick(N, tn), _pick(S, tk, 8)
    while out_joff % tn:
        tn = _pick(N, tn - LANES)
    nk = S // tk
    a_item = (a, _bs(a, (tk, tm), lambda i, j, k, sc: (k, i), a_lead))
    b_item = (b, _bs(b, (tk, tn), lambda i, j, k, sc: (k, j), b_lead))
    out_spec = _bs(out, (tm, tn), lambda i, j, k, sc: (i, j + out_joff // tn), out_lead)
    return _mm_core(name, (K // tm, N // tn, nk), sc, [a_item], b_item, None, (out, out_spec),
                    (((0,), (0,)), ((), ())), [(0, nk)], F32, (tm, tn))


def rms_fwd(name, sc, h, g3, xn_buf, hs_buf):
    S, D = h.shape
    tr = _pick(S, 256, 8)

    def body(sc_ref, h_ref, g_ref, xn_ref, hs_ref):
        x = h_ref[...]
        r = lax.rsqrt(jnp.mean(x * x, axis=-1, keepdims=True) + EPS)
        xn_ref[...] = (x * r * g_ref[...]).astype(BF16)
        hs_ref[...] = x

    return _pcall(body, name=name, grid=(S // tr,), sc=sc,
                  ins=[(h, _bs(h, (tr, D), lambda i, sc: (i, 0))), (g3, _bs(g3, (1, D), lambda i, sc: (0, 0), L0))],
                  outs=[(xn_buf, _bs(xn_buf, (tr, D), lambda i, sc: (i, 0), L1)),
                        (hs_buf, _bs(hs_buf, (tr, D), lambda i, sc: (i, 0), L1))])


def rms_bwd(name, sc, hs_buf, g3, dy, dres, dg_buf):
    S, D = dy.shape
    tr = _pick(S, 256, 8)

    def body(sc_ref, x_ref, g_ref, dy_ref, dres_ref, dx_ref, dxb_ref, dg_ref):
        i = pl.program_id(0)
        x = x_ref[...]
        r = lax.rsqrt(jnp.mean(x * x, axis=-1, keepdims=True) + EPS)
        xh = x * r
        dyv = dy_ref[...]
        dxh = dyv * g_ref[...]
        dx = r * (dxh - xh * jnp.mean(dxh * xh, axis=-1, keepdims=True)) + dres_ref[...]
        dx_ref[...] = dx
        dxb_ref[...] = dx.astype(BF16)
        part = jnp.sum(dyv * xh, axis=0, keepdims=True)

        @pl.when(i == 0)
        def _():
            dg_ref[...] = part

        @pl.when(i > 0)
        def _():
            dg_ref[...] += part

    return _pcall(body, name=name, grid=(S // tr,), sc=sc,
                  ins=[(hs_buf, _bs(hs_buf, (tr, D), lambda i, sc: (i, 0), L1)),
                       (g3, _bs(g3, (1, D), lambda i, sc: (0, 0), L0)),
                       (dy, _bs(dy, (tr, D), lambda i, sc: (i, 0))),
                       (dres, _bs(dres, (tr, D), lambda i, sc: (i, 0)))],
                  outs=[(_sds((S, D), F32), pl.BlockSpec((tr, D), lambda i, sc: (i, 0))),
                        (_sds((S, D), BF16), pl.BlockSpec((tr, D), lambda i, sc: (i, 0))),
                        (dg_buf, _bs(dg_buf, (1, D), lambda i, sc: (0, 0), L0))])


CONV_HALO = 32
CONV_RB = 32


def _rotations(buf, rots, n):
    for b in range(1, 8):
        rots[b - 1, 0:n, :] = buf[b:b + n, :]


def _shifted(buf, rots, s, r, n):
    b = s % 8
    if b == 0:
        return buf[r + s:r + s + n, :]
    return rots[b - 1, r + s - b:r + s - b + n, :]


def conv_fwd(name, sc, proj, cw, cb, lg, lb, hc_buf, cact_buf, C, W):
    S = proj.shape[1]
    T = _pick(S, 256, CONV_HALO)
    nh = T // CONV_HALO
    off = CONV_HALO - (W - 1)

    def body(sc_ref, ua_ref, ug_ref, pa_ref, pg_ref, w_ref, b_ref, lg_ref, lb_ref, hc_ref, ca_ref, buf, rots):
        i = pl.program_id(0)
        halo = pa_ref[...].astype(F32) * _sigmoid(pg_ref[...].astype(F32))
        buf[0:CONV_HALO, :] = jnp.where(i > 0, halo, 0.0)
        buf[CONV_HALO:CONV_HALO + T, :] = ua_ref[...].astype(F32) * _sigmoid(ug_ref[...].astype(F32))
        _rotations(buf, rots, T + CONV_HALO - 8)
        for r in range(0, T, CONV_RB):
            acc = jnp.zeros((CONV_RB, C), F32) + b_ref[...]
            for k in range(W):
                acc = acc + w_ref[k:k + 1, :] * _shifted(buf, rots, off + k, r, CONV_RB)
            hc_ref[r:r + CONV_RB, :] = acc
        hc = hc_ref[...]
        mu = jnp.mean(hc, axis=-1, keepdims=True)
        d = hc - mu
        rstd = lax.rsqrt(jnp.mean(d * d, axis=-1, keepdims=True) + EPS)
        y = d * rstd * lg_ref[...] + lb_ref[...]
        ca_ref[...] = (y * _sigmoid(y)).astype(BF16)

    vec = lambda a: (a, _bs(a, (1, C), lambda i, sc: (0, 0), L0))
    return _pcall(body, name=name, grid=(S // T,), sc=sc,
                  ins=[(proj, _bs(proj, (T, C), lambda i, sc: (i, 0), L0)),
                       (proj, _bs(proj, (T, C), lambda i, sc: (i, 1), L0)),
                       (proj, _bs(proj, (CONV_HALO, C), lambda i, sc: (jnp.maximum(i * nh - 1, 0), 0), L0)),
                       (proj, _bs(proj, (CONV_HALO, C), lambda i, sc: (jnp.maximum(i * nh - 1, 0), 1), L0)),
                       (cw, _bs(cw, (CONV_HALO, C), lambda i, sc: (0, 0), L0)), vec(cb), vec(lg), vec(lb)],
                  outs=[(hc_buf, _bs(hc_buf, (T, C), lambda i, sc: (i, 0), L0)),
                        (cact_buf, _bs(cact_buf, (T, C), lambda i, sc: (i, 0), L0))],
                  scratch=[pltpu.VMEM((T + CONV_HALO, C), F32), pltpu.VMEM((7, T + CONV_HALO, C), F32)])


def conv_bwd1(name, sc, hc_buf, dcact, lg, lb, dlg_buf, dlb_buf, dcb_buf):
    S, C = dcact.shape
    tr = _pick(S, 256, 8)

    def body(sc_ref, hc_ref, dc_ref, lg_ref, lb_ref, dhc_ref, dlg_ref, dlb_ref, dcb_ref):
        i = pl.program_id(0)
        hc = hc_ref[...]
        mu = jnp.mean(hc, axis=-1, keepdims=True)
        d = hc - mu
        rstd = lax.rsqrt(jnp.mean(d * d, axis=-1, keepdims=True) + EPS)
        yh = d * rstd
        y = yh * lg_ref[...] + lb_ref[...]
        sg = _sigmoid(y)
        dy = dc_ref[...] * (sg * (1.0 + y * (1.0 - sg)))
        dyh = dy * lg_ref[...]
        dhc = rstd * (dyh - jnp.mean(dyh, axis=-1, keepdims=True) - yh * jnp.mean(dyh * yh, axis=-1, keepdims=True))
        dhc_ref[...] = dhc
        parts = (jnp.sum(dy * yh, axis=0, keepdims=True), jnp.sum(dy, axis=0, keepdims=True),
                 jnp.sum(dhc, axis=0, keepdims=True))

        @pl.when(i == 0)
        def _():
            dlg_ref[...], dlb_ref[...], dcb_ref[...] = parts

        @pl.when(i > 0)
        def _():
            dlg_ref[...] += parts[0]
            dlb_ref[...] += parts[1]
            dcb_ref[...] += parts[2]

    vec = lambda a: (a, _bs(a, (1, C), lambda i, sc: (0, 0), L0))
    return _pcall(body, name=name, grid=(S // tr,), sc=sc,
                  ins=[(hc_buf, _bs(hc_buf, (tr, C), lambda i, sc: (i, 0), L0)),
                       (dcact, _bs(dcact, (tr, C), lambda i, sc: (i, 0))), vec(lg), vec(lb)],
                  outs=[(_sds((S, C), F32), pl.BlockSpec((tr, C), lambda i, sc: (i, 0))),
                        vec(dlg_buf), vec(dlb_buf), vec(dcb_buf)])


def conv_bwd2(name, sc, proj, dhc, cw, dcw_buf, C, W):
    S = proj.shape[1]
    T = _pick(S, 256, CONV_HALO)
    nh = T // CONV_HALO
    nt = S // T
    off = CONV_HALO - (W - 1)

    def body(sc_ref, ua_ref, ug_ref, pa_ref, pg_ref, d_ref, dn_ref, w_ref, o_ref, dw_ref, buf, dbuf, dhg, rots, drots):
        i = pl.program_id(0)
        halo = pa_ref[...].astype(F32) * _sigmoid(pg_ref[...].astype(F32))
        buf[0:CONV_HALO, :] = jnp.where(i > 0, halo, 0.0)
        sg = _sigmoid(ug_ref[...].astype(F32))
        ua = ua_ref[...].astype(F32)
        buf[CONV_HALO:CONV_HALO + T, :] = ua * sg
        dbuf[0:T, :] = d_ref[...]
        dbuf[T:T + CONV_HALO, :] = jnp.where(i < nt - 1, dn_ref[...], 0.0)
        _rotations(buf, rots, T + CONV_HALO - 8)
        _rotations(dbuf, drots, T + CONV_HALO - 8)

        @pl.when(i == 0)
        def _():
            dw_ref[...] = jnp.zeros_like(dw_ref)

        for r in range(0, T, CONV_RB):
            acc = jnp.zeros((CONV_RB, C), F32)
            for k in range(W):
                acc = acc + w_ref[k:k + 1, :] * _shifted(dbuf, drots, (W - 1) - k, r, CONV_RB)
            dhg[r:r + CONV_RB, :] = acc
        for k in range(W):
            dw_ref[k:k + 1, :] += jnp.sum(dbuf[0:T, :] * _shifted(buf, rots, off + k, 0, T), axis=0, keepdims=True)
        dh = dhg[...]
        o_ref[:, 0:C] = (dh * sg).astype(BF16)
        o_ref[:, C:2 * C] = (dh * ua * sg * (1.0 - sg)).astype(BF16)

    return _pcall(body, name=name, grid=(nt,), sc=sc,
                  ins=[(proj, _bs(proj, (T, C), lambda i, sc: (i, 0), L0)),
                       (proj, _bs(proj, (T, C), lambda i, sc: (i, 1), L0)),
                       (proj, _bs(proj, (CONV_HALO, C), lambda i, sc: (jnp.maximum(i * nh - 1, 0), 0), L0)),
                       (proj, _bs(proj, (CONV_HALO, C), lambda i, sc: (jnp.maximum(i * nh - 1, 0), 1), L0)),
                       (dhc, _bs(dhc, (T, C), lambda i, sc: (i, 0))),
                       (dhc, _bs(dhc, (CONV_HALO, C), lambda i, sc: (jnp.minimum((i + 1) * nh, S // CONV_HALO - 1), 0))),
                       (cw, _bs(cw, (CONV_HALO, C), lambda i, sc: (0, 0), L0))],
                  outs=[(_sds((S, 2 * C), BF16), pl.BlockSpec((T, 2 * C), lambda i, sc: (i, 0))),
                        (dcw_buf, _bs(dcw_buf, (CONV_HALO, C), lambda i, sc: (0, 0), L0))],
                  scratch=[pltpu.VMEM((T + CONV_HALO, C), F32), pltpu.VMEM((T + CONV_HALO, C), F32),
                           pltpu.VMEM((T, C), F32), pltpu.VMEM((7, T + CONV_HALO, C), F32),
                           pltpu.VMEM((7, T + CONV_HALO, C), F32)])


def gate_fwd(name, sc, proj, gb3, y_buf, merged_buf, D):
    S = proj.shape[1]
    tr, tc = _pick(S, 256, 8), _pick(D, 1024)
    nc = D // tc

    def body(sc_ref, g0, g1, g2, b0, b1, b2, y0, y1, y2, o_ref):
        acc = None
        for g, b, y in ((g0, b0, y0), (g1, b1, y1), (g2, b2, y2)):
            t = _sigmoid(g[...].astype(F32) + b[...]) * y[...].astype(F32)
            acc = t if acc is None else acc + t
        o_ref[...] = acc.astype(BF16)

    ins = [(proj, _bs(proj, (tr, tc), lambda i, j, sc, b=b: (i, (3 + b) * nc + j), L0)) for b in range(3)]
    ins += [(gb3, _bs(gb3, (1, tc), lambda i, j, sc, b=b: (0, b * nc + j), L0)) for b in range(3)]
    ins += [(y_buf, _bs(y_buf, (tr, tc), lambda i, j, sc: (i, j), lambda sc, b=b: (sc[0], b))) for b in range(3)]
    return _pcall(body, name=name, grid=(S // tr, nc), sc=sc, ins=ins,
                  outs=[(merged_buf, _bs(merged_buf, (tr, tc), lambda i, j, sc: (i, j), L0))])


def gate_bwd(name, sc, proj, gb3, y_buf, dmerged, dgb_buf, D):
    S = proj.shape[1]
    tr, tc = _pick(S, 256, 8), _pick(D, 1024)
    nc = D // tc

    def body(sc_ref, g_ref, b_ref, y_ref, dm_ref, dy_ref, dg_ref, dgb_ref):
        i = pl.program_id(2)
        g = _sigmoid(g_ref[...].astype(F32) + b_ref[...])
        dm = dm_ref[...]
        dy_ref[...] = (dm * g).astype(BF16)
        dgt = dm * y_ref[...].astype(F32) * g * (1.0 - g)
        dg_ref[...] = dgt.astype(BF16)
        part = jnp.sum(dgt, axis=0, keepdims=True)

        @pl.when(i == 0)
        def _():
            dgb_ref[...] = part

        @pl.when(i > 0)
        def _():
            dgb_ref[...] += part

    dy_sds = _sds((3, S, D), BF16)
    return _pcall(body, name=name, grid=(3, nc, S // tr), sc=sc,
                  ins=[(proj, _bs(proj, (tr, tc), lambda b, j, i, sc: (i, (3 + b) * nc + j), L0)),
                       (gb3, _bs(gb3, (1, tc), lambda b, j, i, sc: (0, b * nc + j), L0)),
                       (y_buf, pl.BlockSpec((None, None, tr, tc), lambda b, j, i, sc: (sc[0], b, i, j))),
                       (dmerged, _bs(dmerged, (tr, tc), lambda b, j, i, sc: (i, j)))],
                  outs=[(dy_sds, pl.BlockSpec((None, tr, tc), lambda b, j, i, sc: (b, i, j))),
                        (_sds((S, 3 * D), BF16), pl.BlockSpec((tr, tc), lambda b, j, i, sc: (i, b * nc + j))),
                        (dgb_buf, _bs(dgb_buf, (1, tc), lambda b, j, i, sc: (0, b * nc + j), L0))])


FFN_HALO = 8
FFN_RB = 16


def _taps(buf, w_ref, init, start, n, W, reverse=False):
    acc = init
    for k in range(W):
        s = start + ((W - 1 - k) if reverse else k)
        acc = acc + w_ref[k:k + 1, :] * buf[s:s + n, :]
    return acc


def ffn_fwd(name, sc, up, fw, fb3, act_buf, Fh, W):
    S = up.shape[1]
    T, tc = _pick(S, 512, FFN_RB), _pick(Fh, 512)
    nf = Fh // tc
    nh = T // FFN_HALO
    off = FFN_HALO - (W - 1)

    def body(sc_ref, v_ref, g_ref, pv_ref, pg_ref, wv_ref, wg_ref, bv_ref, bg_ref, o_ref, bufv, bufg):
        i = pl.program_id(1)
        for m_ref, p_ref, buf in ((v_ref, pv_ref, bufv), (g_ref, pg_ref, bufg)):
            buf[0:FFN_HALO, :] = jnp.where(i > 0, p_ref[...], 0.0)
            buf[FFN_HALO:FFN_HALO + T, :] = m_ref[...]
        for r in range(0, T, FFN_RB):
            val = _taps(bufv, wv_ref, bv_ref[...], r + off, FFN_RB, W)
            gt = _taps(bufg, wg_ref, bg_ref[...], r + off, FFN_RB, W)
            o_ref[r:r + FFN_RB, :] = (gt * _sigmoid(gt) * val).astype(BF16)

    prev = lambda i: jnp.maximum(i * nh - 1, 0)
    return _pcall(body, name=name, grid=(nf, S // T), sc=sc,
                  ins=[(up, _bs(up, (T, tc), lambda j, i, sc: (i, j), L0)),
                       (up, _bs(up, (T, tc), lambda j, i, sc: (i, j + nf), L0)),
                       (up, _bs(up, (FFN_HALO, tc), lambda j, i, sc: (prev(i), j), L0)),
                       (up, _bs(up, (FFN_HALO, tc), lambda j, i, sc: (prev(i), j + nf), L0)),
                       (fw, _bs(fw, (FFN_HALO, tc), lambda j, i, sc: (0, j), L0)),
                       (fw, _bs(fw, (FFN_HALO, tc), lambda j, i, sc: (0, j + nf), L0)),
                       (fb3, _bs(fb3, (1, tc), lambda j, i, sc: (0, j), L0)),
                       (fb3, _bs(fb3, (1, tc), lambda j, i, sc: (0, j + nf), L0))],
                  outs=[(act_buf, _bs(act_buf, (T, tc), lambda j, i, sc: (i, j), L0))],
                  scratch=[pltpu.VMEM((T + FFN_HALO, tc), F32), pltpu.VMEM((T + FFN_HALO, tc), F32)])


def ffn_bwd(name, sc, up, dact, fw, fb3, dfw_v_buf, dfw_g_buf, dfb_v_buf, dfb_g_buf, Fh, W):
    S = up.shape[1]
    T, tc = _pick(S, 512, FFN_RB), _pick(Fh, 512)
    nf = Fh // tc
    nh = T // FFN_HALO
    nt = S // T
    off = FFN_HALO - (W - 1)
    TE = T + FFN_HALO

    def body(sc_ref, v_ref, g_ref, pv_ref, pg_ref, nv_ref, ng_ref, d_ref, dn_ref, wv_ref, wg_ref, bv_ref, bg_ref,
             ov_ref, og_ref, dwv_ref, dwg_ref, dbv_ref, dbg_ref, bufv, bufg, dv, dg):
        i = pl.program_id(1)
        for m_ref, p_ref, n_ref, buf in ((v_ref, pv_ref, nv_ref, bufv), (g_ref, pg_ref, ng_ref, bufg)):
            buf[0:FFN_HALO, :] = jnp.where(i > 0, p_ref[...], 0.0)
            buf[FFN_HALO:FFN_HALO + T, :] = m_ref[...]
            buf[FFN_HALO + T:FFN_HALO + TE, :] = n_ref[...]

        def d_up(r, n, d):
            val = _taps(bufv, wv_ref, bv_ref[...], r + off, n, W)
            gt = _taps(bufg, wg_ref, bg_ref[...], r + off, n, W)
            sg = _sigmoid(gt)
            dv[r:r + n, :] = d * gt * sg
            dg[r:r + n, :] = d * val * (sg * (1.0 + gt * (1.0 - sg)))

        for r in range(0, T, FFN_RB):
            d_up(r, FFN_RB, d_ref[r:r + FFN_RB, :])
        d_up(T, FFN_HALO, jnp.where(i < nt - 1, dn_ref[...], 0.0))

        @pl.when(i == 0)
        def _():
            for r in (dwv_ref, dwg_ref, dbv_ref, dbg_ref):
                r[...] = jnp.zeros_like(r)

        fold = lambda a: a[0:8, :] + a[8:16, :]
        for dsrc, w_ref, buf, o_ref, dw_ref, db_ref in ((dv, wv_ref, bufv, ov_ref, dwv_ref, dbv_ref),
                                                      (dg, wg_ref, bufg, og_ref, dwg_ref, dbg_ref)):
            acc_w = [jnp.zeros((8, tc), F32) for _ in range(W)]
            acc_b = jnp.zeros((8, tc), F32)
            for r in range(0, T, FFN_RB):
                o_ref[r:r + FFN_RB, :] = _taps(dsrc, w_ref, 0.0, r, FFN_RB, W, reverse=True).astype(BF16)
                dm = dsrc[r:r + FFN_RB, :]
                for k in range(W):
                    acc_w[k] = acc_w[k] + fold(dm * buf[r + off + k:r + off + k + FFN_RB, :])
                acc_b = acc_b + fold(dm)
            for k in range(W):
                dw_ref[k:k + 1, :] += jnp.sum(acc_w[k], axis=0, keepdims=True)
            db_ref[...] += jnp.sum(acc_b, axis=0, keepdims=True)

    prev = lambda i: jnp.maximum(i * nh - 1, 0)
    nxt = lambda i: jnp.minimum((i + 1) * nh, S // FFN_HALO - 1)
    o_sds = _sds((S, Fh), BF16)
    return _pcall(body, name=name, grid=(nf, nt), sc=sc,
                  ins=[(up, _bs(up, (T, tc), lambda j, i, sc: (i, j), L0)),
                       (up, _bs(up, (T, tc), lambda j, i, sc: (i, j + nf), L0)),
                       (up, _bs(up, (FFN_HALO, tc), lambda j, i, sc: (prev(i), j), L0)),
                       (up, _bs(up, (FFN_HALO, tc), lambda j, i, sc: (prev(i), j + nf), L0)),
                       (up, _bs(up, (FFN_HALO, tc), lambda j, i, sc: (nxt(i), j), L0)),
                       (up, _bs(up, (FFN_HALO, tc), lambda j, i, sc: (nxt(i), j + nf), L0)),
                       (dact, _bs(dact, (T, tc), lambda j, i, sc: (i, j))),
                       (dact, _bs(dact, (FFN_HALO, tc), lambda j, i, sc: (nxt(i), j))),
                       (fw, _bs(fw, (FFN_HALO, tc), lambda j, i, sc: (0, j), L0)),
                       (fw, _bs(fw, (FFN_HALO, tc), lambda j, i, sc: (0, j + nf), L0)),
                       (fb3, _bs(fb3, (1, tc), lambda j, i, sc: (0, j), L0)),
                       (fb3, _bs(fb3, (1, tc), lambda j, i, sc: (0, j + nf), L0))],
                  outs=[(o_sds, pl.BlockSpec((T, tc), lambda j, i, sc: (i, j))),
                        (o_sds, pl.BlockSpec((T, tc), lambda j, i, sc: (i, j))),
                        (dfw_v_buf, _bs(dfw_v_buf, (FFN_HALO, tc), lambda j, i, sc: (0, j), L0)),
                        (dfw_g_buf, _bs(dfw_g_buf, (FFN_HALO, tc), lambda j, i, sc: (0, j), L0)),
                        (dfb_v_buf, _bs(dfb_v_buf, (1, tc), lambda j, i, sc: (0, j), L0)),
                        (dfb_g_buf, _bs(dfb_g_buf, (1, tc), lambda j, i, sc: (0, j), L0))],
                  scratch=[pltpu.VMEM((T + 2 * FFN_HALO, tc), F32), pltpu.VMEM((T + 2 * FFN_HALO, tc), F32),
                           pltpu.VMEM((T + 2 * FFN_HALO, tc), F32), pltpu.VMEM((T + 2 * FFN_HALO, tc), F32)],
                  sem=("parallel", "arbitrary"))


def final_loss(name, sc, h, g2, target):
    S, D = h.shape
    tr = _pick(S, 256, 8)

    def body(sc_ref, h_ref, g_ref, t_ref, loss_ref, dx_ref, dxb_ref, dg_ref):
        i = pl.program_id(0)
        x = h_ref[...]
        r = lax.rsqrt(jnp.mean(x * x, axis=-1, keepdims=True) + EPS)
        xh = x * r
        err = xh * g_ref[...] - t_ref[...]
        part_loss = 0.5 * jnp.sum(jnp.mean(err * err, axis=-1, keepdims=True), axis=0, keepdims=True)
        dy = err * (1.0 / D)
        dxh = dy * g_ref[...]
        dx = r * (dxh - xh * jnp.mean(dxh * xh, axis=-1, keepdims=True))
        dx_ref[...] = dx
        dxb_ref[...] = dx.astype(BF16)
        part_g = jnp.sum(dy * xh, axis=0, keepdims=True)

        @pl.when(i == 0)
        def _():
            loss_ref[...] = jnp.zeros_like(loss_ref) + part_loss
            dg_ref[...] = part_g

        @pl.when(i > 0)
        def _():
            loss_ref[...] += part_loss
            dg_ref[...] += part_g

    row = lambda a: (a, pl.BlockSpec((tr, D), lambda i, sc: (i, 0)))
    return _pcall(body, name=name, grid=(S // tr,), sc=sc,
                  ins=[row(h), (g2, pl.BlockSpec((1, D), lambda i, sc: (0, 0))), row(target)],
                  outs=[(_sds((8, LANES), F32), pl.BlockSpec((8, LANES), lambda i, sc: (0, 0))),
                        row(_sds((S, D), F32)), row(_sds((S, D), BF16)),
                        (_sds((1, D), F32), pl.BlockSpec((1, D), lambda i, sc: (0, 0)))])


def _split3(a):
    hi = a.astype(BF16)
    r1 = a - hi.astype(F32)
    mid = r1.astype(BF16)
    lo = (r1 - mid.astype(F32)).astype(BF16)
    return hi, mid, lo


def _rel_onehot(qi, rows_are_keys):
    shape = (BANDP, REL_PAD) if rows_are_keys else (REL_PAD, BANDP)
    km = lax.broadcasted_iota(jnp.int32, shape, 0 if rows_are_keys else 1)
    idx = lax.broadcasted_iota(jnp.int32, shape, 1 if rows_are_keys else 0)
    rel = jnp.clip(LEFT_CHUNKS * CHUNK + qi - km, -MAX_REL, MAX_REL) + MAX_REL
    return jnp.where(rel == idx, 1.0, 0.0).astype(BF16)


def bias_expand(name, sc, rel3, H):
    def body(sc_ref, rel_ref, o_ref):
        qi = pl.program_id(0)
        oh = _rel_onehot(qi, False)
        acc = jnp.zeros((H, BANDP), F32)
        for part in _split3(rel_ref[...]):
            acc = acc + jnp.dot(part, oh, preferred_element_type=F32)
        o_ref[...] = acc

    return _pcall(body, name=name, grid=(CHUNK,), sc=sc,
                  ins=[(rel3, _bs(rel3, (H, REL_PAD), lambda q, sc: (0, 0), L0))],
                  outs=[(_sds((CHUNK, H, BANDP), F32), pl.BlockSpec((None, H, BANDP), lambda q, sc: (q, 0, 0)))])[0]


def bias_reduce(name, sc, dbias_q, drel_buf, H):
    def body(sc_ref, d_ref, o_ref):
        qi = pl.program_id(0)
        oh = _rel_onehot(qi, True)
        acc = jnp.zeros((H, REL_PAD), F32)
        for part in _split3(d_ref[...]):
            acc = acc + jnp.dot(part, oh, preferred_element_type=F32)

        @pl.when(qi == 0)
        def _():
            o_ref[...] = acc

        @pl.when(qi > 0)
        def _():
            o_ref[...] += acc

    return _pcall(body, name=name, grid=(CHUNK,), sc=sc,
                  ins=[(dbias_q, pl.BlockSpec((None, H, BANDP), lambda q, sc: (q, 0, 0)))],
                  outs=[(drel_buf, _bs(drel_buf, (H, REL_PAD), lambda q, sc: (0, 0), L0))])[0]


def _head_blocks(a2):
    lo = lax.broadcasted_iota(jnp.int32, a2.shape, 1) < ATTN_HEAD_DIM
    zero = jnp.zeros_like(a2)
    return jnp.concatenate([jnp.where(lo, a2, zero), jnp.where(lo, zero, a2)], axis=0)


def _head_diag(r):
    lo = lax.broadcasted_iota(jnp.int32, (CHUNK, LANES), 1) < ATTN_HEAD_DIM
    return jnp.where(lo, r[0:CHUNK], r[CHUNK:2 * CHUNK])


def _attn_probs_t(q2, k2, bias_t, c, first, scale):
    qbd = _head_blocks(q2)
    s = lax.dot_general(k2, qbd, (((1,), (1,)), ((), ())), preferred_element_type=F32) * scale + bias_t
    km = lax.broadcasted_iota(jnp.int32, s.shape, 0)
    valid = (km < BAND) & (jnp.logical_not(first) | (km + c * CHUNK >= QBLK))
    s = jnp.where(valid, s, NEG_INF)
    p = jnp.exp(s - jnp.max(s, axis=0, keepdims=True))
    return qbd, p * (1.0 / jnp.sum(p, axis=0, keepdims=True))


def bias_to_lanes(bias_q, npair):
    t = jnp.transpose(bias_q, (1, 2, 0)).reshape(npair, 2, BANDP, CHUNK)
    return jnp.transpose(t, (0, 2, 1, 3)).reshape(npair, BANDP, 2 * CHUNK)


def bias_from_lanes(bias_t, npair):
    t = jnp.transpose(bias_t.reshape(npair, BANDP, 2, CHUNK), (3, 0, 2, 1))
    return t.reshape(CHUNK, 2 * npair, BANDP)


def _fill_window(win, prev_ref, cur_ref, A):
    win[0:QBLK, :] = prev_ref[...].astype(BF16)
    win[QBLK:2 * QBLK, :] = cur_ref[...].astype(BF16)
    win[2 * QBLK:2 * QBLK + CHUNK, :] = jnp.zeros((CHUNK, A), BF16)


def attn_fwd(name, sc, proj, bias_buf, ao_buf, A, qcol):
    S = proj.shape[1]
    nb = S // QBLK
    H = A // ATTN_HEAD_DIM
    npair = A // LANES
    scale = ATTN_HEAD_DIM ** -0.5

    def body(sc_ref, q_ref, kp_ref, kc_ref, vp_ref, vc_ref, b_ref, o_ref, kw, vw):
        first = pl.program_id(0) == 0
        _fill_window(kw, kp_ref, kc_ref, A)
        _fill_window(vw, vp_ref, vc_ref, A)

        def chunk(c, carry):
            r0 = pl.multiple_of(c * CHUNK, CHUNK)
            for hp in range(npair):
                cols = slice(hp * LANES, (hp + 1) * LANES)
                q2 = q_ref[pl.ds(r0, CHUNK), cols].astype(BF16)
                k2 = kw[pl.ds(r0, BANDP), cols]
                v2 = vw[pl.ds(r0, BANDP), cols]
                _, p = _attn_probs_t(q2, k2, b_ref[hp], c, first, scale)
                o = lax.dot_general(p.astype(BF16), v2, (((0,), (0,)), ((), ())), preferred_element_type=F32)
                o_ref[pl.ds(r0, CHUNK), cols] = _head_diag(o).astype(BF16)
            return carry

        lax.fori_loop(0, LEFT_CHUNKS, chunk, 0, unroll=2)

    prevb = lambda i: jnp.maximum(i - 1, 0)
    blk = lambda rowf, col: (proj, _bs(proj, (QBLK, A), lambda i, sc: (rowf(i), col), L0))
    same = lambda i: i
    return _pcall(body, name=name, grid=(nb,), sc=sc,
                  ins=[blk(same, qcol), blk(prevb, qcol + 1), blk(same, qcol + 1), blk(prevb, qcol + 2), blk(same, qcol + 2),
                       (bias_buf, pl.BlockSpec((None, npair, BANDP, LANES), lambda i, sc: (sc[0], 0, 0, 0)))],
                  outs=[(ao_buf, _bs(ao_buf, (QBLK, A), lambda i, sc: (i, 0), L0))],
                  scratch=[pltpu.VMEM((2 * QBLK + CHUNK, A), BF16), pltpu.VMEM((2 * QBLK + CHUNK, A), BF16)])


def attn_bwd(name, sc, proj, bias_buf, dao, A, qcol):
    S = proj.shape[1]
    nb = S // QBLK
    H = A // ATTN_HEAD_DIM
    npair = A // LANES
    scale = ATTN_HEAD_DIM ** -0.5
    WIN = 2 * QBLK + CHUNK

    def body(sc_ref, q_ref, kp_ref, kc_ref, vp_ref, vc_ref, b_ref, do_ref, dq_ref, dk_ref, dv_ref, db_ref, kw, vw, dkw, dvw):
        i = pl.program_id(0)
        first = i == 0

        @pl.when(first)
        def _():
            dkw[0:QBLK, :] = jnp.zeros((QBLK, A), F32)
            dvw[0:QBLK, :] = jnp.zeros((QBLK, A), F32)
            db_ref[...] = jnp.zeros_like(db_ref)

        @pl.when(i > 0)
        def _():
            dkw[0:QBLK, :] = dkw[QBLK:2 * QBLK, :]
            dvw[0:QBLK, :] = dvw[QBLK:2 * QBLK, :]

        dkw[QBLK:WIN, :] = jnp.zeros((WIN - QBLK, A), F32)
        dvw[QBLK:WIN, :] = jnp.zeros((WIN - QBLK, A), F32)

        @pl.when(i < nb)
        def _():
            _fill_window(kw, kp_ref, kc_ref, A)
            _fill_window(vw, vp_ref, vc_ref, A)

            def chunk(c, carry):
                r0 = pl.multiple_of(c * CHUNK, CHUNK)
                for hp in range(npair):
                    cols = slice(hp * LANES, (hp + 1) * LANES)
                    q2 = q_ref[pl.ds(r0, CHUNK), cols].astype(BF16)
                    k2 = kw[pl.ds(r0, BANDP), cols]
                    v2 = vw[pl.ds(r0, BANDP), cols]
                    dobd = _head_blocks(do_ref[pl.ds(r0, CHUNK), cols])
                    qbd, p = _attn_probs_t(q2, k2, b_ref[hp], c, first, scale)
                    dp = lax.dot_general(v2, dobd, (((1,), (1,)), ((), ())), preferred_element_type=F32)
                    ds = p * (dp - jnp.sum(dp * p, axis=0, keepdims=True))
                    db_ref[hp] += ds
                    dsb = ds.astype(BF16)
                    dq = lax.dot_general(dsb, k2, (((0,), (0,)), ((), ())), preferred_element_type=F32) * scale
                    dq_ref[pl.ds(r0, CHUNK), cols] = _head_diag(dq).astype(BF16)
                    dkw[pl.ds(r0, BANDP), cols] += jnp.dot(dsb, qbd, preferred_element_type=F32) * scale
                    dvw[pl.ds(r0, BANDP), cols] += jnp.dot(p.astype(BF16), dobd, preferred_element_type=F32)
                return carry

            lax.fori_loop(0, LEFT_CHUNKS, chunk, 0, unroll=2)

        dk_ref[...] = dkw[0:QBLK, :].astype(BF16)
        dv_ref[...] = dvw[0:QBLK, :].astype(BF16)

    cur = lambda i: jnp.minimum(i, nb - 1)
    prevb = lambda i: jnp.maximum(jnp.minimum(i, nb - 1) - 1, 0)
    done = lambda i: jnp.maximum(i - 1, 0)
    blk = lambda rowf, col: (proj, _bs(proj, (QBLK, A), lambda i, sc: (rowf(i), col), L0))
    o_sds = _sds((S, A), BF16)
    return _pcall(body, name=name, grid=(nb + 1,), sc=sc,
                  ins=[blk(cur, qcol), blk(prevb, qcol + 1), blk(cur, qcol + 1), blk(prevb, qcol + 2), blk(cur, qcol + 2),
                       (bias_buf, pl.BlockSpec((None, npair, BANDP, LANES), lambda i, sc: (sc[0], 0, 0, 0))),
                       (dao, pl.BlockSpec((QBLK, A), lambda i, sc: (cur(i), 0)))],
                  outs=[(o_sds, pl.BlockSpec((QBLK, A), lambda i, sc: (cur(i), 0))),
                        (o_sds, pl.BlockSpec((QBLK, A), lambda i, sc: (done(i), 0))),
                        (o_sds, pl.BlockSpec((QBLK, A), lambda i, sc: (done(i), 0))),
                        (_sds((npair, BANDP, LANES), F32), pl.BlockSpec((npair, BANDP, LANES), lambda i, sc: (0, 0, 0)))],
                  scratch=[pltpu.VMEM((WIN, A), BF16), pltpu.VMEM((WIN, A), BF16),
                           pltpu.VMEM((WIN, A), F32), pltpu.VMEM((WIN, A), F32)])


def mem_fwd(name, sc, proj, kv_buf, mo_buf, Dm, qcol):
    S = proj.shape[1]
    NM = kv_buf.shape[1]
    tr = _pick(S, 512, 8)
    hd = Dm // N_MEM_HEADS
    scale = hd ** -0.5

    def body(sc_ref, q_ref, kv_ref, o_ref):
        for h in range(N_MEM_HEADS):
            cols = slice(h * hd, (h + 1) * hd)
            q = q_ref[:, cols].astype(BF16)
            k = kv_ref[:, cols].astype(BF16)
            v = kv_ref[:, Dm + h * hd:Dm + (h + 1) * hd].astype(BF16)
            s = lax.dot_general(q, k, (((1,), (1,)), ((), ())), preferred_element_type=F32) * scale
            p = jnp.exp(s - jnp.max(s, axis=-1, keepdims=True))
            p = p * (1.0 / jnp.sum(p, axis=-1, keepdims=True))
            o_ref[:, cols] = jnp.dot(p.astype(BF16), v, preferred_element_type=F32).astype(BF16)

    return _pcall(body, name=name, grid=(S // tr,), sc=sc,
                  ins=[(proj, _bs(proj, (tr, Dm), lambda i, sc: (i, qcol), L0)),
                       (kv_buf, _bs(kv_buf, (NM, 2 * Dm), lambda i, sc: (0, 0), L0))],
                  outs=[(mo_buf, _bs(mo_buf, (tr, Dm), lambda i, sc: (i, 0), L0))])


def mem_bwd(name, sc, proj, kv_buf, dmo, Dm, qcol):
    S = proj.shape[1]
    NM = kv_buf.shape[1]
    tr = _pick(S, 512, 8)
    hd = Dm // N_MEM_HEADS
    scale = hd ** -0.5

    def body(sc_ref, q_ref, kv_ref, do_ref, dq_ref, dkv_ref):
        i = pl.program_id(0)

        @pl.when(i == 0)
        def _():
            dkv_ref[...] = jnp.zeros_like(dkv_ref)

        for h in range(N_MEM_HEADS):
            cols = slice(h * hd, (h + 1) * hd)
            vcols = slice(Dm + h * hd, Dm + (h + 1) * hd)
            q = q_ref[:, cols].astype(BF16)
            k = kv_ref[:, cols].astype(BF16)
            v = kv_ref[:, vcols].astype(BF16)
            do = do_ref[:, cols]
            s = lax.dot_general(q, k, (((1,), (1,)), ((), ())), preferred_element_type=F32) * scale
            p = jnp.exp(s - jnp.max(s, axis=-1, keepdims=True))
            p = p * (1.0 / jnp.sum(p, axis=-1, keepdims=True))
            dp = lax.dot_general(do, v, (((1,), (1,)), ((), ())), preferred_element_type=F32)
            ds = p * (dp - jnp.sum(dp * p, axis=-1, keepdims=True))
            dsb = ds.astype(BF16)
            dq_ref[:, cols] = (jnp.dot(dsb, k, preferred_element_type=F32) * scale).astype(BF16)
            dkv_ref[:, cols] += lax.dot_general(dsb, q, (((0,), (0,)), ((), ())), preferred_element_type=F32) * scale
            dkv_ref[:, vcols] += lax.dot_general(p.astype(BF16), do, (((0,), (0,)), ((), ())), preferred_element_type=F32)

    return _pcall(body, name=name, grid=(S // tr,), sc=sc,
                  ins=[(proj, _bs(proj, (tr, Dm), lambda i, sc: (i, qcol), L0)),
                       (kv_buf, _bs(kv_buf, (NM, 2 * Dm), lambda i, sc: (0, 0), L0)),
                       (dmo, pl.BlockSpec((tr, Dm), lambda i, sc: (i, 0)))],
                  outs=[(_sds((S, Dm), BF16), pl.BlockSpec((tr, Dm), lambda i, sc: (i, 0))),
                        (_sds((NM, 2 * Dm), F32), pl.BlockSpec((NM, 2 * Dm), lambda i, sc: (0, 0)))])


def _rows2d(a):
    return a.reshape(-1, a.shape[-1])


def _ew(name, fn, ins, out_dtypes):
    R, C = ins[0].shape
    tc = _pick(C, 2048)
    tr = _pick(R, max(8, (1 << 19) // tc), 8)
    n_in = len(ins)

    def body(sc_ref, *refs):
        outs = fn(*[r[...] for r in refs[:n_in]])
        for o_ref, o in zip(refs[n_in:], outs):
            o_ref[...] = o.astype(o_ref.dtype)

    plain = pl.BlockSpec((tr, tc), lambda i, j, sc: (i, j))
    return _pcall(body, name=name, grid=(R // tr, C // tc), sc=jnp.zeros((1,), jnp.int32), ins=[(a, plain) for a in ins],
                  outs=[(_sds((R, C), dt), plain) for dt in out_dtypes], sem=("parallel", "parallel"))


def _ew3(name, fn, sc, dims, ins, outs):
    G, R, C = dims
    tc = _pick(C, 2048)
    tr = _pick(R, max(16, (1 << 19) // tc), 16)
    n_in = len(ins)

    def body(sc_ref, *refs):
        res = fn(*[r[...] for r in refs[:n_in]])
        for o_ref, o in zip(refs[n_in:], res):
            o_ref[...] = o.astype(o_ref.dtype)

    def spec(arr, index):
        nlead = len(arr.shape) - 2

        def imap(g, i, j, s):
            lead, ro, co = index(g, s)
            return (*lead, i + ro // tr, j + co // tc)

        return pl.BlockSpec((None,) * nlead + (tr, tc), imap)

    return _pcall(body, name=name, grid=(G, R // tr, C // tc), sc=sc,
                  ins=[(a, spec(a, ix)) for a, ix in ins], outs=[(o, spec(o, ix)) for o, ix in outs],
                  sem=("parallel", "parallel", "parallel"))


def _at(lead_fn=None, ax=None, size=0):
    def index(g, s):
        lead = (g,) if lead_fn is None else lead_fn(g, s)
        off = s[0] * size
        return lead, (off if ax == 1 else 0), (off if ax == 2 else 0)
    return index


def _adamw_math(w, g, m, v):
    m = ADAM_B1 * m + (1.0 - ADAM_B1) * g
    v = ADAM_B2 * v + (1.0 - ADAM_B2) * (g * g)
    m_hat = m / (1.0 - ADAM_B1 ** ADAM_STEP)
    v_hat = v / (1.0 - ADAM_B2 ** ADAM_STEP)
    delta = -ADAM_LR * (m_hat / (jnp.sqrt(v_hat) + ADAM_EPS) + ADAM_WD * w)
    return delta, m, v


def adamw(name, w, g, m, v):
    shp = w.shape
    d, nm, nv = _ew(name, _adamw_math, [_rows2d(w), _rows2d(g), _rows2d(m), _rows2d(v)], (F32, F32, F32))
    return d.reshape(shp), nm.reshape(shp), nv.reshape(shp)


def _place():
    x, y, c = lax.axis_index("x"), lax.axis_index("y"), lax.axis_index("c")
    chips = [(1 - x, y), (x, 1 - y), (1 - x, 1 - y)]
    return x, y, c, chips


def _sub(ref, ax=None, s=None, n=None, half=None, lh=None):
    lay = slice(None) if half is None else pl.ds(half * lh, lh)
    if ax is None:
        return ref.at[lay]
    cut = pl.ds(pl.multiple_of(s * n, n), n)
    return ref.at[lay, cut, :] if ax == 1 else ref.at[lay, :, cut]


HBM_SPEC = pl.BlockSpec(memory_space=pl.ANY)


def _slab(ref, l, ax, s, half):
    K, N = ref.shape[1], ref.shape[2]
    if ax == 1:
        kh = K // 8
        return ref.at[l, pl.ds(pl.multiple_of((2 * s + half) * kh, kh), kh), :]
    kh, ns = K // 2, N // 4
    return ref.at[l, pl.ds(pl.multiple_of(half * kh, kh), kh), pl.ds(pl.multiple_of(s * ns, ns), ns)]


def _copy_to(view, sems_s, sems_r, k, j, to):
    return pltpu.make_async_remote_copy(src_ref=view, dst_ref=view, send_sem=sems_s.at[k, j], recv_sem=sems_r.at[k, j],
                                        device_id=to, device_id_type=MESH)


def _gather_start(n, view, sems):
    x, y, c, chips = _place()
    for k in range(n):
        for j, (px, py) in enumerate(chips):
            _copy_to(view(k, 2 * x + y, c), sems[0], sems[1], k, j, (px, py, c)).start()


def _gather_finish(n, view, sems):
    x, y, c, chips = _place()
    sibling = (x, y, 1 - c)
    for k in range(n):
        for j, (px, py) in enumerate(chips):
            theirs = view(k, 2 * px + py, c)
            _copy_to(theirs, sems[0], sems[1], k, j, (px, py, c)).wait_recv()
            _copy_to(theirs, sems[2], sems[3], k, j, sibling).start()
    for k in range(n):
        for j, (px, py) in enumerate(chips):
            _copy_to(view(k, 2 * px + py, 1 - c), sems[2], sems[3], k, j, sibling).wait_recv()
    for k in range(n):
        for j, (px, py) in enumerate(chips):
            _copy_to(view(k, 2 * x + y, c), sems[0], sems[1], k, j, (px, py, c)).wait_send()
            _copy_to(view(k, 2 * px + py, c), sems[2], sems[3], k, j, sibling).wait_send()


def _gather_layer_start(refs, axes, l, sems):
    _gather_start(len(refs), lambda k, s, half: _slab(refs[k], l, axes[k], s, half), sems)


def _gather_layer_finish(refs, axes, l, sems):
    _gather_finish(len(refs), lambda k, s, half: _slab(refs[k], l, axes[k], s, half), sems)


def gather_weights(bufs, axes, first_layer_only):
    n = len(bufs)
    lh = bufs[0].shape[0] // 2

    def body(*refs):
        dst = refs[n:2 * n]

        def view(k, s, half):
            if first_layer_only[k]:
                return _slab(dst[k], 0, axes[k], s, half)
            return _sub(dst[k], axes[k], s, dst[k].shape[axes[k]] // 4, half, lh)

        _gather_start(n, view, refs[2 * n:])
        _gather_finish(n, view, refs[2 * n:])

    return pl.pallas_call(
        body, name="gather_weights", out_shape=[_sds(b.shape, b.dtype) for b in bufs],
        in_specs=[HBM_SPEC] * n, out_specs=[HBM_SPEC] * n, input_output_aliases={k: k for k in range(n)},
        scratch_shapes=[pltpu.SemaphoreType.DMA((n, 3)) for _ in range(4)],
    )(*bufs)


def swap_halves(name, grads):
    n = len(grads)
    lh = grads[0].shape[0] // 2

    def body(*refs):
        src, dst = refs[:n], refs[n:2 * n]
        send, recv = refs[2 * n:]
        x, y, c, _ = _place()
        cps = []
        for k in range(n):
            cp = pltpu.make_async_remote_copy(
                src_ref=_sub(src[k], half=1 - c, lh=lh), dst_ref=dst[k], send_sem=send.at[k], recv_sem=recv.at[k],
                device_id=(x, y, 1 - c), device_id_type=MESH)
            cp.start()
            cps.append(cp)
        for cp in cps:
            cp.wait()

    return pl.pallas_call(
        body, name=name, out_shape=[_sds((lh,) + g.shape[1:], g.dtype) for g in grads],
        in_specs=[HBM_SPEC] * n, out_specs=[HBM_SPEC] * n,
        scratch_shapes=[pltpu.SemaphoreType.DMA((n,)), pltpu.SemaphoreType.DMA((n,))],
    )(*grads)


def exchange_chip_sums(name, sums_bf16, axes):
    n = len(sums_bf16)

    def body(*refs):
        sb, got = refs[:n], refs[n:2 * n]
        send, recv = refs[2 * n:]
        x, y, c, chips = _place()
        cps = []
        for k in range(n):
            ax = axes[k]
            ns = got[k].shape[1 + ax]
            for j, (px, py) in enumerate(chips):
                cp = pltpu.make_async_remote_copy(
                    src_ref=_sub(sb[k], ax, 2 * px + py, ns), dst_ref=got[k].at[j],
                    send_sem=send.at[k, j], recv_sem=recv.at[k, j], device_id=(px, py, c), device_id_type=MESH)
                cp.start()
                cps.append(cp)
        for cp in cps:
            cp.wait()

    out_shape = []
    for a, ax in zip(sums_bf16, axes):
        shp = list(a.shape)
        shp[ax] //= 4
        out_shape.append(_sds([3] + shp, BF16))
    return pl.pallas_call(
        body, name=name, out_shape=out_shape, in_specs=[HBM_SPEC] * n, out_specs=[HBM_SPEC] * n,
        scratch_shapes=[pltpu.SemaphoreType.DMA((n, 3)), pltpu.SemaphoreType.DMA((n, 3))],
    )(*sums_bf16)


def join_halves(name, bufs):
    n = len(bufs)
    lh = bufs[0].shape[0] // 2

    def body(*refs):
        dst = refs[n:2 * n]
        send, recv = refs[2 * n:]
        x, y, c, _ = _place()
        cps = []
        for k in range(n):
            mine = _sub(dst[k], half=c, lh=lh)
            cp = pltpu.make_async_remote_copy(
                src_ref=mine, dst_ref=mine, send_sem=send.at[k], recv_sem=recv.at[k],
                device_id=(x, y, 1 - c), device_id_type=MESH)
            cp.start()
            cps.append(cp)
        for k, cp in enumerate(cps):
            cp.wait_send()
            theirs = _sub(dst[k], half=1 - c, lh=lh)
            pltpu.make_async_remote_copy(
                src_ref=theirs, dst_ref=theirs, send_sem=send.at[k], recv_sem=recv.at[k],
                device_id=(x, y, 1 - c), device_id_type=MESH).wait_recv()

    return pl.pallas_call(
        body, name=name, out_shape=[_sds(b.shape, b.dtype) for b in bufs],
        in_specs=[HBM_SPEC] * n, out_specs=[HBM_SPEC] * n, input_output_aliases={k: k for k in range(n)},
        scratch_shapes=[pltpu.SemaphoreType.DMA((n,)), pltpu.SemaphoreType.DMA((n,))],
    )(*bufs)


def allreduce_small(name, packed):
    R = packed.shape[0]

    def body(x_ref, o_ref, buf, send, recv, loc):
        x, y, c, _ = _place()
        me = 4 * x + 2 * y + c
        cps = [pltpu.make_async_copy(x_ref, buf.at[me], loc)]
        cps[0].start()
        for d in range(1, 8):
            px, py, pc = x ^ (d >> 2), y ^ ((d >> 1) & 1), c ^ (d & 1)
            cp = pltpu.make_async_remote_copy(
                src_ref=x_ref, dst_ref=buf.at[me], send_sem=send.at[d - 1], recv_sem=recv.at[d - 1],
                device_id=(px, py, pc), device_id_type=MESH)
            cp.start()
            cps.append(cp)
        for cp in cps:
            cp.wait()
        acc = buf[0]
        for s in range(1, 8):
            acc = acc + buf[s]
        o_ref[...] = acc

    return pl.pallas_call(
        body, name=name, out_shape=_sds((R, LANES), F32),
        in_specs=[pl.BlockSpec(memory_space=pltpu.VMEM)], out_specs=pl.BlockSpec(memory_space=pltpu.VMEM),
        scratch_shapes=[pltpu.VMEM((8, R, LANES), F32), pltpu.SemaphoreType.DMA((7,)), pltpu.SemaphoreType.DMA((7,)),
                        pltpu.SemaphoreType.DMA],
        compiler_params=pltpu.CompilerParams(vmem_limit_bytes=VMEM_LIMIT),
    )(packed)


def _pack(arrs):
    flat = jnp.concatenate([a.reshape(-1) for a in arrs])
    pad = (-flat.shape[0]) % (8 * LANES)
    return jnp.pad(flat, (0, pad)).reshape(-1, LANES)


def _unpack(packed, like):
    flat = packed.reshape(-1)
    out, pos = [], 0
    for a in like:
        out.append(flat[pos:pos + a.size].reshape(a.shape))
        pos += a.size
    return out


def kernel(x, mem, mix_norm_g, mem_norm_g, w_in, gate_b, conv_w, conv_b, conv_ln_g, conv_ln_b, w_conv_out, rel_bias, w_attn_out, w_mem_kv, w_mem_out, w_o, ffn_norm_g, w_up, ffn_conv_w, ffn_conv_b, w_down, final_norm_g, loss_target, m_mix_norm_g, m_mem_norm_g, m_w_in, m_gate_b, m_conv_w, m_conv_b, m_conv_ln_g, m_conv_ln_b, m_w_conv_out, m_rel_bias, m_w_attn_out, m_w_mem_kv, m_w_mem_out, m_w_o, m_ffn_norm_g, m_w_up, m_ffn_conv_w, m_ffn_conv_b, m_w_down, m_final_norm_g, v_mix_norm_g, v_mem_norm_g, v_w_in, v_gate_b, v_conv_w, v_conv_b, v_conv_ln_g, v_conv_ln_b, v_w_conv_out, v_rel_bias, v_w_attn_out, v_w_mem_kv, v_w_mem_out, v_w_o, v_ffn_norm_g, v_w_up, v_ffn_conv_w, v_ffn_conv_b, v_w_down, v_final_norm_g):
    S, D = x.shape[1], x.shape[2]
    NM = mem.shape[1]
    L = w_in.shape[0]
    C = conv_b.shape[1]
    A = w_attn_out.shape[1]
    Dm = w_mem_out.shape[1]
    Fh = w_down.shape[1] * 4
    D_IN = w_in.shape[2] * 4
    CW = conv_w.shape[1]
    FW = ffn_conv_w.shape[1]
    H = A // ATTN_HEAD_DIM
    assert 2 * C == 2 * A == 2 * Dm == D and D_IN == 6 * D and S % QBLK == 0 and L % 2 == 0
    xi, yi, ci = lax.axis_index("x"), lax.axis_index("y"), lax.axis_index("c")
    chip = 2 * xi + yi
    place_sc = jnp.stack([chip, ci]).astype(jnp.int32)
    zero_sc = jnp.zeros((1,), jnp.int32)
    lh = L // 2

    big = [w_in, w_conv_out, w_attn_out, w_mem_kv, w_mem_out, w_o, w_up, w_down]
    big_m = [m_w_in, m_w_conv_out, m_w_attn_out, m_w_mem_kv, m_w_mem_out, m_w_o, m_w_up, m_w_down]
    big_v = [v_w_in, v_w_conv_out, v_w_attn_out, v_w_mem_kv, v_w_mem_out, v_w_o, v_w_up, v_w_down]
    big_ax = [2, 2, 2, 1, 2, 1, 2, 1]
    placed = []
    for k, (w, ax, dt) in enumerate(zip(big + [conv_w, ffn_conv_w], big_ax + [2, 2], [BF16] * len(big) + [F32, F32])):
        shp = list(w.shape)
        shp[ax] *= 4
        placed.append(_ew3(f"place_w{k}", lambda a: (a,), place_sc, w.shape, [(w, _at())],
                           [(_sds(shp, dt), _at(ax=ax, size=w.shape[ax]))])[0])
    gathered = gather_weights(placed, big_ax + [2, 2], [True] * len(big) + [False, False])
    wnames = ["in", "co", "ao", "kv", "mo", "o", "up", "dn"]
    Wd = dict(zip(wnames, gathered[:len(big)]))
    w_axis = dict(zip(wnames, big_ax))
    conv_w_f, ffn_conv_w_f = gathered[len(big):]
    hosted_by_in = ["up", "dn", "o"]
    hosted_by_up = ["in", "co", "ao", "kv", "mo"]

    r3 = lambda a: a.reshape(L, 1, a.shape[-1])
    mix_g3, mem_g3, ffn_g3, gb3 = r3(mix_norm_g), r3(mem_norm_g), r3(ffn_norm_g), r3(gate_b)
    cb3, lg3, lb3, fb3 = r3(conv_b), r3(conv_ln_g), r3(conv_ln_b), r3(ffn_conv_b)
    cw_p = jnp.pad(conv_w_f, ((0, 0), (0, CONV_HALO - CW), (0, 0)))
    fw_p = jnp.pad(ffn_conv_w_f, ((0, 0), (0, FFN_HALO - FW), (0, 0)))
    rel_p = jnp.pad(rel_bias, ((0, 0), (0, 0), (0, REL_PAD - rel_bias.shape[2])))

    x2, mem2, tgt2 = x[0], mem[0], loss_target[0]
    empty = lambda shape, dt: lax.empty(shape, dt)

    def layer_scalars(l):
        i32 = lambda *v: jnp.stack(v).astype(jnp.int32)
        return i32(l, 2 * l, 2 * l + 1), i32(l, 2 * l), i32(l, 2 * l + 1), i32(l, l)

    saved = dict(
        XN=empty((2 * L, S, D), BF16), HS=empty((2 * L, S, D), F32), PROJ=empty((L, S, D_IN), BF16),
        HC=empty((L, S, C), F32), CACT=empty((L, S, C), BF16), AO=empty((L, S, A), BF16), MO=empty((L, S, Dm), BF16),
        Y=empty((L, 3, S, D), BF16), MERGED=empty((L, S, D), BF16), UP=empty((L, S, 2 * Fh), F32),
        ACT=empty((L, S, Fh), BF16), MN=empty((L, NM, D), BF16), MEMS=empty((L, NM, D), F32),
        KV=empty((L, NM, 2 * Dm), F32), BIAS=empty((L, A // LANES, BANDP, LANES), F32))

    def fwd_layer(l, carry):
        h, sv, Wd = carry
        sv, Wd = dict(sv), dict(Wd)
        W_in, W_co, W_ao, W_kv, W_mo, W_o, W_up, W_dn = [Wd[nm] for nm in wnames]
        sc, sc_a, sc_f, sc_m = layer_scalars(l)
        sv["XN"], sv["HS"] = rms_fwd("rms_mix", sc_a, h, mix_g3, sv["XN"], sv["HS"])
        sv["PROJ"], *got = mm_nn("mm_in", sc_a, sv["XN"], L1, W_in, S, D, D_IN, sv["PROJ"], L0, out_dtype=BF16,
                                 gather=([Wd[nm] for nm in hosted_by_in], [w_axis[nm] for nm in hosted_by_in]))
        Wd.update(zip(hosted_by_in, got))
        W_o, W_dn = Wd["o"], Wd["dn"]
        sv["HC"], sv["CACT"] = conv_fwd("conv_fwd", sc, sv["PROJ"], cw_p, cb3, lg3, lb3, sv["HC"], sv["CACT"], C, CW)
        bias_q = bias_expand("bias_expand", sc, rel_p, H)
        sv["BIAS"] = lax.dynamic_update_slice(sv["BIAS"], bias_to_lanes(bias_q, A // LANES)[None], (l, 0, 0, 0))
        sv["AO"], = attn_fwd("attn_fwd", sc, sv["PROJ"], sv["BIAS"], sv["AO"], A, 2)
        sv["MN"], sv["MEMS"] = rms_fwd("rms_mem", sc_m, mem2, mem_g3, sv["MN"], sv["MEMS"])
        sv["KV"] = mm_nn("mm_kv", sc, sv["MN"], L0, W_kv, NM, D, 2 * Dm, sv["KV"], L0)
        sv["MO"], = mem_fwd("mem_fwd", sc, sv["PROJ"], sv["KV"], sv["MO"], Dm, 5)
        for b, (src, w, nm) in enumerate(((sv["CACT"], W_co, "mm_co"), (sv["AO"], W_ao, "mm_ao"), (sv["MO"], W_mo, "mm_mo"))):
            sv["Y"] = mm_nn(nm, sc, src, L0, w, S, C, D, sv["Y"], lambda s, b=b: (s[0], b), out_dtype=BF16)
        sv["MERGED"], = gate_fwd("gate_fwd", sc, sv["PROJ"], gb3, sv["Y"], sv["MERGED"], D)
        h2 = mm_nn("mm_o", sc, sv["MERGED"], L0, W_o, S, D, D, _sds((S, D), F32), add=h)
        sv["XN"], sv["HS"] = rms_fwd("rms_ffn", sc_f, h2, ffn_g3, sv["XN"], sv["HS"])
        sv["UP"], *got = mm_nn("mm_up", sc_f, sv["XN"], L1, Wd["up"], S, D, 2 * Fh, sv["UP"], L0,
                               gather=([Wd[nm] for nm in hosted_by_up], [w_axis[nm] for nm in hosted_by_up]))
        Wd.update(zip(hosted_by_up, got))
        sv["ACT"], = ffn_fwd("ffn_fwd", sc, sv["UP"], fw_p, fb3, sv["ACT"], Fh, FW)
        h3 = mm_nn("mm_down", sc, sv["ACT"], L0, W_dn, S, Fh, D, _sds((S, D), F32), add=h2)
        return h3, sv, Wd

    h_last, saved, Wd = lax.fori_loop(0, L, fwd_layer, (x2, saved, Wd))
    W_in, W_co, W_ao, W_kv, W_mo, W_o, W_up, W_dn = [Wd[nm] for nm in wnames]
    loss_t, dh, dhb, d_final_g = final_loss("final_loss", zero_sc, h_last, final_norm_g.reshape(1, D), tgt2)
    loss = lax.psum(loss_t[0, 0], ("x", "y", "c"))

    zeros = lambda shape: jnp.zeros(shape, F32)
    grads = dict(
        w_in=empty((L, D, D_IN), F32), w_conv_out=empty((L, C, D), F32), w_attn_out=empty((L, A, D), F32),
        w_mem_kv=empty((L, D, 2 * Dm), F32), w_mem_out=empty((L, Dm, D), F32), w_o=empty((L, D, D), F32),
        w_up=empty((L, D, 2 * Fh), F32), w_down=empty((L, Fh, D), F32),
        mix_g=zeros((L, 1, D)), mem_g=zeros((L, 1, D)), ffn_g=zeros((L, 1, D)), gate_b=zeros((L, 1, 3 * D)),
        conv_w=zeros((L, CONV_HALO, C)), conv_b=zeros((L, 1, C)), ln_g=zeros((L, 1, C)), ln_b=zeros((L, 1, C)),
        rel=zeros((L, H, REL_PAD)), fw_v=zeros((L, FFN_HALO, Fh)), fw_g=zeros((L, FFN_HALO, Fh)),
        fb_v=zeros((L, 1, Fh)), fb_g=zeros((L, 1, Fh)))
    zero_mem = jnp.zeros((NM, D), F32)

    def bwd_layer(it, carry):
        dh, dhb, g = carry
        g = dict(g)
        l = L - 1 - it
        sc, sc_a, sc_f, sc_m = layer_scalars(l)
        sv = saved
        dact = mm_nt("mm_down_dx", sc, [(dhb, None)], W_dn, S, Fh, _sds((S, Fh), F32))
        g["w_down"] = mm_tn("mm_down_dw", sc, sv["ACT"], L0, dhb, None, S, Fh, D, g["w_down"], L0)
        dupv, dupg, g["fw_v"], g["fw_g"], g["fb_v"], g["fb_g"] = ffn_bwd(
            "ffn_bwd", sc, sv["UP"], dact, fw_p, fb3, g["fw_v"], g["fw_g"], g["fb_v"], g["fb_g"], Fh, FW)
        dhn = mm_nt("mm_up_dx", sc, [(dupv, None), (dupg, None)], W_up, S, D, _sds((S, D), F32), tk=2816)
        g["w_up"] = mm_tn("mm_up_dw_v", sc_f, sv["XN"], L1, dupv, None, S, D, Fh, g["w_up"], L0)
        g["w_up"] = mm_tn("mm_up_dw_g", sc_f, sv["XN"], L1, dupg, None, S, D, Fh, g["w_up"], L0, out_joff=Fh)
        dh2, dh2b, g["ffn_g"] = rms_bwd("rms_ffn_bwd", sc_f, sv["HS"], ffn_g3, dhn, dh, g["ffn_g"])
        dmerged = mm_nt("mm_o_dx", sc, [(dh2b, None)], W_o, S, D, _sds((S, D), F32))
        g["w_o"] = mm_tn("mm_o_dw", sc, sv["MERGED"], L0, dh2b, None, S, D, D, g["w_o"], L0)
        dy, dgates, g["gate_b"] = gate_bwd("gate_bwd", sc, sv["PROJ"], gb3, sv["Y"], dmerged, g["gate_b"], D)
        dcact = mm_nt("mm_co_dx", sc, [(dy, lambda s: (0,))], W_co, S, C, _sds((S, C), F32))
        dao = mm_nt("mm_ao_dx", sc, [(dy, lambda s: (1,))], W_ao, S, A, _sds((S, A), BF16), out_dtype=BF16)
        dmo = mm_nt("mm_mo_dx", sc, [(dy, lambda s: (2,))], W_mo, S, Dm, _sds((S, Dm), BF16), out_dtype=BF16)
        g["w_conv_out"] = mm_tn("mm_co_dw", sc, sv["CACT"], L0, dy, lambda s: (0,), S, C, D, g["w_conv_out"], L0)
        g["w_attn_out"] = mm_tn("mm_ao_dw", sc, sv["AO"], L0, dy, lambda s: (1,), S, A, D, g["w_attn_out"], L0)
        g["w_mem_out"] = mm_tn("mm_mo_dw", sc, sv["MO"], L0, dy, lambda s: (2,), S, Dm, D, g["w_mem_out"], L0)
        dhc, g["ln_g"], g["ln_b"], g["conv_b"] = conv_bwd1(
            "conv_bwd1", sc, sv["HC"], dcact, lg3, lb3, g["ln_g"], g["ln_b"], g["conv_b"])
        dconv, g["conv_w"] = conv_bwd2("conv_bwd2", sc, sv["PROJ"], dhc, cw_p, g["conv_w"], C, CW)
        dq, dk, dv, dbias = attn_bwd("attn_bwd", sc, sv["PROJ"], sv["BIAS"], dao, A, 2)
        g["rel"] = bias_reduce("bias_reduce", sc, bias_from_lanes(dbias, A // LANES), g["rel"], H)
        dqm, dkv = mem_bwd("mem_bwd", sc, sv["PROJ"], sv["KV"], dmo, Dm, 5)
        g["w_mem_kv"] = mm_tn("mm_kv_dw", sc, sv["MN"], L0, dkv, None, NM, D, 2 * Dm, g["w_mem_kv"], L0)
        dmn = mm_nt("mm_kv_dx", sc, [(dkv, None)], W_kv, NM, D, _sds((NM, D), F32))
        _, _, g["mem_g"] = rms_bwd("rms_mem_bwd", sc_m, sv["MEMS"], mem_g3, dmn, zero_mem, g["mem_g"])
        pieces = [(dconv, None), (dq, None), (dk, None), (dv, None), (dqm, None), (dgates, None)]
        dxn = mm_nt("mm_in_dx", sc, pieces, W_in, S, D, _sds((S, D), F32), tk=1024)
        off = 0
        for nm, (p, _) in zip(("c", "q", "k", "v", "m", "g"), pieces):
            g["w_in"] = mm_tn("mm_in_dw_" + nm, sc_a, sv["XN"], L1, p, None, S, D, p.shape[1], g["w_in"], L0, out_joff=off)
            off += p.shape[1]
        dh0, dh0b, g["mix_g"] = rms_bwd("rms_mix_bwd", sc_a, sv["HS"], mix_g3, dxn, dh2, g["mix_g"])
        return dh0, dh0b, g

    grad_x2, _, grads = lax.fori_loop(0, L, bwd_layer, (dh, dhb, grads))

    names = ["w_in", "w_conv_out", "w_attn_out", "w_mem_kv", "w_mem_out", "w_o", "w_up", "w_down"]
    gl = [grads[nm] for nm in names]
    from_sib = swap_halves("swap_halves", gl)
    my_layer = lambda g_, s: (s[1] * lh + g_,)
    sums_f, sums_b = [], []
    for k, (g_, r_) in enumerate(zip(gl, from_sib)):
        s_f, s_b = _ew3(f"sum_sib{k}", lambda a, b: (a + b, a + b), place_sc, r_.shape,
                        [(g_, _at(my_layer)), (r_, _at())], [(_sds(r_.shape, F32), _at()), (_sds(r_.shape, BF16), _at())])
        sums_f.append(s_f)
        sums_b.append(s_b)
    got = exchange_chip_sums("exchange_chip_sums", sums_b, big_ax)
    halves = []
    for k, (s_f, r_, w, ax) in enumerate(zip(sums_f, got, big, big_ax)):
        fin, = _ew3(f"sum_chips{k}", lambda a, b0, b1, b2: (((a + b0.astype(F32)) + b1.astype(F32)) + b2.astype(F32),),
                    place_sc, (lh,) + w.shape[1:],
                    [(s_f, _at(ax=ax, size=w.shape[ax]))] + [(r_, _at(lambda g_, s, j=j: (j, g_))) for j in range(3)],
                    [(_sds(w.shape, F32), _at(my_layer))])
        halves.append(fin)
    big_g = join_halves("join_halves", halves)
    big_d, big_nm, big_nv = [], [], []
    for k in range(len(big)):
        d_, m_, v_ = adamw(f"adamw_big{k}", big[k], big_g[k], big_m[k], big_v[k])
        big_d.append(d_), big_nm.append(m_), big_nv.append(v_)

    g_small_full = [
        grads["mix_g"].reshape(L, D), grads["mem_g"].reshape(L, D), grads["gate_b"].reshape(L, 3 * D),
        grads["conv_w"][:, :CW, :], grads["conv_b"].reshape(L, C), grads["ln_g"].reshape(L, C), grads["ln_b"].reshape(L, C),
        grads["rel"][:, :, :rel_bias.shape[2]], grads["ffn_g"].reshape(L, D),
        jnp.concatenate([grads["fw_v"][:, :FW, :], grads["fw_g"][:, :FW, :]], axis=-1),
        jnp.concatenate([grads["fb_v"], grads["fb_g"]], axis=-1).reshape(L, 2 * Fh), d_final_g.reshape(D)]
    summed = _unpack(allreduce_small("allreduce_small", _pack(g_small_full)), g_small_full)
    cws, fws = conv_w.shape[2], ffn_conv_w.shape[2]
    summed[3] = lax.dynamic_slice_in_dim(summed[3], chip * cws, cws, axis=2)
    summed[9] = lax.dynamic_slice_in_dim(summed[9], chip * fws, fws, axis=2)
    small_w = [mix_norm_g, mem_norm_g, gate_b, conv_w, conv_b, conv_ln_g, conv_ln_b, rel_bias, ffn_norm_g, ffn_conv_w, ffn_conv_b, final_norm_g]
    small_m = [m_mix_norm_g, m_mem_norm_g, m_gate_b, m_conv_w, m_conv_b, m_conv_ln_g, m_conv_ln_b, m_rel_bias, m_ffn_norm_g, m_ffn_conv_w, m_ffn_conv_b, m_final_norm_g]
    small_v = [v_mix_norm_g, v_mem_norm_g, v_gate_b, v_conv_w, v_conv_b, v_conv_ln_g, v_conv_ln_b, v_rel_bias, v_ffn_norm_g, v_ffn_conv_w, v_ffn_conv_b, v_final_norm_g]
    sd, sm, sv_ = _ew("adamw_small", _adamw_math, [_pack(small_w), _pack(summed), _pack(small_m), _pack(small_v)], (F32, F32, F32))
    small_d, small_nm, small_nv = _unpack(sd, small_w), _unpack(sm, small_w), _unpack(sv_, small_w)

    order = ["mix_norm_g", "mem_norm_g", "w_in", "gate_b", "conv_w", "conv_b", "conv_ln_g", "conv_ln_b", "w_conv_out",
             "rel_bias", "w_attn_out", "w_mem_kv", "w_mem_out", "w_o", "ffn_norm_g", "w_up", "ffn_conv_w", "ffn_conv_b",
             "w_down", "final_norm_g"]
    small_names = ["mix_norm_g", "mem_norm_g", "gate_b", "conv_w", "conv_b", "conv_ln_g", "conv_ln_b", "rel_bias",
                   "ffn_norm_g", "ffn_conv_w", "ffn_conv_b", "final_norm_g"]

    def collect(bigs, smalls):
        table = dict(zip(names, bigs))
        table.update(zip(small_names, smalls))
        return [table[nm] for nm in order]

    return (loss, grad_x2[None], *collect(big_g, summed), *collect(big_d, small_d), *collect(big_nm, small_nm),
            *collect(big_nv, small_nv))
```
